```python
import math, functools
import jax, jax.numpy as jnp
from jax import lax
import numpy as np

D_MODEL = 1024
BATCH = 2
SEQ = 8192
DEPTH = 2
DEC_BATCH = 32
DEC_SEQ = 4
PAST_LEN = 8192
PAGE_SIZE = 128

D_S5 = 256
D_ATT = 512
D_RW = 256
D_MIX = D_S5 + D_ATT + D_RW
S5_GROUP = 16
S5_GROUPS = D_S5 // S5_GROUP
S5_STATE = 64
S5_DT_MIN = 1e-3
S5_DT_MAX = 1e-1
HEAD_DIM = 64
N_ATT_HEADS = D_ATT // HEAD_DIM
N_IDX_HEADS = 4
IDX_DIM = 64
TOPK_MAX = 256
Q_BLOCK = 128
ROPE_THETA = 10000.0
N_RW_HEADS = D_RW // HEAD_DIM
RW_W_RANK = 32
RW_A_RANK = 32
RW_G_RANK = 64
RW_GN_EPS = 64e-5
N_RW_COLS = 3 * D_RW + RW_W_RANK + RW_A_RANK + RW_G_RANK
D_FF = 2816
HALF_STEP = 0.5
RMS_EPS = 1e-6
N_SUB = 3
NEG_INF = -1e30
OFF_S5 = 0
OFF_Q = OFF_S5 + D_S5
OFF_K = OFF_Q + D_ATT
OFF_V = OFF_K + D_ATT
OFF_QI = OFF_V + D_ATT
OFF_KI = OFF_QI + N_IDX_HEADS * IDX_DIM
OFF_WI = OFF_KI + IDX_DIM
OFF_RW = OFF_WI + N_IDX_HEADS
N_IN = OFF_RW + N_RW_COLS

kernel_name = 'hymba_s5_dsa_rwkv7_macaron_step'

F32 = jnp.float32


def rmsnorm(x, g):
    xf = x.astype(F32)
    return xf * lax.rsqrt(jnp.mean(xf * xf, -1, keepdims=True) + RMS_EPS) * g.astype(F32)


def rotary(x, pos):
    half = x.shape[-1] // 2
    inv = ROPE_THETA ** (-jnp.arange(half, dtype=F32) / half)
    ang = pos.astype(F32)[:, None] * inv[None, :]
    shape = (1, pos.shape[0]) + (1,) * (x.ndim - 3) + (half,)
    cos = jnp.cos(ang).reshape(shape)
    sin = jnp.sin(ang).reshape(shape)
    x1 = x[..., :half].astype(F32)
    x2 = x[..., half:].astype(F32)
    return jnp.concatenate([x1 * cos - x2 * sin, x1 * sin + x2 * cos], -1)


def swiglu(h, wi, wo):
    gate, up = jnp.split(h @ wi, 2, axis=-1)
    return (jax.nn.silu(gate) * up) @ wo


def s5_mixer(u, h0_re, h0_im, a_re, a_im, log_dt, b_re, b_im, c_re, c_im, d, glu_w, glu_b):
    Bsz, S, _ = u.shape
    uf = u.astype(F32).reshape(Bsz, S, S5_GROUPS, S5_GROUP)
    ar, ai = a_re.astype(F32), a_im.astype(F32)
    dt = jnp.exp(log_dt.astype(F32))[:, None]
    mag = jnp.exp(ar * dt)
    abar_re, abar_im = mag * jnp.cos(ai * dt), mag * jnp.sin(ai * dt)
    den = ar * ar + ai * ai
    nr, ni = abar_re - 1.0, abar_im
    coef_re = (nr * ar + ni * ai) / den
    coef_im = (ni * ar - nr * ai) / den
    br, bi = b_re.astype(F32), b_im.astype(F32)
    bbar_re = coef_re[..., None] * br - coef_im[..., None] * bi
    bbar_im = coef_re[..., None] * bi + coef_im[..., None] * br
    bu_re = jnp.einsum('bsgh,gph->bsgp', uf, bbar_re)
    bu_im = jnp.einsum('bsgh,gph->bsgp', uf, bbar_im)
    h0r, h0i = h0_re.astype(F32), h0_im.astype(F32)
    bu_re = bu_re.at[:, 0].add(abar_re * h0r - abar_im * h0i)
    bu_im = bu_im.at[:, 0].add(abar_re * h0i + abar_im * h0r)
    a_rb = jnp.broadcast_to(abar_re, bu_re.shape)
    a_ib = jnp.broadcast_to(abar_im, bu_im.shape)

    def combine(e1, e2):
        a1r, a1i, b1r, b1i = e1
        a2r, a2i, b2r, b2i = e2
        return (a2r * a1r - a2i * a1i, a2r * a1i + a2i * a1r,
                a2r * b1r - a2i * b1i + b2r, a2r * b1i + a2i * b1r + b2i)

    _, _, hr, hi = lax.associative_scan(combine, (a_rb, a_ib, bu_re, bu_im), axis=1)
    y = (jnp.einsum('bsgp,ghp->bsgh', hr, c_re.astype(F32))
         - jnp.einsum('bsgp,ghp->bsgh', hi, c_im.astype(F32)))
    y = y.reshape(Bsz, S, D_S5) + d.astype(F32) * u.astype(F32)
    y = jax.nn.gelu(y)
    y = y * jax.nn.sigmoid(y @ glu_w.astype(F32) + glu_b.astype(F32))
    return y, hr[:, -1], hi[:, -1]


def rwkv7_mixer(cols, shift0, state0, mu, w0, w2, a0, a2, g2, k_k, k_a, r_k, ln_w, ln_b):
    Bsz, S, _ = cols.shape
    cf = cols.astype(F32)
    prev = jnp.concatenate([shift0.astype(F32)[:, None], cf[:, :-1]], 1)
    xs = cf + (prev - cf) * mu.astype(F32)
    r, k, v, wl, al, gl = jnp.split(
        xs, [D_RW, 2 * D_RW, 3 * D_RW, 3 * D_RW + RW_W_RANK, 3 * D_RW + RW_W_RANK + RW_A_RANK], axis=-1)
    w = -jax.nn.softplus(-(w0.astype(F32) + jnp.tanh(wl) @ w2.astype(F32))) - 0.5
    decay = jnp.exp(-jnp.exp(w))
    a = jax.nn.sigmoid(a0.astype(F32) + al @ a2.astype(F32))
    g = jax.nn.sigmoid(gl) @ g2.astype(F32)
    hs = lambda t: t.reshape(Bsz, S, N_RW_HEADS, HEAD_DIM)
    hp = lambda t: t.astype(F32).reshape(N_RW_HEADS, HEAD_DIM)
    r, k, v, decay, a = hs(r), hs(k), hs(v), hs(decay), hs(a)
    kk = k * hp(k_k)
    kk = kk / jnp.maximum(jnp.sqrt(jnp.sum(kk * kk, -1, keepdims=True)), 1e-12)
    k = k * (1.0 + (a - 1.0) * hp(k_a))

    def step(st, inp):
        r_t, w_t, k_t, v_t, kk_t, a_t = inp
        sk = jnp.einsum('bhvk,bhk->bhv', st, kk_t)
        st = (st * w_t[:, :, None, :] - sk[..., None] * (kk_t * a_t)[:, :, None, :]
              + v_t[..., None] * k_t[:, :, None, :])
        return st, jnp.einsum('bhvk,bhk->bhv', st, r_t)

    seq = tuple(jnp.moveaxis(t, 1, 0) for t in (r, decay, k, v, kk, a))
    state_last, y = lax.scan(step, state0.astype(F32), seq)
    y = jnp.moveaxis(y, 0, 1)
    mean = jnp.mean(y, -1, keepdims=True)
    var = jnp.mean(jnp.square(y - mean), -1, keepdims=True)
    y = ((y - mean) * lax.rsqrt(var + RW_GN_EPS)).reshape(Bsz, S, D_RW) * ln_w.astype(F32) + ln_b.astype(F32)
    bonus = jnp.sum(r * k * hp(r_k), -1, keepdims=True) * v
    y = (y + bonus.reshape(Bsz, S, D_RW)) * g
    return y, cf[:, -1], state_last


def indexer_scores(qi, wi, ki):
    s = jax.nn.relu(jnp.einsum('bqhd,bld->bqhl', qi.astype(F32), ki.astype(F32)) * IDX_DIM ** -0.5)
    return jnp.einsum('bqhl,bqh->bql', s, wi.astype(F32) * N_IDX_HEADS ** -0.5)


def sparse_attend(q, kg, vg, valid):
    s = jnp.einsum('bqhd,bqkhd->bqhk', q.astype(F32), kg.astype(F32)) * HEAD_DIM ** -0.5
    s = jnp.where(valid[:, :, None, :], s, NEG_INF)
    p = jax.nn.softmax(s, axis=-1)
    return jnp.einsum('bqhk,bqkhd->bqhd', p, vg.astype(F32))


def dsa_prompt(q, k, v, qi, ki, wi):
    Bsz, S = q.shape[:2]
    k_sel = min(TOPK_MAX, S // 4)
    nb = S // Q_BLOCK
    blocks = lambda t: jnp.moveaxis(t.reshape((Bsz, nb, Q_BLOCK) + t.shape[2:]), 1, 0)
    key_pos = jnp.arange(S)
    gather = jax.vmap(lambda t, i: t[i])

    def one_block(args):
        qb, qib, wib, start = args
        q_pos = start + jnp.arange(Q_BLOCK)
        causal = key_pos[None, :] <= q_pos[:, None]
        sc = jnp.where(causal[None], indexer_scores(qib, wib, ki), NEG_INF)
        _, idx = lax.top_k(sc, k_sel)
        valid = idx <= q_pos[None, :, None]
        return sparse_attend(qb, gather(k, idx), gather(v, idx), valid)

    out = lax.map(one_block, (blocks(q), blocks(qi), blocks(wi), jnp.arange(nb) * Q_BLOCK))
    return jnp.moveaxis(out, 0, 1).reshape(Bsz, S, D_ATT)


def dsa_sample(q, k, v, qi, ki, wi, ck, cv, cki, page_table):
    Bsz, S = q.shape[:2]
    past = page_table.shape[1] * PAGE_SIZE
    L = past + S
    k_sel = min(TOPK_MAX, L // 4)
    ki_past = cki[page_table].reshape(Bsz, past, IDX_DIM).astype(F32)
    ki_all = jnp.concatenate([ki_past, ki.astype(F32)], 1)
    q_pos = past + jnp.arange(S)
    causal = jnp.arange(L)[None, :] <= q_pos[:, None]
    sc = jnp.where(causal[None], indexer_scores(qi, wi, ki_all), NEG_INF)
    _, idx = lax.top_k(sc, k_sel)
    in_past = (idx < past)[..., None, None]
    pidx = jnp.minimum(idx, past - 1)
    phys = jnp.take_along_axis(page_table, (pidx // PAGE_SIZE).reshape(Bsz, -1), axis=1).reshape(idx.shape)
    off = pidx % PAGE_SIZE
    nidx = jnp.clip(idx - past, 0, S - 1)
    gather = jax.vmap(lambda t, i: t[i])
    kg = jnp.where(in_past, ck[phys, off].astype(F32), gather(k, nidx).astype(F32))
    vg = jnp.where(in_past, cv[phys, off].astype(F32), gather(v, nidx).astype(F32))
    valid = idx <= q_pos[None, :, None]
    return sparse_attend(q, kg, vg, valid).reshape(Bsz, S, D_ATT)


def trunk_layer(x, c, pos, s5_h0_re, s5_h0_im, rw_state0, rw_shift0, attend, p):
    Bsz, S, _ = x.shape
    mod = (jax.nn.silu(c.astype(F32)) @ p['ada_w'] + p['ada_b']).reshape(Bsz, N_SUB, 3, D_MODEL)

    def pre(i, xr):
        return rmsnorm(xr, p['norm_pre'][i]) * (1.0 + mod[:, i, 1, None]) + mod[:, i, 0, None]

    def post(i, xr, o, res_w):
        return xr + res_w * mod[:, i, 2, None] * rmsnorm(o, p['norm_post'][i])

    x = post(0, x, swiglu(pre(0, x), p['ffn_wi'][0], p['ffn_wo'][0]), HALF_STEP)
    z = pre(1, x) @ p['w_in']
    heads = lambda a, b, n, dd: z[..., a:b].reshape(Bsz, S, n, dd)
    y_s5, s5_re, s5_im = s5_mixer(z[..., OFF_S5:OFF_Q], s5_h0_re, s5_h0_im, p['s5_a_re'], p['s5_a_im'],
                                  p['s5_log_dt'], p['s5_b_re'], p['s5_b_im'], p['s5_c_re'], p['s5_c_im'],
                                  p['s5_d'], p['s5_glu_w'], p['s5_glu_b'])
    q = rotary(heads(OFF_Q, OFF_K, N_ATT_HEADS, HEAD_DIM), pos)
    k = rotary(heads(OFF_K, OFF_V, N_ATT_HEADS, HEAD_DIM), pos)
    v = heads(OFF_V, OFF_QI, N_ATT_HEADS, HEAD_DIM).astype(F32)
    qi = rotary(heads(OFF_QI, OFF_KI, N_IDX_HEADS, IDX_DIM), pos)
    ki = rotary(z[..., OFF_KI:OFF_WI], pos)
    wi = z[..., OFF_WI:OFF_RW]
    y_att = attend(q, k, v, qi, ki, wi)
    y_rw, rw_shift, rw_state = rwkv7_mixer(z[..., OFF_RW:], rw_shift0, rw_state0, p['rw_mu'], p['rw_w0'],
                                           p['rw_w2'], p['rw_a0'], p['rw_a2'], p['rw_g2'], p['rw_k_k'],
                                           p['rw_k_a'], p['rw_r_k'], p['rw_ln_w'], p['rw_ln_b'])
    o = jnp.concatenate([y_s5, y_att, y_rw], -1) @ p['w_out']
    x = post(1, x, o, 1.0)
    x = post(2, x, swiglu(pre(2, x), p['ffn_wi'][1], p['ffn_wo'][1]), HALF_STEP)
    return x, (k, v, ki, s5_re, s5_im, rw_state, rw_shift)


def setup_inputs(seed: int = 0) -> dict:
    key = jax.random.key(seed)
    ks = iter(jax.random.split(key, 64))
    nrm = lambda shape, scale: jax.random.normal(next(ks), shape, F32) * scale
    n_pages = PAST_LEN // PAGE_SIZE
    n_used = DEC_BATCH * n_pages
    n_pool = n_used + n_used // 4
    page_table = jax.random.permutation(next(ks), n_pool)[:n_used].reshape(DEC_BATCH, n_pages).astype(jnp.int32)
    G, P = S5_GROUPS, S5_STATE
    lin = jnp.linspace(0.0, 1.0, D_RW, dtype=F32)
    return {
        'x_prompt': nrm((BATCH, SEQ, D_MODEL), 1.0),
        'x_sample': nrm((DEC_BATCH, DEC_SEQ, D_MODEL), 1.0),
        'c_prompt': nrm((BATCH, D_MODEL), 1.0),
        'c_sample': nrm((DEC_BATCH, D_MODEL), 1.0),
        'cache_k': nrm((DEPTH, n_pool, PAGE_SIZE, N_ATT_HEADS, HEAD_DIM), 1.0),
        'cache_v': nrm((DEPTH, n_pool, PAGE_SIZE, N_ATT_HEADS, HEAD_DIM), 1.0),
        'cache_kidx': nrm((DEPTH, n_pool, PAGE_SIZE, IDX_DIM), 1.0),
        'state_s5_re': nrm((DEPTH, DEC_BATCH, G, P), 1.0),
        'state_s5_im': nrm((DEPTH, DEC_BATCH, G, P), 1.0),
        'state_rwkv': nrm((DEPTH, DEC_BATCH, N_RW_HEADS, HEAD_DIM, HEAD_DIM), 0.5),
        'state_rwkv_shift': nrm((DEPTH, DEC_BATCH, N_RW_COLS), 1.0),
        'page_table': page_table,
        'ada_w': nrm((DEPTH, D_MODEL, N_SUB * 3 * D_MODEL), 0.5 * D_MODEL ** -0.5),
        'ada_b': nrm((DEPTH, N_SUB * 3 * D_MODEL), 0.02),
        'norm_pre': 1.0 + nrm((DEPTH, N_SUB, D_MODEL), 0.02),
        'norm_post': 1.0 + nrm((DEPTH, N_SUB, D_MODEL), 0.02),
        'ffn_wi': nrm((DEPTH, 2, D_MODEL, 2 * D_FF), D_MODEL ** -0.5),
        'ffn_wo': nrm((DEPTH, 2, D_FF, D_MODEL), D_FF ** -0.5),
        'w_in': nrm((DEPTH, D_MODEL, N_IN), D_MODEL ** -0.5),
        'w_out': nrm((DEPTH, D_MIX, D_MODEL), D_MIX ** -0.5),
        's5_a_re': -0.5 + nrm((DEPTH, G, P), 0.01),
        's5_a_im': math.pi * jnp.arange(P, dtype=F32) + nrm((DEPTH, G, P), 0.01),
        's5_log_dt': jax.random.uniform(next(ks), (DEPTH, G), F32, math.log(S5_DT_MIN), math.log(S5_DT_MAX)),
        's5_b_re': nrm((DEPTH, G, P, S5_GROUP), (2 * S5_GROUP) ** -0.5),
        's5_b_im': nrm((DEPTH, G, P, S5_GROUP), (2 * S5_GROUP) ** -0.5),
        's5_c_re': nrm((DEPTH, G, S5_GROUP, P), (2 * P) ** -0.5),
        's5_c_im': nrm((DEPTH, G, S5_GROUP, P), (2 * P) ** -0.5),
        's5_d': nrm((DEPTH, D_S5), 1.0),
        's5_glu_w': nrm((DEPTH, D_S5, D_S5), D_S5 ** -0.5),
        's5_glu_b': nrm((DEPTH, D_S5), 0.01),
        'rw_mu': jax.random.uniform(next(ks), (DEPTH, N_RW_COLS), F32),
        'rw_w0': (-5.5 + 5.0 * lin ** 0.85) + nrm((DEPTH, D_RW), 0.05),
        'rw_w2': nrm((DEPTH, RW_W_RANK, D_RW), 0.1),
        'rw_a0': nrm((DEPTH, D_RW), 0.1),
        'rw_a2': nrm((DEPTH, RW_A_RANK, D_RW), 0.1),
        'rw_g2': nrm((DEPTH, RW_G_RANK, D_RW), RW_G_RANK ** -0.5),
        'rw_k_k': 0.85 + nrm((DEPTH, D_RW), 0.02),
        'rw_k_a': 1.0 + nrm((DEPTH, D_RW), 0.02),
        'rw_r_k': nrm((DEPTH, D_RW), 0.1),
        'rw_ln_w': 1.0 + nrm((DEPTH, D_RW), 0.02),
        'rw_ln_b': nrm((DEPTH, D_RW), 0.01),
    }


def _stack(outs, i):
    return jnp.stack([o[i] for o in outs])


def reference(x_prompt, x_sample, c_prompt, c_sample, cache_k, cache_v, cache_kidx, state_s5_re, state_s5_im,
              state_rwkv, state_rwkv_shift, page_table, ada_w, ada_b, norm_pre, norm_post, ffn_wi, ffn_wo, w_in,
              w_out, s5_a_re, s5_a_im, s5_log_dt, s5_b_re, s5_b_im, s5_c_re, s5_c_im, s5_d, s5_glu_w, s5_glu_b,
              rw_mu, rw_w0, rw_w2, rw_a0, rw_a2, rw_g2, rw_k_k, rw_k_a, rw_r_k, rw_ln_w, rw_ln_b):
    Bp, Sp = x_prompt.shape[:2]
    Ss = x_sample.shape[1]
    past = page_table.shape[1] * PAGE_SIZE
    pos_p = jnp.arange(Sp)
    pos_s = past + jnp.arange(Ss)
    zeros_s5 = jnp.zeros((Bp, S5_GROUPS, S5_STATE), F32)
    zeros_rw = jnp.zeros((Bp, N_RW_HEADS, HEAD_DIM, HEAD_DIM), F32)
    zeros_shift = jnp.zeros((Bp, N_RW_COLS), F32)
    yp, ys = x_prompt, x_sample
    outs_p, outs_s = [], []
    for l in range(DEPTH):
        p = {'ada_w': ada_w[l], 'ada_b': ada_b[l], 'norm_pre': norm_pre[l], 'norm_post': norm_post[l],
             'ffn_wi': ffn_wi[l], 'ffn_wo': ffn_wo[l], 'w_in': w_in[l], 'w_out': w_out[l],
             's5_a_re': s5_a_re[l], 's5_a_im': s5_a_im[l], 's5_log_dt': s5_log_dt[l], 's5_b_re': s5_b_re[l],
             's5_b_im': s5_b_im[l], 's5_c_re': s5_c_re[l], 's5_c_im': s5_c_im[l], 's5_d': s5_d[l],
             's5_glu_w': s5_glu_w[l], 's5_glu_b': s5_glu_b[l], 'rw_mu': rw_mu[l], 'rw_w0': rw_w0[l],
             'rw_w2': rw_w2[l], 'rw_a0': rw_a0[l], 'rw_a2': rw_a2[l], 'rw_g2': rw_g2[l], 'rw_k_k': rw_k_k[l],
             'rw_k_a': rw_k_a[l], 'rw_r_k': rw_r_k[l], 'rw_ln_w': rw_ln_w[l], 'rw_ln_b': rw_ln_b[l]}
        yp, sp = trunk_layer(yp, c_prompt, pos_p, zeros_s5, zeros_s5, zeros_rw, zeros_shift, dsa_prompt, p)
        attend_s = functools.partial(dsa_sample, ck=cache_k[l], cv=cache_v[l], cki=cache_kidx[l],
                                     page_table=page_table)
        ys, ss = trunk_layer(ys, c_sample, pos_s, state_s5_re[l], state_s5_im[l], state_rwkv[l],
                             state_rwkv_shift[l], attend_s, p)
        outs_p.append(sp)
        outs_s.append(ss)
    y_prompt = yp.astype(x_prompt.dtype)
    y_sample = ys.astype(x_sample.dtype)
    k_prompt, v_prompt, kidx_prompt = _stack(outs_p, 0), _stack(outs_p, 1), _stack(outs_p, 2)
    s5_re_prompt, s5_im_prompt = _stack(outs_p, 3), _stack(outs_p, 4)
    rwkv_prompt, shift_prompt = _stack(outs_p, 5), _stack(outs_p, 6)
    k_sample, v_sample, kidx_sample = _stack(outs_s, 0), _stack(outs_s, 1), _stack(outs_s, 2)
    s5_re_sample, s5_im_sample = _stack(outs_s, 3), _stack(outs_s, 4)
    rwkv_sample, shift_sample = _stack(outs_s, 5), _stack(outs_s, 6)
    return (y_prompt, y_sample, k_prompt, v_prompt, kidx_prompt, s5_re_prompt, s5_im_prompt, rwkv_prompt,
            shift_prompt, k_sample, v_sample, kidx_sample, s5_re_sample, s5_im_sample, rwkv_sample, shift_sample)
```

```python
import functools
import math

import jax
import jax.numpy as jnp
from jax import lax
from jax.experimental import pallas as pl
from jax.experimental.pallas import tpu as pltpu

F32 = jnp.float32
BF16 = jnp.bfloat16
I32 = jnp.int32

D_MODEL = 1024
PAGE_SIZE = 128
D_S5 = 256
D_ATT = 512
D_RW = 256
S5_GROUP = 16
S5_GROUPS = D_S5 // S5_GROUP
S5_STATE = 64
S5_W = S5_GROUPS * S5_STATE
HEAD_DIM = 64
N_ATT_HEADS = D_ATT // HEAD_DIM
N_IDX_HEADS = 4
IDX_DIM = 64
TOPK_MAX = 256
Q_BLOCK = 128
ROPE_THETA = 10000.0
N_RW_HEADS = D_RW // HEAD_DIM
RW_W_RANK = 32
RW_A_RANK = 32
RW_G_RANK = 64
RW_GN_EPS = 64e-5
N_RW_COLS = 3 * D_RW + RW_W_RANK + RW_A_RANK + RW_G_RANK
D_FF = 2816
HALF_STEP = 0.5
RMS_EPS = 1e-6
N_SUB = 3
NEG_INF = -1e30
OFF_S5 = 0
OFF_Q = OFF_S5 + D_S5
OFF_K = OFF_Q + D_ATT
OFF_V = OFF_K + D_ATT
OFF_QI = OFF_V + D_ATT
OFF_KI = OFF_QI + N_IDX_HEADS * IDX_DIM
OFF_WI = OFF_KI + IDX_DIM
OFF_RW = OFF_WI + N_IDX_HEADS
N_IN = OFF_RW + N_RW_COLS

LANES = 128
VMEM_LIMIT = 56 * 1024 * 1024
INT_MIN = -(2 ** 31)

ZP_U, ZP_Q, ZP_K, ZP_V, ZP_QI, ZP_RW, ZP_KIW, ZP_END = 0, 256, 768, 1280, 1792, 2048, 2944, 3072

RW_CHUNK = 64
S5_CHUNK = 128
KEY_CHUNK = 512


def _cparams(sem):
    return pltpu.CompilerParams(dimension_semantics=sem, vmem_limit_bytes=VMEM_LIMIT)


def _dot(a, b):
    return jnp.dot(a, b, preferred_element_type=F32)


def _dot_nt(a, b):
    return lax.dot_general(a, b, (((1,), (1,)), ((), ())), preferred_element_type=F32)


def _split(x):
    hi = x.astype(BF16)
    lo = (x - hi.astype(F32)).astype(BF16)
    return hi, lo


def _dot3(a, b):
    ah, al = _split(a)
    bh, bl = _split(b)
    return _dot(ah, bh) + (_dot(ah, bl) + _dot(al, bh))


def _dot3_nt(a, b):
    ah, al = _split(a)
    bh, bl = _split(b)
    return _dot_nt(ah, bh) + (_dot_nt(ah, bl) + _dot_nt(al, bh))


def _rms(x, g):
    return x * lax.rsqrt(jnp.mean(x * x, axis=-1, keepdims=True) + RMS_EPS) * g


def _sigmoid(x):
    return 1.0 / (1.0 + jnp.exp(-x))


def _ada_kernel(c_ref, w_ref, b_ref, o_ref):
    c = c_ref[...]
    h = (c * _sigmoid(c)).astype(BF16)
    o_ref[...] = _dot(h, w_ref[...].astype(BF16)) + b_ref[...]


def ada_mod(c_all, ada_w, ada_b, tn=1152):
    depth, d, n = ada_w.shape
    rows = c_all.shape[0]
    return pl.pallas_call(
        _ada_kernel,
        grid=(depth, n // tn),
        in_specs=[pl.BlockSpec((rows, d), lambda l, j: (0, 0)),
                  pl.BlockSpec((None, d, tn), lambda l, j: (l, 0, j)),
                  pl.BlockSpec((None, 1, tn), lambda l, j: (l, 0, j))],
        out_specs=pl.BlockSpec((None, rows, tn), lambda l, j: (l, 0, j)),
        out_shape=jax.ShapeDtypeStruct((depth, rows, n), F32),
        compiler_params=_cparams(("arbitrary", "arbitrary")),
        name="ada_mod",
    )(c_all, ada_w, ada_b.reshape(depth, 1, n))


def _mod_spec(mod, tiles_per_group):
    r = mod.shape[1]
    return pl.BlockSpec((None, r, mod.shape[2]), lambda i: (i // tiles_per_group, 0, 0))


def _ffn_kernel(x_ref, shift_ref, scale_ref, gate_ref, gpre_ref, gpost_ref, wi_ref, wo_ref, o_ref, *, res_w):
    x = x_ref[...]
    h = (_rms(x, gpre_ref[...]) * (1.0 + scale_ref[...]) + shift_ref[...]).astype(BF16)
    g = _dot(h, wi_ref[:, :D_FF])
    u = _dot(h, wi_ref[:, D_FF:])
    a = (g * _sigmoid(g) * u).astype(BF16)
    o = _dot(a, wo_ref[...])
    o_ref[...] = x + (res_w * gate_ref[...]) * _rms(o, gpost_ref[...])


def ffn_block(x, shift, scale, gate, g_pre, g_post, wi, wo, tm, tiles_per_group):
    n, d = x.shape
    row = pl.BlockSpec((tm, d), lambda i: (i, 0))
    const = lambda a: pl.BlockSpec(a.shape, lambda i: (0,) * a.ndim)
    return pl.pallas_call(
        functools.partial(_ffn_kernel, res_w=HALF_STEP),
        grid=(n // tm,),
        in_specs=[row, _mod_spec(shift, tiles_per_group), _mod_spec(scale, tiles_per_group),
                  _mod_spec(gate, tiles_per_group), const(g_pre), const(g_post), const(wi), const(wo)],
        out_specs=row,
        out_shape=jax.ShapeDtypeStruct((n, d), F32),
        compiler_params=_cparams(("arbitrary",)),
        name="ffn_block",
    )(x, shift, scale, gate, g_pre, g_post, wi, wo)


def _rot_block(blk, cos, sin, lane):
    partner = jnp.where((lane & 32) == 0, pltpu.roll(blk, LANES - 32, 1), pltpu.roll(blk, 32, 1))
    return blk * cos + partner * sin


def _inproj_kernel(x_ref, shift_ref, scale_ref, gpre_ref, w_ref, cos_ref, sin_ref,
                   u_ref, q_ref, k_ref, kb_ref, v_ref, vb_ref, qi3_ref, kiw_ref, ki3_ref, rw_ref):
    x = x_ref[...]
    h = (_rms(x, gpre_ref[...]) * (1.0 + scale_ref[...]) + shift_ref[...]).astype(BF16)
    z = _dot(h, w_ref[...])
    cos = cos_ref[...]
    sin = sin_ref[...]
    lane = lax.broadcasted_iota(I32, cos.shape, 1)
    lo_half = lane < 64
    rot = lambda off: _rot_block(z[:, off:off + LANES], cos, sin, lane)

    u_ref[...] = z[:, ZP_U:ZP_Q]
    for j in range(D_ATT // LANES):
        q_ref[:, j * LANES:(j + 1) * LANES] = (rot(ZP_Q + j * LANES) * HEAD_DIM ** -0.5).astype(BF16)
        kr = rot(ZP_K + j * LANES)
        k_ref[:, j * LANES:(j + 1) * LANES] = kr
        kb_ref[:, j * LANES:(j + 1) * LANES] = kr.astype(BF16)
    v = z[:, ZP_V:ZP_QI]
    v_ref[...] = v
    vb_ref[...] = v.astype(BF16)
    for j in range(N_IDX_HEADS * IDX_DIM // LANES):
        qr = rot(ZP_QI + j * LANES)
        hi = qr.astype(BF16).astype(F32)
        lo = qr - hi
        hi_sw = pltpu.roll(hi, 64, 1)
        lo_sw = pltpu.roll(lo, 64, 1)
        zero = jnp.zeros_like(hi)
        for half in range(2):
            base = (2 * j + half) * 2 * LANES
            a, b = (hi, lo) if half == 0 else (hi_sw, lo_sw)
            a_sw = hi_sw if half == 0 else hi
            qi3_ref[:, base:base + LANES] = jnp.where(lo_half, a, a_sw).astype(BF16)
            qi3_ref[:, base + LANES:base + 2 * LANES] = jnp.where(lo_half, b, zero).astype(BF16)
    raw = z[:, ZP_KIW:ZP_END]
    kiw = jnp.where(lo_half, _rot_block(raw, cos, sin, lane), raw)
    kiw_ref[...] = kiw
    hi = kiw.astype(BF16).astype(F32)
    lo = kiw - hi
    ki3_ref[:, :LANES] = jnp.where(lo_half, hi, pltpu.roll(lo, 64, 1)).astype(BF16)
    ki3_ref[:, LANES:] = jnp.where(lo_half, hi, jnp.zeros_like(hi)).astype(BF16)
    rw_ref[...] = z[:, ZP_RW:ZP_KIW]


def inproj_block(x, shift, scale, g_pre, w, cos_t, sin_t, tm, tiles_per_group, pos_tiles):
    n, d = x.shape
    row = lambda width: pl.BlockSpec((tm, width), lambda i: (i, 0))
    const = lambda a: pl.BlockSpec(a.shape, lambda i: (0,) * a.ndim)
    tab = pl.BlockSpec((tm, LANES), lambda i: (i % pos_tiles, 0))
    widths = [(D_S5, F32), (D_ATT, BF16), (D_ATT, F32), (D_ATT, BF16), (D_ATT, F32), (D_ATT, BF16),
              (N_IDX_HEADS * 2 * LANES, BF16), (LANES, F32), (2 * LANES, BF16), (N_RW_COLS, F32)]
    return pl.pallas_call(
        _inproj_kernel,
        grid=(n // tm,),
        in_specs=[row(d), _mod_spec(shift, tiles_per_group), _mod_spec(scale, tiles_per_group),
                  const(g_pre), const(w), tab, tab],
        out_specs=[row(wd) for wd, _ in widths],
        out_shape=[jax.ShapeDtypeStruct((n, wd), dt) for wd, dt in widths],
        compiler_params=_cparams(("arbitrary",)),
        name="inproj_block",
    )(x, shift, scale, g_pre, w, cos_t, sin_t)


def _outproj_kernel(x_ref, ys5_ref, yatt_ref, yrw_ref, gate_ref, gpost_ref, w_ref, o_ref):
    o = (_dot(ys5_ref[...].astype(BF16), w_ref[:D_S5, :])
         + _dot(yatt_ref[...].astype(BF16), w_ref[D_S5:D_S5 + D_ATT, :])
         + _dot(yrw_ref[...].astype(BF16), w_ref[D_S5 + D_ATT:, :]))
    o_ref[...] = x_ref[...] + gate_ref[...] * _rms(o, gpost_ref[...])


def outproj_block(x, y_s5, y_att, y_rw, gate, g_post, w, tm, tiles_per_group):
    n, d = x.shape
    row = lambda width: pl.BlockSpec((tm, width), lambda i: (i, 0))
    const = lambda a: pl.BlockSpec(a.shape, lambda i: (0,) * a.ndim)
    return pl.pallas_call(
        _outproj_kernel,
        grid=(n // tm,),
        in_specs=[row(d), row(D_S5), row(D_ATT), row(D_RW), _mod_spec(gate, tiles_per_group),
                  const(g_post), const(w)],
        out_specs=row(d),
        out_shape=jax.ShapeDtypeStruct((n, d), F32),
        compiler_params=_cparams(("arbitrary",)),
        name="outproj_block",
    )(x, y_s5, y_att, y_rw, gate, g_post, w)


def _gelu_tanh(x):
    return 0.5 * x * (1.0 + jnp.tanh(math.sqrt(2.0 / math.pi) * (x + 0.044715 * (x * x * x))))


def _s5_disc_kernel(are_ref, aim_ref, ldt_ref, arec_ref, aimc_ref, ldtc_ref, bre_ref, bim_ref,
                    powre_ref, powim_ref, bbre_ref, bbim_ref, *, rows):
    def zoh(ar, ai, ldt):
        dt = jnp.exp(ldt)
        mag = jnp.exp(ar * dt)
        abr, abi = mag * jnp.cos(ai * dt), mag * jnp.sin(ai * dt)
        den = ar * ar + ai * ai
        nr, ni = abr - 1.0, abi
        return abr, abi, (nr * ar + ni * ai) / den, (ni * ar - nr * ai) / den

    abr, abi, _, _ = zoh(are_ref[...], aim_ref[...], ldt_ref[...])
    pr = jnp.broadcast_to(abr, (rows, abr.shape[1]))
    pi = jnp.broadcast_to(abi, (rows, abr.shape[1]))
    row = lax.broadcasted_iota(I32, pr.shape, 0)
    d = 1
    while d < rows:
        sr = pltpu.roll(pr, d, 0)
        si = pltpu.roll(pi, d, 0)
        m = row >= d
        pr, pi = jnp.where(m, pr * sr - pi * si, pr), jnp.where(m, pr * si + pi * sr, pi)
        d *= 2
    powre_ref[...] = pr
    powim_ref[...] = pi
    _, _, cr, ci = zoh(arec_ref[...], aimc_ref[...], ldtc_ref[...])
    br, bi = bre_ref[...], bim_ref[...]
    bbre_ref[...] = cr * br - ci * bi
    bbim_ref[...] = cr * bi + ci * br


def s5_discretise(a_re, a_im, log_dt, b_re, b_im, rows):
    g, p = a_re.shape
    w = g * p
    ldt = jnp.broadcast_to(log_dt[:, None], (g, p))
    args = (a_re.reshape(1, w), a_im.reshape(1, w), ldt.reshape(1, w),
            a_re.reshape(w, 1), a_im.reshape(w, 1), ldt.reshape(w, 1),
            b_re.reshape(w, S5_GROUP), b_im.reshape(w, S5_GROUP))
    return pl.pallas_call(
        functools.partial(_s5_disc_kernel, rows=rows),
        out_shape=[jax.ShapeDtypeStruct((rows, w), F32), jax.ShapeDtypeStruct((rows, w), F32),
                   jax.ShapeDtypeStruct((w, S5_GROUP), F32), jax.ShapeDtypeStruct((w, S5_GROUP), F32)],
        name="s5_discretise",
    )(*args)


def _s5_head(y, u, d_ref, gluw_ref, glub_ref):
    y = _gelu_tanh(y + d_ref[...] * u)
    return y * _sigmoid(_dot(y.astype(BF16), gluw_ref[...]) + glub_ref[...])


def _s5_kernel(u_ref, h0_ref, powre_ref, powim_ref, wb_ref, wc_ref, d_ref, gluw_ref, glub_ref,
               y_ref, ht_ref, cre, cim):
    t = pl.program_id(1)
    rows = u_ref.shape[0]

    @pl.when(t == 0)
    def _():
        cre[...] = h0_ref[:, :S5_W]
        cim[...] = h0_ref[:, S5_W:]

    u = u_ref[...]
    bu = _dot(u.astype(BF16), wb_ref[...])
    hr, hi = bu[:, :S5_W], bu[:, S5_W:]
    row = lax.broadcasted_iota(I32, hr.shape, 0)
    d = 1
    while d < rows:
        ar, ai = powre_ref[d - 1:d, :], powim_ref[d - 1:d, :]
        sr, si = pltpu.roll(hr, d, 0), pltpu.roll(hi, d, 0)
        m = row >= d
        hr, hi = (hr + jnp.where(m, ar * sr - ai * si, 0.0), hi + jnp.where(m, ar * si + ai * sr, 0.0))
        d *= 2
    pr, pi = powre_ref[...], powim_ref[...]
    c_r, c_i = cre[...], cim[...]
    hr, hi = hr + (pr * c_r - pi * c_i), hi + (pr * c_i + pi * c_r)
    cre[...] = hr[rows - 1:rows, :]
    cim[...] = hi[rows - 1:rows, :]
    ht_ref[:, :S5_W] = hr[rows - 1:rows, :]
    ht_ref[:, S5_W:] = hi[rows - 1:rows, :]
    y = _dot(hr.astype(BF16), wc_ref[:S5_W, :]) + _dot(hi.astype(BF16), wc_ref[S5_W:, :])
    y_ref[...] = _s5_head(y, u, d_ref, gluw_ref, glub_ref)


def s5_prompt(u, h0, powre, powim, wb, wc, d, gluw, glub):
    b, s, _ = u.shape
    rows = powre.shape[0]
    const = lambda a: pl.BlockSpec(a.shape, lambda i, j: (0,) * a.ndim)
    return pl.pallas_call(
        _s5_kernel,
        grid=(b, s // rows),
        in_specs=[pl.BlockSpec((None, rows, D_S5), lambda i, j: (i, j, 0)),
                  pl.BlockSpec((None, 1, 2 * S5_W), lambda i, j: (i, 0, 0)),
                  const(powre), const(powim), const(wb), const(wc), const(d), const(gluw), const(glub)],
        out_specs=[pl.BlockSpec((None, rows, D_S5), lambda i, j: (i, j, 0)),
                   pl.BlockSpec((None, 1, 2 * S5_W), lambda i, j: (i, 0, 0))],
        out_shape=[jax.ShapeDtypeStruct((b, s, D_S5), F32), jax.ShapeDtypeStruct((b, 1, 2 * S5_W), F32)],
        scratch_shapes=[pltpu.VMEM((1, S5_W), F32), pltpu.VMEM((1, S5_W), F32)],
        compiler_params=_cparams(("arbitrary", "arbitrary")),
        name="s5_prompt",
    )(u, h0, powre, powim, wb, wc, d, gluw, glub)


def _s5_step_kernel(u_ref, h0_ref, powre_ref, powim_ref, wb_ref, wc_ref, d_ref, gluw_ref, glub_ref,
                    y_ref, ht_ref):
    ar, ai = powre_ref[0:1, :], powim_ref[0:1, :]
    hr, hi = h0_ref[:, :S5_W], h0_ref[:, S5_W:]
    for t in range(u_ref.shape[0]):
        u = u_ref[t]
        bu = _dot(u.astype(BF16), wb_ref[...])
        hr, hi = ar * hr - ai * hi + bu[:, :S5_W], ar * hi + ai * hr + bu[:, S5_W:]
        y = _dot(hr.astype(BF16), wc_ref[:S5_W, :]) + _dot(hi.astype(BF16), wc_ref[S5_W:, :])
        y_ref[t] = _s5_head(y, u, d_ref, gluw_ref, glub_ref)
    ht_ref[:, :S5_W] = hr
    ht_ref[:, S5_W:] = hi


def s5_sample(u_tm, h0, powre, powim, wb, wc, d, gluw, glub):
    s, b, _ = u_tm.shape
    return pl.pallas_call(
        _s5_step_kernel,
        out_shape=[jax.ShapeDtypeStruct((s, b, D_S5), F32), jax.ShapeDtypeStruct((b, 2 * S5_W), F32)],
        compiler_params=pltpu.CompilerParams(vmem_limit_bytes=VMEM_LIMIT),
        name="s5_sample",
    )(u_tm, h0, powre, powim, wb, wc, d, gluw, glub)


def s5_matrices(bb_re, bb_im, c_re, c_im):
    g, p, h = S5_GROUPS, S5_STATE, S5_GROUP
    eye = jnp.eye(g, dtype=F32)
    bd_in = lambda bb: jnp.einsum('gph,gk->ghkp', bb.reshape(g, p, h), eye).reshape(g * h, g * p)
    bd_out = lambda c: jnp.einsum('ghp,gk->gpkh', c, eye).reshape(g * p, g * h)
    wb = jnp.concatenate([bd_in(bb_re), bd_in(bb_im)], axis=1).astype(BF16)
    wc = jnp.concatenate([bd_out(c_re), -bd_out(c_im)], axis=0).astype(BF16)
    return wb, wc


RW_PACK = 8 * D_RW


def _head_ones(n):
    r = lax.broadcasted_iota(I32, (n, n), 0) // HEAD_DIM
    c = lax.broadcasted_iota(I32, (n, n), 1) // HEAD_DIM
    return jnp.where(r == c, 1.0, 0.0).astype(BF16)


def _seg_sum(x, ones_bd):
    hi, lo = _split(x)
    return _dot(hi, ones_bd) + _dot(lo, ones_bd)


def _softplus(x):
    return jnp.maximum(x, 0.0) + jnp.log(1.0 + jnp.exp(-jnp.abs(x)))


def _rw_pre_kernel(cols_ref, shift0_ref, mu_ref, w0_ref, a0_ref, wlr_ref, kk_ref, ka_ref, rk_ref,
                   o_ref, carry, *, chunk, valid):
    @pl.when(pl.program_id(1) == 0)
    def _():
        carry[...] = shift0_ref[...]

    cf = cols_ref[...]
    tm = cf.shape[0]
    row = lax.broadcasted_iota(I32, (tm, 1), 0)
    prev = jnp.where(row == 0, carry[...], pltpu.roll(cf, 1, 0))
    carry[...] = cf[tm - 1:tm, :]
    xs = cf + (prev - cf) * mu_ref[...]
    r, k, v = xs[:, :D_RW], xs[:, D_RW:2 * D_RW], xs[:, 2 * D_RW:3 * D_RW]
    lr = xs[:, 3 * D_RW:]
    lane = lax.broadcasted_iota(I32, lr.shape, 1)
    t = jnp.where(lane < RW_W_RANK, jnp.tanh(lr), jnp.where(lane < RW_W_RANK + RW_A_RANK, lr, _sigmoid(lr)))
    proj = _dot3(t, wlr_ref[...])
    w = -_softplus(-(w0_ref[...] + proj[:, :D_RW])) - 0.5
    logw = -jnp.exp(w)
    a = _sigmoid(a0_ref[...] + proj[:, D_RW:2 * D_RW])
    g = proj[:, 2 * D_RW:]
    ones_bd = _head_ones(D_RW)
    kk = k * kk_ref[...]
    kk = kk / jnp.maximum(jnp.sqrt(_seg_sum(kk * kk, ones_bd)), 1e-12)
    km = k * (1.0 + (a - 1.0) * ka_ref[...])
    bonus = _seg_sum(r * km * rk_ref[...], ones_bd) * v
    pos = row % chunk
    if valid < chunk:
        live = pos < valid
        zero = jnp.zeros_like(r)
        logw, kk, km, v, r = (jnp.where(live, logw, zero), jnp.where(live, kk, zero), jnp.where(live, km, zero),
                              jnp.where(live, v, zero), jnp.where(live, r, zero))
    gc = logw
    d = 1
    while d < chunk:
        gc = gc + jnp.where(pos >= d, pltpu.roll(gc, d, 0), 0.0)
        d *= 2
    eg = jnp.exp(gc)
    eng = jnp.exp(-gc)
    o_ref[:, 0 * D_RW:1 * D_RW] = r * eg
    o_ref[:, 1 * D_RW:2 * D_RW] = kk * jnp.exp(gc - logw)
    o_ref[:, 2 * D_RW:3 * D_RW] = kk * a * eng
    o_ref[:, 3 * D_RW:4 * D_RW] = km * eng
    o_ref[:, 4 * D_RW:5 * D_RW] = v
    o_ref[:, 5 * D_RW:6 * D_RW] = eg
    o_ref[:, 6 * D_RW:7 * D_RW] = bonus
    o_ref[:, 7 * D_RW:8 * D_RW] = g


def rwkv_prepare(cols, shift0, mu, w0, a0, wlr, k_k, k_a, r_k, tm, chunk, valid):
    b, s, c = cols.shape
    const = lambda a: pl.BlockSpec(a.shape, lambda i, j: (0,) * a.ndim)
    return pl.pallas_call(
        functools.partial(_rw_pre_kernel, chunk=chunk, valid=valid),
        grid=(b, s // tm),
        in_specs=[pl.BlockSpec((None, tm, c), lambda i, j: (i, j, 0)),
                  pl.BlockSpec((None, 1, c), lambda i, j: (i, 0, 0)),
                  const(mu), const(w0), const(a0), const(wlr), const(k_k), const(k_a), const(r_k)],
        out_specs=pl.BlockSpec((None, tm, RW_PACK), lambda i, j: (i, j, 0)),
        out_shape=jax.ShapeDtypeStruct((b, s, RW_PACK), F32),
        scratch_shapes=[pltpu.VMEM((1, c), F32)],
        compiler_params=_cparams(("arbitrary", "arbitrary")),
        name="rwkv_prepare",
    )(cols, shift0, mu, w0, a0, wlr, k_k, k_a, r_k)


def _rw_chunk_kernel(x_ref, s0_ref, lnw_ref, lnb_ref, y_ref, st_ref, s_scr, y_scr, *, chunk):
    @pl.when(pl.program_id(1) == 0)
    def _():
        s_scr[...] = s0_ref[...]

    rows = x_ref.shape[0]
    hd = HEAD_DIM
    ri = lax.broadcasted_iota(I32, (chunk, chunk), 0)
    ci = lax.broadcasted_iota(I32, (chunk, chunk), 1)
    strict = ri > ci
    incl = ri >= ci
    eye = jnp.where(ri == ci, 1.0, 0.0)
    ek = lax.broadcasted_iota(I32, (hd, hd), 0) == lax.broadcasted_iota(I32, (hd, hd), 1)
    eye_k = jnp.where(ek, 1.0, 0.0)
    n_sq = chunk.bit_length() - 2
    for c in range(rows // chunk):
        r0 = c * chunk
        for h in range(N_RW_HEADS):
            col = lambda j: x_ref[r0:r0 + chunk, j * D_RW + h * hd:j * D_RW + (h + 1) * hd]
            rt, kt, bt, km, v = col(0), col(1), col(2), col(3), col(4)
            eg_last = x_ref[r0 + chunk - 1:r0 + chunk, 5 * D_RW + h * hd:5 * D_RW + (h + 1) * hd]
            lhs = jnp.concatenate([kt, rt], axis=0)
            gb = _dot3_nt(lhs, bt)
            gk = _dot3_nt(lhs, km)
            a_bb = jnp.where(strict, gb[:chunk], 0.0)
            a_rb = jnp.where(incl, gb[chunk:], 0.0)
            a_kk = jnp.where(strict, gk[:chunk], 0.0)
            a_rk = jnp.where(incl, gk[chunk:], 0.0)
            minv = eye - a_bb
            p = a_bb
            for _ in range(n_sq):
                p = _dot3(p, p)
                minv = minv + _dot3(minv, p)
            av = _dot3(jnp.concatenate([a_kk, a_rk], axis=0), v)
            khat = _dot3(minv, kt)
            p1 = _dot3(minv, av[:chunk])
            rhat = rt - _dot3(a_rb, khat)
            y1 = av[chunk:] - _dot3(a_rb, p1)
            tb = _dot3(jnp.concatenate([khat.T, p1.T], axis=0), bt)
            gmat = (eye_k - tb[:hd]) * eg_last
            umat = (_dot3(v.T, km) - tb[hd:]) * eg_last
            s_prev = s_scr[h]
            y_scr[r0:r0 + chunk, h * hd:(h + 1) * hd] = _dot3_nt(rhat, s_prev) + y1
            s_scr[h] = _dot3(s_prev, gmat) + umat
    st_ref[...] = s_scr[...]
    y = y_scr[...]
    ones_bd = _head_ones(D_RW)
    mean = _seg_sum(y, ones_bd) * (1.0 / hd)
    yc = y - mean
    var = _seg_sum(yc * yc, ones_bd) * (1.0 / hd)
    yn = yc * lax.rsqrt(var + RW_GN_EPS) * lnw_ref[...] + lnb_ref[...]
    y_ref[...] = (yn + x_ref[:, 6 * D_RW:7 * D_RW]) * x_ref[:, 7 * D_RW:8 * D_RW]


def rwkv_chunked(packed, state0, ln_w, ln_b, rows, chunk):
    b, s, _ = packed.shape
    const = lambda a: pl.BlockSpec(a.shape, lambda i, j: (0,) * a.ndim)
    st_spec = pl.BlockSpec((None, N_RW_HEADS, HEAD_DIM, HEAD_DIM), lambda i, j: (i, 0, 0, 0))
    return pl.pallas_call(
        functools.partial(_rw_chunk_kernel, chunk=chunk),
        grid=(b, s // rows),
        in_specs=[pl.BlockSpec((None, rows, RW_PACK), lambda i, j: (i, j, 0)), st_spec, const(ln_w), const(ln_b)],
        out_specs=[pl.BlockSpec((None, rows, D_RW), lambda i, j: (i, j, 0)), st_spec],
        out_shape=[jax.ShapeDtypeStruct((b, s, D_RW), F32),
                   jax.ShapeDtypeStruct((b, N_RW_HEADS, HEAD_DIM, HEAD_DIM), F32)],
        scratch_shapes=[pltpu.VMEM((N_RW_HEADS, HEAD_DIM, HEAD_DIM), F32), pltpu.VMEM((rows, D_RW), F32)],
        compiler_params=_cparams(("arbitrary", "arbitrary")),
        name="rwkv_chunked",
    )(packed, state0, ln_w, ln_b)


def rwkv_lowrank_matrix(w2, a2, g2):
    z = lambda r: jnp.zeros((r, D_RW), F32)
    return jnp.concatenate([
        jnp.concatenate([w2, z(RW_W_RANK), z(RW_W_RANK)], axis=1),
        jnp.concatenate([z(RW_A_RANK), a2, z(RW_A_RANK)], axis=1),
        jnp.concatenate([z(RW_G_RANK), z(RW_G_RANK), g2], axis=1)], axis=0)


def _float_key(x):
    b = lax.bitcast_convert_type(x, I32)
    return jnp.where(b < 0, b ^ jnp.int32(0x7FFFFFFF), b)


def _py_key(v):
    import numpy as np
    b = int(np.float32(v).view(np.int32))
    return b ^ 0x7FFFFFFF if b < 0 else b


KEY_NEG_INF = _py_key(NEG_INF)


def _count_ge(sc_ref, nc, cand):
    _, r, wc = sc_ref.shape

    def body(c, acc):
        return acc + jnp.where(sc_ref[c] >= cand, 1.0, 0.0)

    acc = lax.fori_loop(0, nc, body, jnp.zeros((r, wc), F32))
    return jnp.sum(acc, axis=1, keepdims=True)


def _select_threshold(sc_ref, nc, k):
    _, r, wc = sc_ref.shape

    def bit_body(i, carry):
        t, cge = carry
        cand = t + jnp.left_shift(jnp.int32(1), 31 - i)
        cnt = _count_ge(sc_ref, nc, cand)
        ok = cnt >= k
        return jnp.where(ok, cand, t), jnp.where(ok, cnt, cge)

    t0 = jnp.full((r, 1), INT_MIN, I32)
    cge0 = jnp.zeros((r, 1), F32) + jnp.asarray(nc * wc, F32)
    return lax.fori_loop(0, 32, bit_body, (t0, cge0))


def _resolve_ties(sc_ref, nc, t, need):
    _, r, wc = sc_ref.shape
    upper = jnp.where(lax.broadcasted_iota(I32, (wc, wc), 0) < lax.broadcasted_iota(I32, (wc, wc), 1), 1.0, 0.0)

    def body(c, offs):
        x = sc_ref[c]
        e = x == t
        ef = jnp.where(e, 1.0, 0.0)
        rank = _dot(ef, upper) + offs
        sc_ref[c] = jnp.where(e, jnp.where(rank >= need, jnp.int32(INT_MIN), x), x)
        return offs + jnp.sum(ef, axis=1, keepdims=True)

    lax.fori_loop(0, nc, body, jnp.zeros((r, 1), F32))


def _topk_mask_prepare(sc_ref, nc, k, valid_rows):
    t, cge = _select_threshold(sc_ref, nc, k)
    r = t.shape[0]
    live = lax.broadcasted_iota(I32, (r, 1), 0) < valid_rows
    tied = live & (cge > k) & (t != KEY_NEG_INF)

    @pl.when(jnp.max(jnp.where(tied, 1.0, 0.0)) > 0.0)
    def _():
        cgt = _count_ge(sc_ref, nc, t + 1)
        _resolve_ties(sc_ref, nc, t, k - cgt)

    return t


def _indexer_total(s4, wcols, rows):
    tot = jnp.maximum(s4[:rows], 0.0) * wcols[0]
    for h in range(1, N_IDX_HEADS):
        tot = tot + jnp.maximum(s4[h * rows:(h + 1) * rows], 0.0) * wcols[h]
    return jnp.where(tot == 0.0, 0.0, tot)


def _dsa_prompt_kernel(q_ref, kb_ref, vb_ref, qi3_ref, ki3_ref, kiw_ref, o_ref, sc, m_scr, l_scr, acc_scr, *, k_sel):
    j = pl.program_id(1)
    qb, wc = Q_BLOCK, KEY_CHUNK
    nck = (j * qb + qb + wc - 1) // wc
    rowpos = j * qb + lax.broadcasted_iota(I32, (qb, 1), 0)
    colbase = lax.broadcasted_iota(I32, (1, wc), 1)

    lhs = jnp.concatenate([qi3_ref[:, 2 * LANES * h:2 * LANES * (h + 1)] for h in range(N_IDX_HEADS)], axis=0)
    wcols = [kiw_ref[:, IDX_DIM + h:IDX_DIM + h + 1] for h in range(N_IDX_HEADS)]

    def score_body(c, carry):
        kc = ki3_ref[pl.ds(pl.multiple_of(c * wc, wc), wc), :]
        tot = _indexer_total(_dot_nt(lhs, kc), wcols, qb)
        tot = jnp.where(c * wc + colbase <= rowpos, tot, NEG_INF)
        sc[c] = _float_key(tot)
        return carry

    lax.fori_loop(0, nck, score_body, 0)
    t = _topk_mask_prepare(sc, nck, k_sel, qb)

    m_scr[...] = jnp.full(m_scr.shape, NEG_INF, F32)
    l_scr[...] = jnp.zeros(l_scr.shape, F32)
    acc_scr[...] = jnp.zeros(acc_scr.shape, F32)
    lane = lax.broadcasted_iota(I32, (qb, LANES), 1)
    qpad = []
    for pr in range(N_ATT_HEADS // 2):
        blk = q_ref[:, pr * LANES:(pr + 1) * LANES]
        zero = jnp.zeros_like(blk)
        qpad += [jnp.where(lane < HEAD_DIM, blk, zero), jnp.where(lane < HEAD_DIM, zero, blk)]

    def att_body(c, carry):
        off = pl.multiple_of(c * wc, wc)
        mask = (sc[c] >= t) & (c * wc + colbase <= rowpos)
        for pr in range(N_ATT_HEADS // 2):
            kp = kb_ref[pl.ds(off, wc), pr * LANES:(pr + 1) * LANES]
            vp = vb_ref[pl.ds(off, wc), pr * LANES:(pr + 1) * LANES]
            for half in range(2):
                h = 2 * pr + half
                s = jnp.where(mask, _dot_nt(qpad[h], kp), NEG_INF)
                m_prev = m_scr[h]
                m_new = jnp.maximum(m_prev, jnp.max(s, axis=1, keepdims=True))
                alpha = jnp.exp(m_prev - m_new)
                p = jnp.exp(s - m_new)
                l_scr[h] = alpha * l_scr[h] + jnp.sum(p, axis=1, keepdims=True)
                acc_scr[h] = alpha * acc_scr[h] + _dot(p.astype(BF16), vp)
                m_scr[h] = m_new
        return carry

    lax.fori_loop(0, nck, att_body, 0)
    for pr in range(N_ATT_HEADS // 2):
        oa = acc_scr[2 * pr] / l_scr[2 * pr]
        ob = acc_scr[2 * pr + 1] / l_scr[2 * pr + 1]
        o_ref[:, pr * LANES:(pr + 1) * LANES] = jnp.where(lane < HEAD_DIM, oa, ob)


def dsa_prompt(q, kb, vb, qi3, ki3, kiw):
    b, s, _ = q.shape
    k_sel = min(TOPK_MAX, s // 4)
    blk = lambda w: pl.BlockSpec((None, Q_BLOCK, w), lambda i, j: (i, j, 0))
    full = lambda w: pl.BlockSpec((None, s, w), lambda i, j: (i, 0, 0))
    nh = N_ATT_HEADS
    return pl.pallas_call(
        functools.partial(_dsa_prompt_kernel, k_sel=k_sel),
        grid=(b, s // Q_BLOCK),
        in_specs=[blk(D_ATT), full(D_ATT), full(D_ATT), blk(N_IDX_HEADS * 2 * LANES), full(2 * LANES), blk(LANES)],
        out_specs=blk(D_ATT),
        out_shape=jax.ShapeDtypeStruct((b, s, D_ATT), F32),
        scratch_shapes=[pltpu.VMEM((s // KEY_CHUNK, Q_BLOCK, KEY_CHUNK), I32),
                        pltpu.VMEM((nh, Q_BLOCK, 1), F32), pltpu.VMEM((nh, Q_BLOCK, 1), F32),
                        pltpu.VMEM((nh, Q_BLOCK, LANES), F32)],
        compiler_params=_cparams(("arbitrary", "arbitrary")),
        name="dsa_prompt",
    )(q, kb, vb, qi3, ki3, kiw)


SAMPLE_ROWS = 8
PAGE_GROUP = 8


def _dsa_sample_index_kernel(pt_ref, *refs, n_valid, k_sel):
    pages = refs[:PAGE_GROUP]
    qi3_ref, kiw_ref, ki3n_ref, sc_ref, thr_ref = refs[PAGE_GROUP:]
    g = pl.program_id(1)
    r = SAMPLE_ROWS
    n_chunks = sc_ref.shape[0]
    q3 = qi3_ref[...].astype(F32)
    qhi = jnp.concatenate([q3[:, 2 * LANES * h:2 * LANES * h + IDX_DIM] for h in range(N_IDX_HEADS)],
                          axis=0).astype(BF16)
    qlo = jnp.concatenate([q3[:, 2 * LANES * h + LANES:2 * LANES * h + LANES + IDX_DIM]
                           for h in range(N_IDX_HEADS)], axis=0).astype(BF16)
    wcols = [kiw_ref[:, IDX_DIM + h:IDX_DIM + h + 1] for h in range(N_IDX_HEADS)]

    def scores(khi, klo):
        return _dot_nt(qhi, khi) + (_dot_nt(qhi, klo) + _dot_nt(qlo, khi))

    for i in range(PAGE_GROUP):
        khi, klo = _split(pages[i][...])
        sc_ref[g * PAGE_GROUP + i] = _float_key(_indexer_total(scores(khi, klo), wcols, r))

    @pl.when(g == pl.num_programs(1) - 1)
    def _():
        khi = ki3n_ref[:, :IDX_DIM]
        klo = ki3n_ref[:, IDX_DIM:2 * IDX_DIM]
        tot = _indexer_total(scores(khi, klo), wcols, r)
        row = lax.broadcasted_iota(I32, tot.shape, 0)
        col = lax.broadcasted_iota(I32, tot.shape, 1)
        tot = jnp.where((col <= row) & (col < n_valid), tot, NEG_INF)
        sc_ref[n_chunks - 1] = _float_key(tot)
        t = _topk_mask_prepare(sc_ref, n_chunks, k_sel, n_valid)
        thr_ref[...] = jnp.broadcast_to(t, thr_ref.shape)


def dsa_sample_index(page_table, cache_kidx, layer, qi3, kiw, ki3_new, n_valid):
    b, n_pages = page_table.shape
    n_chunks = n_pages + 1
    k_sel = min(TOPK_MAX, (n_pages * PAGE_SIZE + n_valid) // 4)
    page_spec = lambda i: pl.BlockSpec((None, None, PAGE_SIZE, IDX_DIM),
                                       lambda bi, g, pt: (layer, pt[bi, g * PAGE_GROUP + i], 0, 0))
    per_b = lambda shape: pl.BlockSpec((None,) + shape, lambda bi, g, pt: (bi,) + (0,) * len(shape))
    r = SAMPLE_ROWS
    return pl.pallas_call(
        functools.partial(_dsa_sample_index_kernel, n_valid=n_valid, k_sel=k_sel),
        grid_spec=pltpu.PrefetchScalarGridSpec(
            num_scalar_prefetch=1,
            grid=(b, n_pages // PAGE_GROUP),
            in_specs=[page_spec(i) for i in range(PAGE_GROUP)]
            + [per_b((r, N_IDX_HEADS * 2 * LANES)), per_b((r, LANES)), per_b((PAGE_SIZE, 2 * LANES))],
            out_specs=[per_b((n_chunks, r, PAGE_SIZE)), per_b((r, PAGE_SIZE))],
        ),
        out_shape=[jax.ShapeDtypeStruct((b, n_chunks, r, PAGE_SIZE), I32),
                   jax.ShapeDtypeStruct((b, r, PAGE_SIZE), I32)],
        compiler_params=_cparams(("arbitrary", "arbitrary")),
        name="dsa_sample_index",
    )(page_table, *([cache_kidx] * PAGE_GROUP), qi3, kiw, ki3_new)


def _dsa_sample_attend_kernel(pt_ref, *refs, n_valid):
    kpages = refs[:PAGE_GROUP]
    vpages = refs[PAGE_GROUP:2 * PAGE_GROUP]
    qbd_ref, sel_ref, seln_ref, thr_ref, kn_ref, vn_ref, o_ref, m_scr, l_scr, acc_scr = refs[2 * PAGE_GROUP:]
    g = pl.program_id(1)
    r = SAMPLE_ROWS

    @pl.when(g == 0)
    def _():
        m_scr[...] = jnp.full(m_scr.shape, NEG_INF, F32)
        l_scr[...] = jnp.zeros(l_scr.shape, F32)
        acc_scr[...] = jnp.zeros(acc_scr.shape, F32)

    qbd = qbd_ref[...]
    thr = thr_ref[...]
    per_head = lambda x: jnp.concatenate([x] * N_ATT_HEADS, axis=0)
    thr_all = per_head(thr)

    def update(sel, extra, k, v):
        mask = per_head(sel) >= thr_all
        if extra is not None:
            mask = mask & (per_head(extra) > 0)
        s = jnp.where(mask, _dot_nt(qbd, k), NEG_INF)
        m_prev = m_scr[...]
        m_new = jnp.maximum(m_prev, jnp.max(s, axis=1, keepdims=True))
        alpha = jnp.exp(m_prev - m_new)
        p = jnp.exp(s - m_new)
        l_scr[...] = alpha * l_scr[...] + jnp.sum(p, axis=1, keepdims=True)
        acc_scr[...] = alpha * acc_scr[...] + _dot(p.astype(BF16), v)
        m_scr[...] = m_new

    for i in range(PAGE_GROUP):
        update(sel_ref[i], None, kpages[i][...].astype(BF16), vpages[i][...].astype(BF16))

    @pl.when(g == pl.num_programs(1) - 1)
    def _():
        row = lax.broadcasted_iota(I32, thr.shape, 0)
        col = lax.broadcasted_iota(I32, thr.shape, 1)
        causal = jnp.where((col <= row) & (col < n_valid), 1, 0)
        update(seln_ref[...], causal, kn_ref[...], vn_ref[...])
        out = acc_scr[...] / l_scr[...]
        lane = lax.broadcasted_iota(I32, (r, D_ATT), 1) // HEAD_DIM
        tot = jnp.where(lane == 0, out[:r], 0.0)
        for h in range(1, N_ATT_HEADS):
            tot = tot + jnp.where(lane == h, out[h * r:(h + 1) * r], 0.0)
        o_ref[...] = tot


def dsa_sample_attend(page_table, cache_k, cache_v, layer, q_bd, sel, thr, k_new, v_new, n_valid):
    b, n_pages = page_table.shape
    r = SAMPLE_ROWS
    page_spec = lambda i: pl.BlockSpec((None, None, PAGE_SIZE, D_ATT),
                                       lambda bi, g, pt: (layer, pt[bi, g * PAGE_GROUP + i], 0, 0))
    per_b = lambda shape: pl.BlockSpec((None,) + shape, lambda bi, g, pt: (bi,) + (0,) * len(shape))
    rows = N_ATT_HEADS * r
    return pl.pallas_call(
        functools.partial(_dsa_sample_attend_kernel, n_valid=n_valid),
        grid_spec=pltpu.PrefetchScalarGridSpec(
            num_scalar_prefetch=1,
            grid=(b, n_pages // PAGE_GROUP),
            in_specs=[page_spec(i) for i in range(PAGE_GROUP)] + [page_spec(i) for i in range(PAGE_GROUP)]
            + [per_b((rows, D_ATT)),
               pl.BlockSpec((None, PAGE_GROUP, r, PAGE_SIZE), lambda bi, g, pt: (bi, g, 0, 0)),
               pl.BlockSpec((None, None, r, PAGE_SIZE), lambda bi, g, pt: (bi, n_pages, 0, 0)),
               per_b((r, PAGE_SIZE)), per_b((PAGE_SIZE, D_ATT)), per_b((PAGE_SIZE, D_ATT))],
            out_specs=per_b((r, D_ATT)),
            scratch_shapes=[pltpu.VMEM((rows, 1), F32), pltpu.VMEM((rows, 1), F32), pltpu.VMEM((rows, D_ATT), F32)],
        ),
        out_shape=jax.ShapeDtypeStruct((b, r, D_ATT), F32),
        compiler_params=_cparams(("arbitrary", "arbitrary")),
        name="dsa_sample_attend",
    )(page_table, *([cache_k] * PAGE_GROUP), *([cache_v] * PAGE_GROUP), q_bd, sel, sel, thr, k_new, v_new)


def pack_w_in(w_in):
    d = w_in.shape[0]
    pad = jnp.zeros((d, ZP_END - ZP_KIW - (OFF_RW - OFF_KI)), w_in.dtype)
    return jnp.concatenate([w_in[:, OFF_S5:OFF_KI], w_in[:, OFF_RW:], w_in[:, OFF_KI:OFF_RW], pad], axis=1).astype(BF16)


def rope_tables(pos):
    half = HEAD_DIM // 2
    inv = ROPE_THETA ** (-jnp.arange(half, dtype=F32) / half)
    ang = pos.astype(F32)[:, None] * inv[None, :]
    cos, sin = jnp.cos(ang), jnp.sin(ang)
    return jnp.tile(jnp.concatenate([cos, cos], 1), (1, 2)), jnp.tile(jnp.concatenate([-sin, sin], 1), (1, 2))


def block_diag_queries(q):
    b, r, d = q.shape
    head_of_lane = jnp.arange(d) // HEAD_DIM
    keep = head_of_lane[None, :] == jnp.arange(N_ATT_HEADS)[:, None]
    return jnp.where(keep[None, :, None, :], q[:, None], jnp.zeros((), q.dtype)).reshape(b, N_ATT_HEADS * r, d)


def _pad_rows(a, rows):
    return jnp.pad(a, ((0, 0), (0, rows - a.shape[1])) + ((0, 0),) * (a.ndim - 2))


PROMPT_TILE = 256
RW_PRE_TILE = 512
RW_STEP_ROWS = 128


def kernel(x_prompt, x_sample, c_prompt, c_sample, cache_k, cache_v, cache_kidx, state_s5_re, state_s5_im, state_rwkv, state_rwkv_shift, page_table, ada_w, ada_b, norm_pre, norm_post, ffn_wi, ffn_wo, w_in, w_out, s5_a_re, s5_a_im, s5_log_dt, s5_b_re, s5_b_im, s5_c_re, s5_c_im, s5_d, s5_glu_w, s5_glu_b, rw_mu, rw_w0, rw_w2, rw_a0, rw_a2, rw_g2, rw_k_k, rw_k_a, rw_r_k, rw_ln_w, rw_ln_b):
    bp, sp, d = x_prompt.shape
    bs, ss, _ = x_sample.shape
    depth = ada_w.shape[0]
    n_pool = cache_k.shape[1]
    past = page_table.shape[1] * PAGE_SIZE
    np_tok, ns_tok = bp * sp, bs * ss
    tpb = sp // PROMPT_TILE
    row1 = lambda a: a.reshape(1, -1)

    c_all = _pad_rows(jnp.concatenate([c_prompt, c_sample], axis=0)[None], -(-(bp + bs) // 8) * 8)[0]
    mod = ada_mod(c_all, ada_w, ada_b).reshape(depth, c_all.shape[0], N_SUB, 3, d)

    cos_p, sin_p = rope_tables(jnp.arange(sp))
    cos_s, sin_s = (jnp.tile(t, (bs, 1)) for t in rope_tables(past + jnp.arange(ss)))
    ck = cache_k.reshape(depth, n_pool, PAGE_SIZE, D_ATT)
    cv = cache_v.reshape(depth, n_pool, PAGE_SIZE, D_ATT)

    xp = x_prompt.reshape(np_tok, d)
    xs = x_sample.reshape(ns_tok, d)
    outs_p, outs_s = [], []
    for l in range(depth):
        mod_p = mod[l, :bp]
        mod_s = jnp.repeat(mod[l, bp:bp + bs], ss, axis=0)
        pm = lambda i, j: mod_p[:, i, j][:, None, :]
        sm = lambda i, j: mod_s[:, i, j][None]
        npre = lambda i: row1(norm_pre[l, i])
        npost = lambda i: row1(norm_post[l, i])
        wi0, wo0 = ffn_wi[l, 0].astype(BF16), ffn_wo[l, 0].astype(BF16)
        wi1, wo1 = ffn_wi[l, 1].astype(BF16), ffn_wo[l, 1].astype(BF16)
        w_in_p = pack_w_in(w_in[l])
        w_out_b = w_out[l].astype(BF16)
        powre, powim, bb_re, bb_im = s5_discretise(s5_a_re[l], s5_a_im[l], s5_log_dt[l], s5_b_re[l], s5_b_im[l],
                                                   rows=S5_CHUNK)
        wb, wc = s5_matrices(bb_re, bb_im, s5_c_re[l], s5_c_im[l])
        s5_tail = (powre, powim, wb, wc, row1(s5_d[l]), s5_glu_w[l].astype(BF16), row1(s5_glu_b[l]))
        wlr = rwkv_lowrank_matrix(rw_w2[l], rw_a2[l], rw_g2[l])
        rw_pre = (row1(rw_mu[l]), row1(rw_w0[l]), row1(rw_a0[l]), wlr, row1(rw_k_k[l]), row1(rw_k_a[l]),
                  row1(rw_r_k[l]))
        ln = (row1(rw_ln_w[l]), row1(rw_ln_b[l]))

        xp = ffn_block(xp, pm(0, 0), pm(0, 1), pm(0, 2), npre(0), npost(0), wi0, wo0, PROMPT_TILE, tpb)
        u, q, k, kb, v, vb, qi3, kiw, ki3, rw = inproj_block(xp, pm(1, 0), pm(1, 1), npre(1), w_in_p, cos_p, sin_p,
                                                             PROMPT_TILE, tpb, tpb)
        seq = lambda a: a.reshape(bp, sp, a.shape[-1])
        y_s5, h_s5 = s5_prompt(seq(u), jnp.zeros((bp, 1, 2 * S5_W), F32), *s5_tail)
        y_att = dsa_prompt(seq(q), seq(kb), seq(vb), seq(qi3), seq(ki3), seq(kiw))
        cols = seq(rw)
        packed = rwkv_prepare(cols, jnp.zeros((bp, 1, N_RW_COLS), F32), *rw_pre, RW_PRE_TILE, RW_CHUNK, RW_CHUNK)
        y_rw, st_rw = rwkv_chunked(packed, jnp.zeros((bp, N_RW_HEADS, HEAD_DIM, HEAD_DIM), F32), *ln,
                                   RW_STEP_ROWS, RW_CHUNK)
        flat = lambda a: a.reshape(np_tok, a.shape[-1])
        xp = outproj_block(xp, flat(y_s5), flat(y_att), flat(y_rw), pm(1, 2), npost(1), w_out_b, PROMPT_TILE, tpb)
        xp = ffn_block(xp, pm(2, 0), pm(2, 1), pm(2, 2), npre(2), npost(2), wi1, wo1, PROMPT_TILE, tpb)
        outs_p.append((k.reshape(bp, sp, N_ATT_HEADS, HEAD_DIM), v.reshape(bp, sp, N_ATT_HEADS, HEAD_DIM),
                       seq(kiw)[:, :, :IDX_DIM],
                       h_s5[:, 0, :S5_W].reshape(bp, S5_GROUPS, S5_STATE),
                       h_s5[:, 0, S5_W:].reshape(bp, S5_GROUPS, S5_STATE), st_rw, cols[:, sp - 1]))

        xs = ffn_block(xs, sm(0, 0), sm(0, 1), sm(0, 2), npre(0), npost(0), wi0, wo0, ns_tok, 1)
        u, q, k, kb, v, vb, qi3, kiw, ki3, rw = inproj_block(xs, sm(1, 0), sm(1, 1), npre(1), w_in_p, cos_s, sin_s,
                                                             ns_tok, 1, 1)
        seq = lambda a: a.reshape(bs, ss, a.shape[-1])
        h0 = jnp.concatenate([state_s5_re[l].reshape(bs, S5_W), state_s5_im[l].reshape(bs, S5_W)], axis=1)
        y_s5, h_s5 = s5_sample(jnp.swapaxes(seq(u), 0, 1), h0, *s5_tail)
        y_s5 = jnp.swapaxes(y_s5, 0, 1)
        sel, thr = dsa_sample_index(page_table, cache_kidx, l, _pad_rows(seq(qi3), SAMPLE_ROWS),
                                    _pad_rows(seq(kiw), SAMPLE_ROWS), _pad_rows(seq(ki3), PAGE_SIZE), ss)
        q_bd = block_diag_queries(_pad_rows(seq(q), SAMPLE_ROWS))
        y_att = dsa_sample_attend(page_table, ck, cv, l, q_bd, sel, thr, _pad_rows(seq(kb), PAGE_SIZE),
                                  _pad_rows(seq(vb), PAGE_SIZE), ss)[:, :ss]
        cols = seq(rw)
        packed = rwkv_prepare(_pad_rows(cols, RW_CHUNK), state_rwkv_shift[l].reshape(bs, 1, N_RW_COLS), *rw_pre,
                              RW_CHUNK, RW_CHUNK, ss)
        y_rw, st_rw = rwkv_chunked(packed, state_rwkv[l], *ln, RW_CHUNK, RW_CHUNK)
        flat = lambda a: a.reshape(ns_tok, a.shape[-1])
        xs = outproj_block(xs, flat(y_s5), flat(y_att), flat(y_rw[:, :ss]), sm(1, 2), npost(1), w_out_b, ns_tok, 1)
        xs = ffn_block(xs, sm(2, 0), sm(2, 1), sm(2, 2), npre(2), npost(2), wi1, wo1, ns_tok, 1)
        outs_s.append((k.reshape(bs, ss, N_ATT_HEADS, HEAD_DIM), v.reshape(bs, ss, N_ATT_HEADS, HEAD_DIM),
                       seq(kiw)[:, :, :IDX_DIM],
                       h_s5[:, :S5_W].reshape(bs, S5_GROUPS, S5_STATE),
                       h_s5[:, S5_W:].reshape(bs, S5_GROUPS, S5_STATE), st_rw, cols[:, ss - 1]))

    stack = lambda outs, i: jnp.stack([o[i] for o in outs])
    return ((xp.reshape(bp, sp, d), xs.reshape(bs, ss, d))
            + tuple(stack(outs_p, i) for i in range(7)) + tuple(stack(outs_s, i) for i in range(7)))
```

```python
import functools
import math

import jax
import jax.numpy as jnp
from jax import lax
from jax.experimental import pallas as pl
from jax.experimental.pallas import tpu as pltpu

F32 = jnp.float32
BF16 = jnp.bfloat16
I32 = jnp.int32

D_MODEL = 1024
PAGE_SIZE = 128
D_S5 = 256
D_ATT = 512
D_RW = 256
S5_GROUP = 16
S5_GROUPS = D_S5 // S5_GROUP
S5_STATE = 64
S5_W = S5_GROUPS * S5_STATE
HEAD_DIM = 64
N_ATT_HEADS = D_ATT // HEAD_DIM
N_IDX_HEADS = 4
IDX_DIM = 64
TOPK_MAX = 256
Q_BLOCK = 128
ROPE_THETA = 10000.0
N_RW_HEADS = D_RW // HEAD_DIM
RW_W_RANK = 32
RW_A_RANK = 32
RW_G_RANK = 64
RW_GN_EPS = 64e-5
N_RW_COLS = 3 * D_RW + RW_W_RANK + RW_A_RANK + RW_G_RANK
D_FF = 2816
HALF_STEP = 0.5
RMS_EPS = 1e-6
N_SUB = 3
NEG_INF = -1e30
OFF_S5 = 0
OFF_Q = OFF_S5 + D_S5
OFF_K = OFF_Q + D_ATT
OFF_V = OFF_K + D_ATT
OFF_QI = OFF_V + D_ATT
OFF_KI = OFF_QI + N_IDX_HEADS * IDX_DIM
OFF_WI = OFF_KI + IDX_DIM
OFF_RW = OFF_WI + N_IDX_HEADS
N_IN = OFF_RW + N_RW_COLS

LANES = 128
VMEM_LIMIT = 56 * 1024 * 1024
INT_MIN = -(2 ** 31)

ZP_U, ZP_Q, ZP_K, ZP_V, ZP_QI, ZP_RW, ZP_KIW, ZP_END = 0, 256, 768, 1280, 1792, 2048, 2944, 3072

RW_CHUNK = 64
S5_CHUNK = 128
KEY_CHUNK = 512


def _cparams(sem):
    return pltpu.CompilerParams(dimension_semantics=sem, vmem_limit_bytes=VMEM_LIMIT)


def _dot(a, b):
    return jnp.dot(a, b, preferred_element_type=F32)


def _dot_nt(a, b):
    return lax.dot_general(a, b, (((1,), (1,)), ((), ())), preferred_element_type=F32)


def _split(x):
    hi = x.astype(BF16)
    lo = (x - hi.astype(F32)).astype(BF16)
    return hi, lo


def _dot3(a, b):
    ah, al = _split(a)
    bh, bl = _split(b)
    return _dot(ah, bh) + (_dot(ah, bl) + _dot(al, bh))


def _dot3_nt(a, b):
    ah, al = _split(a)
    bh, bl = _split(b)
    return _dot_nt(ah, bh) + (_dot_nt(ah, bl) + _dot_nt(al, bh))


def _rms(x, g):
    return x * lax.rsqrt(jnp.mean(x * x, axis=-1, keepdims=True) + RMS_EPS) * g


def _sigmoid(x):
    return 1.0 / (1.0 + jnp.exp(-x))


def _ada_kernel(c_ref, w_ref, b_ref, o_ref):
    c = c_ref[...]
    h = (c * _sigmoid(c)).astype(BF16)
    o_ref[...] = _dot(h, w_ref[...].astype(BF16)) + b_ref[...]


def ada_mod(c_all, ada_w, ada_b, tn=1152):
    depth, d, n = ada_w.shape
    rows = c_all.shape[0]
    return pl.pallas_call(
        _ada_kernel,
        grid=(depth, n // tn),
        in_specs=[pl.BlockSpec((rows, d), lambda l, j: (0, 0)),
                  pl.BlockSpec((None, d, tn), lambda l, j: (l, 0, j)),
                  pl.BlockSpec((None, 1, tn), lambda l, j: (l, 0, j))],
        out_specs=pl.BlockSpec((None, rows, tn), lambda l, j: (l, 0, j)),
        out_shape=jax.ShapeDtypeStruct((depth, rows, n), F32),
        compiler_params=_cparams(("arbitrary", "arbitrary")),
        name="ada_mod",
    )(c_all, ada_w, ada_b.reshape(depth, 1, n))


def _mod_spec(mod, tiles_per_group):
    r = mod.shape[1]
    return pl.BlockSpec((None, r, mod.shape[2]), lambda i: (i // tiles_per_group, 0, 0))


def _ffn_kernel(x_ref, shift_ref, scale_ref, gate_ref, gpre_ref, gpost_ref, wi_ref, wo_ref, o_ref, *, res_w):
    x = x_ref[...]
    h = (_rms(x, gpre_ref[...]) * (1.0 + scale_ref[...]) + shift_ref[...]).astype(BF16)
    g = _dot(h, wi_ref[:, :D_FF])
    u = _dot(h, wi_ref[:, D_FF:])
    a = (g * _sigmoid(g) * u).astype(BF16)
    o = _dot(a, wo_ref[...])
    o_ref[...] = x + (res_w * gate_ref[...]) * _rms(o, gpost_ref[...])


def ffn_block(x, shift, scale, gate, g_pre, g_post, wi, wo, tm, tiles_per_group):
    n, d = x.shape
    row = pl.BlockSpec((tm, d), lambda i: (i, 0))
    const = lambda a: pl.BlockSpec(a.shape, lambda i: (0,) * a.ndim)
    return pl.pallas_call(
        functools.partial(_ffn_kernel, res_w=HALF_STEP),
        grid=(n // tm,),
        in_specs=[row, _mod_spec(shift, tiles_per_group), _mod_spec(scale, tiles_per_group),
                  _mod_spec(gate, tiles_per_group), const(g_pre), const(g_post), const(wi), const(wo)],
        out_specs=row,
        out_shape=jax.ShapeDtypeStruct((n, d), F32),
        compiler_params=_cparams(("arbitrary",)),
        name="ffn_block",
    )(x, shift, scale, gate, g_pre, g_post, wi, wo)


def _rot_block(blk, cos, sin, lane):
    partner = jnp.where((lane & 32) == 0, pltpu.roll(blk, LANES - 32, 1), pltpu.roll(blk, 32, 1))
    return blk * cos + partner * sin


def _inproj_kernel(x_ref, shift_ref, scale_ref, gpre_ref, w_ref, cos_ref, sin_ref,
                   u_ref, q_ref, k_ref, kb_ref, v_ref, vb_ref, qi3_ref, kiw_ref, ki3_ref, rw_ref, vt_ref, *, q_scale):
    x = x_ref[...]
    h = (_rms(x, gpre_ref[...]) * (1.0 + scale_ref[...]) + shift_ref[...]).astype(BF16)
    z = _dot(h, w_ref[...])
    cos = cos_ref[...]
    sin = sin_ref[...]
    lane = lax.broadcasted_iota(I32, cos.shape, 1)
    lo_half = lane < 64
    rot = lambda off: _rot_block(z[:, off:off + LANES], cos, sin, lane)

    u_ref[...] = z[:, ZP_U:ZP_Q]
    for j in range(D_ATT // LANES):
        q_ref[:, j * LANES:(j + 1) * LANES] = (rot(ZP_Q + j * LANES) * q_scale).astype(BF16)
        kr = rot(ZP_K + j * LANES)
        k_ref[:, j * LANES:(j + 1) * LANES] = kr
        kb_ref[:, j * LANES:(j + 1) * LANES] = kr.astype(BF16)
    v = z[:, ZP_V:ZP_QI]
    v_ref[...] = v
    vb_ref[...] = v.astype(BF16)
    vt_ref[...] = v.T.astype(BF16)
    for j in range(N_IDX_HEADS * IDX_DIM // LANES):
        qr = rot(ZP_QI + j * LANES)
        hi = qr.astype(BF16).astype(F32)
        lo = qr - hi
        hi_sw = pltpu.roll(hi, 64, 1)
        lo_sw = pltpu.roll(lo, 64, 1)
        zero = jnp.zeros_like(hi)
        for half in range(2):
            base = (2 * j + half) * 2 * LANES
            a, b = (hi, lo) if half == 0 else (hi_sw, lo_sw)
            a_sw = hi_sw if half == 0 else hi
            qi3_ref[:, base:base + LANES] = jnp.where(lo_half, a, a_sw).astype(BF16)
            qi3_ref[:, base + LANES:base + 2 * LANES] = jnp.where(lo_half, b, zero).astype(BF16)
    raw = z[:, ZP_KIW:ZP_END]
    kiw = jnp.where(lo_half, _rot_block(raw, cos, sin, lane), raw)
    kiw_ref[...] = kiw
    hi = kiw.astype(BF16).astype(F32)
    lo = kiw - hi
    ki3_ref[:, :LANES] = jnp.where(lo_half, hi, pltpu.roll(lo, 64, 1)).astype(BF16)
    ki3_ref[:, LANES:] = jnp.where(lo_half, hi, jnp.zeros_like(hi)).astype(BF16)
    rw_ref[...] = z[:, ZP_RW:ZP_KIW]


def inproj_block(x, shift, scale, g_pre, w, cos_t, sin_t, tm, tiles_per_group, pos_tiles, q_scale):
    n, d = x.shape
    row = lambda width: pl.BlockSpec((tm, width), lambda i: (i, 0))
    const = lambda a: pl.BlockSpec(a.shape, lambda i: (0,) * a.ndim)
    tab = pl.BlockSpec((tm, LANES), lambda i: (i % pos_tiles, 0))
    widths = [(D_S5, F32), (D_ATT, BF16), (D_ATT, F32), (D_ATT, BF16), (D_ATT, F32), (D_ATT, BF16),
              (N_IDX_HEADS * 2 * LANES, BF16), (LANES, F32), (2 * LANES, BF16), (N_RW_COLS, F32)]
    return pl.pallas_call(
        functools.partial(_inproj_kernel, q_scale=q_scale),
        grid=(n // tm,),
        in_specs=[row(d), _mod_spec(shift, tiles_per_group), _mod_spec(scale, tiles_per_group),
                  const(g_pre), const(w), tab, tab],
        out_specs=[row(wd) for wd, _ in widths]
        + [pl.BlockSpec((None, D_ATT, tm), lambda i: (i // pos_tiles, 0, i % pos_tiles))],
        out_shape=[jax.ShapeDtypeStruct((n, wd), dt) for wd, dt in widths]
        + [jax.ShapeDtypeStruct((n // (pos_tiles * tm), D_ATT, pos_tiles * tm), BF16)],
        compiler_params=_cparams(("arbitrary",)),
        name="inproj_block",
    )(x, shift, scale, g_pre, w, cos_t, sin_t)


def _outproj_kernel(x_ref, ys5_ref, yatt_ref, yrw_ref, gate_ref, gpost_ref, w_ref, o_ref):
    o = (_dot(ys5_ref[...].astype(BF16), w_ref[:D_S5, :])
         + _dot(yatt_ref[...].astype(BF16), w_ref[D_S5:D_S5 + D_ATT, :])
         + _dot(yrw_ref[...].astype(BF16), w_ref[D_S5 + D_ATT:, :]))
    o_ref[...] = x_ref[...] + gate_ref[...] * _rms(o, gpost_ref[...])


def outproj_block(x, y_s5, y_att, y_rw, gate, g_post, w, tm, tiles_per_group):
    n, d = x.shape
    row = lambda width: pl.BlockSpec((tm, width), lambda i: (i, 0))
    const = lambda a: pl.BlockSpec(a.shape, lambda i: (0,) * a.ndim)
    return pl.pallas_call(
        _outproj_kernel,
        grid=(n // tm,),
        in_specs=[row(d), row(D_S5), row(D_ATT), row(D_RW), _mod_spec(gate, tiles_per_group),
                  const(g_post), const(w)],
        out_specs=row(d),
        out_shape=jax.ShapeDtypeStruct((n, d), F32),
        compiler_params=_cparams(("arbitrary",)),
        name="outproj_block",
    )(x, y_s5, y_att, y_rw, gate, g_post, w)


def _gelu_tanh(x):
    return 0.5 * x * (1.0 + jnp.tanh(math.sqrt(2.0 / math.pi) * (x + 0.044715 * (x * x * x))))


def _s5_disc_kernel(are_ref, aim_ref, ldt_ref, arec_ref, aimc_ref, ldtc_ref, bre_ref, bim_ref,
                    powre_ref, powim_ref, bbre_ref, bbim_ref, *, rows):
    def zoh(ar, ai, ldt):
        dt = jnp.exp(ldt)
        mag = jnp.exp(ar * dt)
        abr, abi = mag * jnp.cos(ai * dt), mag * jnp.sin(ai * dt)
        den = ar * ar + ai * ai
        nr, ni = abr - 1.0, abi
        return abr, abi, (nr * ar + ni * ai) / den, (ni * ar - nr * ai) / den

    abr, abi, _, _ = zoh(are_ref[...], aim_ref[...], ldt_ref[...])
    pr = jnp.broadcast_to(abr, (rows, abr.shape[1]))
    pi = jnp.broadcast_to(abi, (rows, abr.shape[1]))
    row = lax.broadcasted_iota(I32, pr.shape, 0)
    d = 1
    while d < rows:
        sr = pltpu.roll(pr, d, 0)
        si = pltpu.roll(pi, d, 0)
        m = row >= d
        pr, pi = jnp.where(m, pr * sr - pi * si, pr), jnp.where(m, pr * si + pi * sr, pi)
        d *= 2
    powre_ref[...] = pr
    powim_ref[...] = pi
    _, _, cr, ci = zoh(arec_ref[...], aimc_ref[...], ldtc_ref[...])
    br, bi = bre_ref[...], bim_ref[...]
    bbre_ref[...] = cr * br - ci * bi
    bbim_ref[...] = cr * bi + ci * br


def s5_discretise(a_re, a_im, log_dt, b_re, b_im, rows):
    g, p = a_re.shape
    w = g * p
    ldt = jnp.broadcast_to(log_dt[:, None], (g, p))
    args = (a_re.reshape(1, w), a_im.reshape(1, w), ldt.reshape(1, w),
            a_re.reshape(w, 1), a_im.reshape(w, 1), ldt.reshape(w, 1),
            b_re.reshape(w, S5_GROUP), b_im.reshape(w, S5_GROUP))
    return pl.pallas_call(
        functools.partial(_s5_disc_kernel, rows=rows),
        out_shape=[jax.ShapeDtypeStruct((rows, w), F32), jax.ShapeDtypeStruct((rows, w), F32),
                   jax.ShapeDtypeStruct((w, S5_GROUP), F32), jax.ShapeDtypeStruct((w, S5_GROUP), F32)],
        name="s5_discretise",
    )(*args)


def _s5_head(y, u, d_ref, gluw_ref, glub_ref):
    y = _gelu_tanh(y + d_ref[...] * u)
    return y * _sigmoid(_dot(y.astype(BF16), gluw_ref[...]) + glub_ref[...])


def _s5_kernel(u_ref, h0_ref, powre_ref, powim_ref, wb_ref, wc_ref, d_ref, gluw_ref, glub_ref,
               y_ref, ht_ref, cre, cim):
    t = pl.program_id(1)
    rows = u_ref.shape[0]

    @pl.when(t == 0)
    def _():
        cre[...] = h0_ref[:, :S5_W]
        cim[...] = h0_ref[:, S5_W:]

    u = u_ref[...]
    bu = _dot(u.astype(BF16), wb_ref[...])
    hr, hi = bu[:, :S5_W], bu[:, S5_W:]
    row = lax.broadcasted_iota(I32, hr.shape, 0)
    d = 1
    while d < rows:
        ar, ai = powre_ref[d - 1:d, :], powim_ref[d - 1:d, :]
        sr, si = pltpu.roll(hr, d, 0), pltpu.roll(hi, d, 0)
        m = row >= d
        hr, hi = (hr + jnp.where(m, ar * sr - ai * si, 0.0), hi + jnp.where(m, ar * si + ai * sr, 0.0))
        d *= 2
    pr, pi = powre_ref[...], powim_ref[...]
    c_r, c_i = cre[...], cim[...]
    hr, hi = hr + (pr * c_r - pi * c_i), hi + (pr * c_i + pi * c_r)
    cre[...] = hr[rows - 1:rows, :]
    cim[...] = hi[rows - 1:rows, :]
    ht_ref[:, :S5_W] = hr[rows - 1:rows, :]
    ht_ref[:, S5_W:] = hi[rows - 1:rows, :]
    y = _dot(hr.astype(BF16), wc_ref[:S5_W, :]) + _dot(hi.astype(BF16), wc_ref[S5_W:, :])
    y_ref[...] = _s5_head(y, u, d_ref, gluw_ref, glub_ref)


def s5_prompt(u, h0, powre, powim, wb, wc, d, gluw, glub):
    b, s, _ = u.shape
    rows = powre.shape[0]
    const = lambda a: pl.BlockSpec(a.shape, lambda i, j: (0,) * a.ndim)
    return pl.pallas_call(
        _s5_kernel,
        grid=(b, s // rows),
        in_specs=[pl.BlockSpec((None, rows, D_S5), lambda i, j: (i, j, 0)),
                  pl.BlockSpec((None, 1, 2 * S5_W), lambda i, j: (i, 0, 0)),
                  const(powre), const(powim), const(wb), const(wc), const(d), const(gluw), const(glub)],
        out_specs=[pl.BlockSpec((None, rows, D_S5), lambda i, j: (i, j, 0)),
                   pl.BlockSpec((None, 1, 2 * S5_W), lambda i, j: (i, 0, 0))],
        out_shape=[jax.ShapeDtypeStruct((b, s, D_S5), F32), jax.ShapeDtypeStruct((b, 1, 2 * S5_W), F32)],
        scratch_shapes=[pltpu.VMEM((1, S5_W), F32), pltpu.VMEM((1, S5_W), F32)],
        compiler_params=_cparams(("arbitrary", "arbitrary")),
        name="s5_prompt",
    )(u, h0, powre, powim, wb, wc, d, gluw, glub)


def _s5_step_kernel(u_ref, h0_ref, powre_ref, powim_ref, wb_ref, wc_ref, d_ref, gluw_ref, glub_ref,
                    y_ref, ht_ref):
    ar, ai = powre_ref[0:1, :], powim_ref[0:1, :]
    hr, hi = h0_ref[:, :S5_W], h0_ref[:, S5_W:]
    for t in range(u_ref.shape[0]):
        u = u_ref[t]
        bu = _dot(u.astype(BF16), wb_ref[...])
        hr, hi = ar * hr - ai * hi + bu[:, :S5_W], ar * hi + ai * hr + bu[:, S5_W:]
        y = _dot(hr.astype(BF16), wc_ref[:S5_W, :]) + _dot(hi.astype(BF16), wc_ref[S5_W:, :])
        y_ref[t] = _s5_head(y, u, d_ref, gluw_ref, glub_ref)
    ht_ref[:, :S5_W] = hr
    ht_ref[:, S5_W:] = hi


def s5_sample(u_tm, h0, powre, powim, wb, wc, d, gluw, glub):
    s, b, _ = u_tm.shape
    return pl.pallas_call(
        _s5_step_kernel,
        out_shape=[jax.ShapeDtypeStruct((s, b, D_S5), F32), jax.ShapeDtypeStruct((b, 2 * S5_W), F32)],
        compiler_params=pltpu.CompilerParams(vmem_limit_bytes=VMEM_LIMIT),
        name="s5_sample",
    )(u_tm, h0, powre, powim, wb, wc, d, gluw, glub)


def s5_matrices(bb_re, bb_im, c_re, c_im):
    g, p, h = S5_GROUPS, S5_STATE, S5_GROUP
    eye = jnp.eye(g, dtype=F32)
    bd_in = lambda bb: jnp.einsum('gph,gk->ghkp', bb.reshape(g, p, h), eye).reshape(g * h, g * p)
    bd_out = lambda c: jnp.einsum('ghp,gk->gpkh', c, eye).reshape(g * p, g * h)
    wb = jnp.concatenate([bd_in(bb_re), bd_in(bb_im)], axis=1).astype(BF16)
    wc = jnp.concatenate([bd_out(c_re), -bd_out(c_im)], axis=0).astype(BF16)
    return wb, wc


RW_PACK = 8 * D_RW


def _head_ones(n):
    r = lax.broadcasted_iota(I32, (n, n), 0) // HEAD_DIM
    c = lax.broadcasted_iota(I32, (n, n), 1) // HEAD_DIM
    return jnp.where(r == c, 1.0, 0.0).astype(BF16)


def _seg_sum(x, ones_bd):
    hi, lo = _split(x)
    return _dot(hi, ones_bd) + _dot(lo, ones_bd)


def _softplus(x):
    return jnp.maximum(x, 0.0) + jnp.log(1.0 + jnp.exp(-jnp.abs(x)))


def _rw_pre_kernel(cols_ref, shift0_ref, mu_ref, w0_ref, a0_ref, wlr_ref, kk_ref, ka_ref, rk_ref,
                   o_ref, carry, *, chunk, valid):
    @pl.when(pl.program_id(1) == 0)
    def _():
        carry[...] = shift0_ref[...]

    cf = cols_ref[...]
    tm = cf.shape[0]
    row = lax.broadcasted_iota(I32, (tm, 1), 0)
    prev = jnp.where(row == 0, carry[...], pltpu.roll(cf, 1, 0))
    carry[...] = cf[tm - 1:tm, :]
    xs = cf + (prev - cf) * mu_ref[...]
    r, k, v = xs[:, :D_RW], xs[:, D_RW:2 * D_RW], xs[:, 2 * D_RW:3 * D_RW]
    lr = xs[:, 3 * D_RW:]
    lane = lax.broadcasted_iota(I32, lr.shape, 1)
    t = jnp.where(lane < RW_W_RANK, jnp.tanh(lr), jnp.where(lane < RW_W_RANK + RW_A_RANK, lr, _sigmoid(lr)))
    proj = _dot3(t, wlr_ref[...])
    w = -_softplus(-(w0_ref[...] + proj[:, :D_RW])) - 0.5
    logw = -jnp.exp(w)
    a = _sigmoid(a0_ref[...] + proj[:, D_RW:2 * D_RW])
    g = proj[:, 2 * D_RW:]
    ones_bd = _head_ones(D_RW)
    kk = k * kk_ref[...]
    kk = kk / jnp.maximum(jnp.sqrt(_seg_sum(kk * kk, ones_bd)), 1e-12)
    km = k * (1.0 + (a - 1.0) * ka_ref[...])
    bonus = _seg_sum(r * km * rk_ref[...], ones_bd) * v
    pos = row % chunk
    if valid < chunk:
        live = pos < valid
        zero = jnp.zeros_like(r)
        logw, kk, km, v, r = (jnp.where(live, logw, zero), jnp.where(live, kk, zero), jnp.where(live, km, zero),
                              jnp.where(live, v, zero), jnp.where(live, r, zero))
    gc = logw
    d = 1
    while d < chunk:
        gc = gc + jnp.where(pos >= d, pltpu.roll(gc, d, 0), 0.0)
        d *= 2
    eg = jnp.exp(gc)
    eng = jnp.exp(-gc)
    o_ref[:, 0 * D_RW:1 * D_RW] = r * eg
    o_ref[:, 1 * D_RW:2 * D_RW] = kk * jnp.exp(gc - logw)
    o_ref[:, 2 * D_RW:3 * D_RW] = kk * a * eng
    o_ref[:, 3 * D_RW:4 * D_RW] = km * eng
    o_ref[:, 4 * D_RW:5 * D_RW] = v
    o_ref[:, 5 * D_RW:6 * D_RW] = eg
    o_ref[:, 6 * D_RW:7 * D_RW] = bonus
    o_ref[:, 7 * D_RW:8 * D_RW] = g


def rwkv_prepare(cols, shift0, mu, w0, a0, wlr, k_k, k_a, r_k, tm, chunk, valid):
    b, s, c = cols.shape
    const = lambda a: pl.BlockSpec(a.shape, lambda i, j: (0,) * a.ndim)
    return pl.pallas_call(
        functools.partial(_rw_pre_kernel, chunk=chunk, valid=valid),
        grid=(b, s // tm),
        in_specs=[pl.BlockSpec((None, tm, c), lambda i, j: (i, j, 0)),
                  pl.BlockSpec((None, 1, c), lambda i, j: (i, 0, 0)),
                  const(mu), const(w0), const(a0), const(wlr), const(k_k), const(k_a), const(r_k)],
        out_specs=pl.BlockSpec((None, tm, RW_PACK), lambda i, j: (i, j, 0)),
        out_shape=jax.ShapeDtypeStruct((b, s, RW_PACK), F32),
        scratch_shapes=[pltpu.VMEM((1, c), F32)],
        compiler_params=_cparams(("arbitrary", "arbitrary")),
        name="rwkv_prepare",
    )(cols, shift0, mu, w0, a0, wlr, k_k, k_a, r_k)


def _rw_chunk_kernel(x_ref, s0_ref, lnw_ref, lnb_ref, y_ref, st_ref, s_scr, y_scr, *, chunk):
    @pl.when(pl.program_id(1) == 0)
    def _():
        s_scr[...] = s0_ref[...]

    rows = x_ref.shape[0]
    hd = HEAD_DIM
    ri = lax.broadcasted_iota(I32, (chunk, chunk), 0)
    ci = lax.broadcasted_iota(I32, (chunk, chunk), 1)
    strict = ri > ci
    incl = ri >= ci
    eye = jnp.where(ri == ci, 1.0, 0.0)
    ek = lax.broadcasted_iota(I32, (hd, hd), 0) == lax.broadcasted_iota(I32, (hd, hd), 1)
    eye_k = jnp.where(ek, 1.0, 0.0)
    n_sq = chunk.bit_length() - 2
    n_chunks = rows // chunk
    probs = [(c, h) for c in range(n_chunks) for h in range(N_RW_HEADS)]
    each = lambda f: [f(i) for i in range(len(probs))]
    bf = lambda a: a.astype(BF16)

    def col(i, j):
        c, h = probs[i]
        return x_ref[c * chunk:(c + 1) * chunk, j * D_RW + h * hd:j * D_RW + (h + 1) * hd]

    rt, kt, bt, km, v = (each(lambda i: col(i, j)) for j in range(5))
    btb, kmb, vb = each(lambda i: bf(bt[i])), each(lambda i: bf(km[i])), each(lambda i: bf(v[i]))
    lhs = each(lambda i: bf(jnp.concatenate([kt[i], rt[i]], axis=0)))
    gb = each(lambda i: _dot_nt(lhs[i], btb[i]))
    gk = each(lambda i: _dot_nt(lhs[i], kmb[i]))
    a_bb = each(lambda i: jnp.where(strict, gb[i][:chunk], 0.0))
    a_rb = each(lambda i: bf(jnp.where(incl, gb[i][chunk:], 0.0)))
    a_kr = each(lambda i: bf(jnp.concatenate([jnp.where(strict, gk[i][:chunk], 0.0),
                                              jnp.where(incl, gk[i][chunk:], 0.0)], axis=0)))
    av = each(lambda i: _dot(a_kr[i], vb[i]))
    minv = each(lambda i: eye - a_bb[i])
    p = a_bb
    for _ in range(n_sq):
        pb = each(lambda i: bf(p[i]))
        p = each(lambda i: _dot(pb[i], pb[i]))
        minv = each(lambda i: minv[i] + _dot(bf(minv[i]), bf(p[i])))
    minvb = each(lambda i: bf(minv[i]))
    khat = each(lambda i: _dot(minvb[i], bf(kt[i])))
    p1 = each(lambda i: _dot(minvb[i], bf(av[i][:chunk])))
    rhat = each(lambda i: rt[i] - _dot(a_rb[i], bf(khat[i])))
    y1 = each(lambda i: av[i][chunk:] - _dot(a_rb[i], bf(p1[i])))
    tb = each(lambda i: _dot(bf(jnp.concatenate([khat[i].T, p1[i].T], axis=0)), btb[i]))
    vk = each(lambda i: _dot(bf(v[i].T), kmb[i]))

    def eg_last(i):
        c, h = probs[i]
        return x_ref[(c + 1) * chunk - 1:(c + 1) * chunk, 5 * D_RW + h * hd:5 * D_RW + (h + 1) * hd]

    gmat = each(lambda i: (eye_k - tb[i][:hd]) * eg_last(i))
    umat = each(lambda i: (vk[i] - tb[i][hd:]) * eg_last(i))
    state = [s_scr[h] for h in range(N_RW_HEADS)]
    for i, (c, h) in enumerate(probs):
        y_scr[c * chunk:(c + 1) * chunk, h * hd:(h + 1) * hd] = _dot3_nt(rhat[i], state[h]) + y1[i]
        state[h] = _dot3(state[h], gmat[i]) + umat[i]
    for h in range(N_RW_HEADS):
        s_scr[h] = state[h]
    st_ref[...] = s_scr[...]
    y = y_scr[...]
    ones_bd = _head_ones(D_RW)
    mean = _seg_sum(y, ones_bd) * (1.0 / hd)
    yc = y - mean
    var = _seg_sum(yc * yc, ones_bd) * (1.0 / hd)
    yn = yc * lax.rsqrt(var + RW_GN_EPS) * lnw_ref[...] + lnb_ref[...]
    y_ref[...] = (yn + x_ref[:, 6 * D_RW:7 * D_RW]) * x_ref[:, 7 * D_RW:8 * D_RW]


def rwkv_chunked(packed, state0, ln_w, ln_b, rows, chunk):
    b, s, _ = packed.shape
    const = lambda a: pl.BlockSpec(a.shape, lambda i, j: (0,) * a.ndim)
    st_spec = pl.BlockSpec((None, N_RW_HEADS, HEAD_DIM, HEAD_DIM), lambda i, j: (i, 0, 0, 0))
    return pl.pallas_call(
        functools.partial(_rw_chunk_kernel, chunk=chunk),
        grid=(b, s // rows),
        in_specs=[pl.BlockSpec((None, rows, RW_PACK), lambda i, j: (i, j, 0)), st_spec, const(ln_w), const(ln_b)],
        out_specs=[pl.BlockSpec((None, rows, D_RW), lambda i, j: (i, j, 0)), st_spec],
        out_shape=[jax.ShapeDtypeStruct((b, s, D_RW), F32),
                   jax.ShapeDtypeStruct((b, N_RW_HEADS, HEAD_DIM, HEAD_DIM), F32)],
        scratch_shapes=[pltpu.VMEM((N_RW_HEADS, HEAD_DIM, HEAD_DIM), F32), pltpu.VMEM((rows, D_RW), F32)],
        compiler_params=_cparams(("arbitrary", "arbitrary")),
        name="rwkv_chunked",
    )(packed, state0, ln_w, ln_b)


def rwkv_lowrank_matrix(w2, a2, g2):
    z = lambda r: jnp.zeros((r, D_RW), F32)
    return jnp.concatenate([
        jnp.concatenate([w2, z(RW_W_RANK), z(RW_W_RANK)], axis=1),
        jnp.concatenate([z(RW_A_RANK), a2, z(RW_A_RANK)], axis=1),
        jnp.concatenate([z(RW_G_RANK), z(RW_G_RANK), g2], axis=1)], axis=0)


def _float_key(x):
    b = lax.bitcast_convert_type(x, I32)
    return jnp.where(b < 0, b ^ jnp.int32(0x7FFFFFFF), b)


def _py_key(v):
    import numpy as np
    b = int(np.float32(v).view(np.int32))
    return b ^ 0x7FFFFFFF if b < 0 else b


KEY_NEG_INF = _py_key(NEG_INF)


def _count_ge(sc_ref, nc, cand):
    _, r, wc = sc_ref.shape

    def body(c, acc):
        return acc + jnp.where(sc_ref[c] >= cand, 1.0, 0.0)

    acc = lax.fori_loop(0, nc, body, jnp.zeros((r, wc), F32))
    return jnp.sum(acc, axis=1, keepdims=True)


def _select_threshold(sc_ref, nc, k):
    _, r, wc = sc_ref.shape

    def bit_body(i, carry):
        t, cge = carry
        cand = t + jnp.left_shift(jnp.int32(1), 31 - i)
        cnt = _count_ge(sc_ref, nc, cand)
        ok = cnt >= k
        return jnp.where(ok, cand, t), jnp.where(ok, cnt, cge)

    t0 = jnp.full((r, 1), INT_MIN, I32)
    cge0 = jnp.zeros((r, 1), F32) + jnp.asarray(nc * wc, F32)
    return lax.fori_loop(0, 32, bit_body, (t0, cge0))


def _resolve_ties(sc_ref, nc, t, need):
    _, r, wc = sc_ref.shape
    upper = jnp.where(lax.broadcasted_iota(I32, (wc, wc), 0) < lax.broadcasted_iota(I32, (wc, wc), 1), 1.0, 0.0)

    def body(c, offs):
        x = sc_ref[c]
        e = x == t
        ef = jnp.where(e, 1.0, 0.0)
        rank = _dot(ef, upper) + offs
        sc_ref[c] = jnp.where(e, jnp.where(rank >= need, jnp.int32(INT_MIN), x), x)
        return offs + jnp.sum(ef, axis=1, keepdims=True)

    lax.fori_loop(0, nc, body, jnp.zeros((r, 1), F32))


def _topk_mask_prepare(sc_ref, nc, k, valid_rows):
    t, cge = _select_threshold(sc_ref, nc, k)
    r = t.shape[0]
    live = lax.broadcasted_iota(I32, (r, 1), 0) < valid_rows
    tied = live & (cge > k) & (t != KEY_NEG_INF)

    @pl.when(jnp.max(jnp.where(tied, 1.0, 0.0)) > 0.0)
    def _():
        cgt = _count_ge(sc_ref, nc, t + 1)
        _resolve_ties(sc_ref, nc, t, k - cgt)

    return t


def _indexer_total(s4, wcols, rows):
    tot = jnp.maximum(s4[:rows], 0.0) * wcols[0]
    for h in range(1, N_IDX_HEADS):
        tot = tot + jnp.maximum(s4[h * rows:(h + 1) * rows], 0.0) * wcols[h]
    return jnp.where(tot == 0.0, 0.0, tot)


SUBLANES = 8


def _col_tree(x, op):
    parts = [x[i:i + SUBLANES] for i in range(0, x.shape[0], SUBLANES)]
    while len(parts) > 1:
        parts = [op(parts[i], parts[i + 1]) for i in range(0, len(parts) - 1, 2)] + (parts[-1:] if len(parts) % 2 else [])
    return parts[0]


def _col_sum(x):
    return _col_tree(x, jnp.add)


def _col_max(x):
    return _col_tree(x, jnp.maximum)


def _count_ge_t(sc_ref, nc, cand):
    _, wc, nq = sc_ref.shape

    def body(c, acc):
        return acc + _col_sum(jnp.where(sc_ref[c] >= cand, 1.0, 0.0))

    acc = lax.fori_loop(0, nc, body, jnp.zeros((SUBLANES, nq), F32))
    return jnp.sum(acc, axis=0, keepdims=True)


def _topk_mask_prepare_t(sc_ref, nc, k):
    _, wc, nq = sc_ref.shape

    def bit_body(i, carry):
        t, cge = carry
        cand = t + jnp.left_shift(jnp.int32(1), 31 - i)
        cnt = _count_ge_t(sc_ref, nc, cand)
        ok = cnt >= k
        return jnp.where(ok, cand, t), jnp.where(ok, cnt, cge)

    t0 = jnp.full((1, nq), INT_MIN, I32)
    cge0 = jnp.zeros((1, nq), F32) + jnp.asarray(nc * wc, F32)
    t, cge = lax.fori_loop(0, 32, bit_body, (t0, cge0))
    tied = (cge > k) & (t != KEY_NEG_INF)

    @pl.when(jnp.max(jnp.where(tied, 1.0, 0.0)) > 0.0)
    def _():
        need = k - _count_ge_t(sc_ref, nc, t + 1)
        ri = lax.broadcasted_iota(I32, (wc, wc), 0)
        ci = lax.broadcasted_iota(I32, (wc, wc), 1)
        lower = jnp.where(ci < ri, 1.0, 0.0)

        def body(c, offs):
            x = sc_ref[c]
            e = x == t
            ef = jnp.where(e, 1.0, 0.0)
            rank = _dot(lower, ef) + offs
            sc_ref[c] = jnp.where(e, jnp.where(rank >= need, jnp.int32(INT_MIN), x), x)
            return offs + jnp.sum(ef, axis=0, keepdims=True)

        lax.fori_loop(0, nc, body, jnp.zeros((1, nq), F32))

    return t


def _dsa_prompt_kernel(q_ref, kb_ref, vt_ref, qi3_ref, ki3_ref, kiw_ref, o_ref, sc, *, k_sel):
    j = pl.program_id(1)
    qb, wc = Q_BLOCK, KEY_CHUNK
    nck = (j * qb + qb + wc - 1) // wc
    qpos = j * qb + lax.broadcasted_iota(I32, (1, qb), 1)
    keybase = lax.broadcasted_iota(I32, (wc, 1), 0)

    lhs = jnp.concatenate([qi3_ref[:, 2 * LANES * h:2 * LANES * (h + 1)] for h in range(N_IDX_HEADS)], axis=0)
    kiw_t = kiw_ref[...].T
    wrows = [kiw_t[IDX_DIM + h:IDX_DIM + h + 1, :] for h in range(N_IDX_HEADS)]

    def score_body(c, carry):
        kc = ki3_ref[pl.ds(pl.multiple_of(c * wc, wc), wc), :]
        s4 = _dot_nt(kc, lhs)
        tot = jnp.maximum(s4[:, :qb], 0.0) * wrows[0]
        for h in range(1, N_IDX_HEADS):
            tot = tot + jnp.maximum(s4[:, h * qb:(h + 1) * qb], 0.0) * wrows[h]
        tot = jnp.where(tot == 0.0, 0.0, tot)
        tot = jnp.where(c * wc + keybase <= qpos, tot, NEG_INF)
        sc[c] = _float_key(tot)
        return carry

    lax.fori_loop(0, nck, score_body, 0)
    t = _topk_mask_prepare_t(sc, nck, k_sel)

    lane = lax.broadcasted_iota(I32, (qb, LANES), 1)
    n_pairs = N_ATT_HEADS // 2
    qpair = []
    for pr in range(n_pairs):
        blk = q_ref[:, pr * LANES:(pr + 1) * LANES]
        zero = jnp.zeros_like(blk)
        qpair.append(jnp.concatenate([jnp.where(lane < HEAD_DIM, blk, zero), jnp.where(lane < HEAD_DIM, zero, blk)],
                                     axis=0))

    def att_body(c, carry):
        ms, ls, accs = carry
        off = pl.multiple_of(c * wc, wc)
        keep = (sc[c] >= t) & (c * wc + keybase <= qpos)
        bias = jnp.where(keep, 0.0, NEG_INF)
        bias2 = jnp.concatenate([bias, bias], axis=1)
        ss = [_dot_nt(kb_ref[pl.ds(off, wc), pr * LANES:(pr + 1) * LANES], qpair[pr]) + bias2
              for pr in range(n_pairs)]
        new_m = [jnp.maximum(ms[pr], jnp.max(_col_max(ss[pr]), axis=0, keepdims=True)) for pr in range(n_pairs)]
        alphas = [jnp.exp2(ms[pr] - new_m[pr]) for pr in range(n_pairs)]
        ps = [jnp.exp2(ss[pr] - new_m[pr]) for pr in range(n_pairs)]
        new_l = [alphas[pr] * ls[pr] + jnp.sum(_col_sum(ps[pr]), axis=0, keepdims=True) for pr in range(n_pairs)]
        new_acc = [alphas[pr] * accs[pr]
                   + _dot(vt_ref[pr * LANES:(pr + 1) * LANES, pl.ds(off, wc)], ps[pr].astype(BF16))
                   for pr in range(n_pairs)]
        return tuple(new_m), tuple(new_l), tuple(new_acc)

    init = (tuple(jnp.full((1, 2 * qb), NEG_INF, F32) for _ in range(n_pairs)),
            tuple(jnp.zeros((1, 2 * qb), F32) for _ in range(n_pairs)),
            tuple(jnp.zeros((LANES, 2 * qb), F32) for _ in range(n_pairs)))
    _, ls, accs = lax.fori_loop(0, nck, att_body, init)
    sub = lax.broadcasted_iota(I32, (LANES, qb), 0)
    for pr in range(n_pairs):
        o2 = accs[pr] / ls[pr]
        o_ref[:, pr * LANES:(pr + 1) * LANES] = jnp.where(sub < HEAD_DIM, o2[:, :qb], o2[:, qb:]).T


def dsa_prompt(q, kb, vt, qi3, ki3, kiw):
    b, s, _ = q.shape
    k_sel = min(TOPK_MAX, s // 4)
    blk = lambda w: pl.BlockSpec((None, Q_BLOCK, w), lambda i, j: (i, j, 0))
    full = lambda w: pl.BlockSpec((None, s, w), lambda i, j: (i, 0, 0))
    return pl.pallas_call(
        functools.partial(_dsa_prompt_kernel, k_sel=k_sel),
        grid=(b, s // Q_BLOCK),
        in_specs=[blk(D_ATT), full(D_ATT), pl.BlockSpec((None, D_ATT, s), lambda i, j: (i, 0, 0)),
                  blk(N_IDX_HEADS * 2 * LANES), full(2 * LANES), blk(LANES)],
        out_specs=blk(D_ATT),
        out_shape=jax.ShapeDtypeStruct((b, s, D_ATT), F32),
        scratch_shapes=[pltpu.VMEM((s // KEY_CHUNK, KEY_CHUNK, Q_BLOCK), I32)],
        compiler_params=_cparams(("arbitrary", "arbitrary")),
        name="dsa_prompt",
    )(q, kb, vt, qi3, ki3, kiw)


SAMPLE_ROWS = 8
PAGE_GROUP = 8


def _dsa_sample_index_kernel(pt_ref, *refs, n_valid, k_sel):
    pages = refs[:PAGE_GROUP]
    qi3_ref, kiw_ref, ki3n_ref, sc_ref, thr_ref = refs[PAGE_GROUP:]
    g = pl.program_id(1)
    r = SAMPLE_ROWS
    n_chunks = sc_ref.shape[0]
    q3 = qi3_ref[...].astype(F32)
    qhi = jnp.concatenate([q3[:, 2 * LANES * h:2 * LANES * h + IDX_DIM] for h in range(N_IDX_HEADS)],
                          axis=0).astype(BF16)
    qlo = jnp.concatenate([q3[:, 2 * LANES * h + LANES:2 * LANES * h + LANES + IDX_DIM]
                           for h in range(N_IDX_HEADS)], axis=0).astype(BF16)
    wcols = [kiw_ref[:, IDX_DIM + h:IDX_DIM + h + 1] for h in range(N_IDX_HEADS)]

    def scores(khi, klo):
        return _dot_nt(qhi, khi) + (_dot_nt(qhi, klo) + _dot_nt(qlo, khi))

    for i in range(PAGE_GROUP):
        khi, klo = _split(pages[i][...])
        sc_ref[g * PAGE_GROUP + i] = _float_key(_indexer_total(scores(khi, klo), wcols, r))

    @pl.when(g == pl.num_programs(1) - 1)
    def _():
        khi = ki3n_ref[:, :IDX_DIM]
        klo = ki3n_ref[:, IDX_DIM:2 * IDX_DIM]
        tot = _indexer_total(scores(khi, klo), wcols, r)
        row = lax.broadcasted_iota(I32, tot.shape, 0)
        col = lax.broadcasted_iota(I32, tot.shape, 1)
        tot = jnp.where((col <= row) & (col < n_valid), tot, NEG_INF)
        sc_ref[n_chunks - 1] = _float_key(tot)
        t = _topk_mask_prepare(sc_ref, n_chunks, k_sel, n_valid)
        thr_ref[...] = jnp.broadcast_to(t, thr_ref.shape)


def dsa_sample_index(page_table, cache_kidx, layer, qi3, kiw, ki3_new, n_valid):
    b, n_pages = page_table.shape
    n_chunks = n_pages + 1
    k_sel = min(TOPK_MAX, (n_pages * PAGE_SIZE + n_valid) // 4)
    page_spec = lambda i: pl.BlockSpec((None, None, PAGE_SIZE, IDX_DIM),
                                       lambda bi, g, pt: (layer, pt[bi, g * PAGE_GROUP + i], 0, 0))
    per_b = lambda shape: pl.BlockSpec((None,) + shape, lambda bi, g, pt: (bi,) + (0,) * len(shape))
    r = SAMPLE_ROWS
    return pl.pallas_call(
        functools.partial(_dsa_sample_index_kernel, n_valid=n_valid, k_sel=k_sel),
        grid_spec=pltpu.PrefetchScalarGridSpec(
            num_scalar_prefetch=1,
            grid=(b, n_pages // PAGE_GROUP),
            in_specs=[page_spec(i) for i in range(PAGE_GROUP)]
            + [per_b((r, N_IDX_HEADS * 2 * LANES)), per_b((r, LANES)), per_b((PAGE_SIZE, 2 * LANES))],
            out_specs=[per_b((n_chunks, r, PAGE_SIZE)), per_b((r, PAGE_SIZE))],
        ),
        out_shape=[jax.ShapeDtypeStruct((b, n_chunks, r, PAGE_SIZE), I32),
                   jax.ShapeDtypeStruct((b, r, PAGE_SIZE), I32)],
        compiler_params=_cparams(("arbitrary", "arbitrary")),
        name="dsa_sample_index",
    )(page_table, *([cache_kidx] * PAGE_GROUP), qi3, kiw, ki3_new)


def _dsa_sample_attend_kernel(pt_ref, *refs, n_valid):
    kpages = refs[:PAGE_GROUP]
    vpages = refs[PAGE_GROUP:2 * PAGE_GROUP]
    (qbd_ref, sel_ref, seln_ref, thr_ref, kn_ref, vn_ref, o_ref,
     m_scr, l_scr, acc_scr, kd_scr, vd_scr) = refs[2 * PAGE_GROUP:]
    g = pl.program_id(1)
    r = SAMPLE_ROWS

    def lane_dense(page_ref, dense_scr):
        for h in range(N_ATT_HEADS):
            dense_scr[:, h * HEAD_DIM:(h + 1) * HEAD_DIM] = page_ref[:, h, :]
        return dense_scr[...].astype(BF16)

    @pl.when(g == 0)
    def _():
        m_scr[...] = jnp.full(m_scr.shape, NEG_INF, F32)
        l_scr[...] = jnp.zeros(l_scr.shape, F32)
        acc_scr[...] = jnp.zeros(acc_scr.shape, F32)

    qbd = qbd_ref[...]
    thr = thr_ref[...]
    per_head = lambda x: jnp.concatenate([x] * N_ATT_HEADS, axis=0)
    thr_all = per_head(thr)

    def update(sel, extra, k, v):
        mask = per_head(sel) >= thr_all
        if extra is not None:
            mask = mask & (per_head(extra) > 0)
        s = jnp.where(mask, _dot_nt(qbd, k), NEG_INF)
        m_prev = m_scr[...]
        m_new = jnp.maximum(m_prev, jnp.max(s, axis=1, keepdims=True))
        alpha = jnp.exp(m_prev - m_new)
        p = jnp.exp(s - m_new)
        l_scr[...] = alpha * l_scr[...] + jnp.sum(p, axis=1, keepdims=True)
        acc_scr[...] = alpha * acc_scr[...] + _dot(p.astype(BF16), v)
        m_scr[...] = m_new

    for i in range(PAGE_GROUP):
        update(sel_ref[i], None, lane_dense(kpages[i], kd_scr), lane_dense(vpages[i], vd_scr))

    @pl.when(g == pl.num_programs(1) - 1)
    def _():
        row = lax.broadcasted_iota(I32, thr.shape, 0)
        col = lax.broadcasted_iota(I32, thr.shape, 1)
        causal = jnp.where((col <= row) & (col < n_valid), 1, 0)
        update(seln_ref[...], causal, kn_ref[...], vn_ref[...])
        out = acc_scr[...] / l_scr[...]
        lane = lax.broadcasted_iota(I32, (r, D_ATT), 1) // HEAD_DIM
        tot = jnp.where(lane == 0, out[:r], 0.0)
        for h in range(1, N_ATT_HEADS):
            tot = tot + jnp.where(lane == h, out[h * r:(h + 1) * r], 0.0)
        o_ref[...] = tot


def dsa_sample_attend(page_table, cache_k, cache_v, layer, q_bd, sel, thr, k_new, v_new, n_valid):
    b, n_pages = page_table.shape
    r = SAMPLE_ROWS
    page_spec = lambda i: pl.BlockSpec((None, None, PAGE_SIZE, N_ATT_HEADS, HEAD_DIM),
                                       lambda bi, g, pt: (layer, pt[bi, g * PAGE_GROUP + i], 0, 0, 0))
    per_b = lambda shape: pl.BlockSpec((None,) + shape, lambda bi, g, pt: (bi,) + (0,) * len(shape))
    rows = N_ATT_HEADS * r
    return pl.pallas_call(
        functools.partial(_dsa_sample_attend_kernel, n_valid=n_valid),
        grid_spec=pltpu.PrefetchScalarGridSpec(
            num_scalar_prefetch=1,
            grid=(b, n_pages // PAGE_GROUP),
            in_specs=[page_spec(i) for i in range(PAGE_GROUP)] + [page_spec(i) for i in range(PAGE_GROUP)]
            + [per_b((rows, D_ATT)),
               pl.BlockSpec((None, PAGE_GROUP, r, PAGE_SIZE), lambda bi, g, pt: (bi, g, 0, 0)),
               pl.BlockSpec((None, None, r, PAGE_SIZE), lambda bi, g, pt: (bi, n_pages, 0, 0)),
               per_b((r, PAGE_SIZE)), per_b((PAGE_SIZE, D_ATT)), per_b((PAGE_SIZE, D_ATT))],
            out_specs=per_b((r, D_ATT)),
            scratch_shapes=[pltpu.VMEM((rows, 1), F32), pltpu.VMEM((rows, 1), F32), pltpu.VMEM((rows, D_ATT), F32),
                            pltpu.VMEM((PAGE_SIZE, D_ATT), F32), pltpu.VMEM((PAGE_SIZE, D_ATT), F32)],
        ),
        out_shape=jax.ShapeDtypeStruct((b, r, D_ATT), F32),
        compiler_params=_cparams(("arbitrary", "arbitrary")),
        name="dsa_sample_attend",
    )(page_table, *([cache_k] * PAGE_GROUP), *([cache_v] * PAGE_GROUP), q_bd, sel, sel, thr, k_new, v_new)


def pack_w_in(w_in):
    d = w_in.shape[0]
    pad = jnp.zeros((d, ZP_END - ZP_KIW - (OFF_RW - OFF_KI)), w_in.dtype)
    return jnp.concatenate([w_in[:, OFF_S5:OFF_KI], w_in[:, OFF_RW:], w_in[:, OFF_KI:OFF_RW], pad], axis=1).astype(BF16)


def rope_tables(pos):
    half = HEAD_DIM // 2
    inv = ROPE_THETA ** (-jnp.arange(half, dtype=F32) / half)
    ang = pos.astype(F32)[:, None] * inv[None, :]
    cos, sin = jnp.cos(ang), jnp.sin(ang)
    return jnp.tile(jnp.concatenate([cos, cos], 1), (1, 2)), jnp.tile(jnp.concatenate([-sin, sin], 1), (1, 2))


def block_diag_queries(q):
    b, r, d = q.shape
    head_of_lane = jnp.arange(d) // HEAD_DIM
    keep = head_of_lane[None, :] == jnp.arange(N_ATT_HEADS)[:, None]
    return jnp.where(keep[None, :, None, :], q[:, None], jnp.zeros((), q.dtype)).reshape(b, N_ATT_HEADS * r, d)


def _pad_rows(a, rows):
    return jnp.pad(a, ((0, 0), (0, rows - a.shape[1])) + ((0, 0),) * (a.ndim - 2))


Q_SCALE = HEAD_DIM ** -0.5
Q_SCALE_EXP2 = Q_SCALE * math.log2(math.e)
PROMPT_TILE = 256
RW_PRE_TILE = 512
RW_STEP_ROWS = 256


def kernel(x_prompt, x_sample, c_prompt, c_sample, cache_k, cache_v, cache_kidx, state_s5_re, state_s5_im, state_rwkv, state_rwkv_shift, page_table, ada_w, ada_b, norm_pre, norm_post, ffn_wi, ffn_wo, w_in, w_out, s5_a_re, s5_a_im, s5_log_dt, s5_b_re, s5_b_im, s5_c_re, s5_c_im, s5_d, s5_glu_w, s5_glu_b, rw_mu, rw_w0, rw_w2, rw_a0, rw_a2, rw_g2, rw_k_k, rw_k_a, rw_r_k, rw_ln_w, rw_ln_b):
    bp, sp, d = x_prompt.shape
    bs, ss, _ = x_sample.shape
    depth = ada_w.shape[0]
    past = page_table.shape[1] * PAGE_SIZE
    np_tok, ns_tok = bp * sp, bs * ss
    tpb = sp // PROMPT_TILE
    row1 = lambda a: a.reshape(1, -1)

    c_all = _pad_rows(jnp.concatenate([c_prompt, c_sample], axis=0)[None], -(-(bp + bs) // 8) * 8)[0]
    mod = ada_mod(c_all, ada_w, ada_b).reshape(depth, c_all.shape[0], N_SUB, 3, d)

    cos_p, sin_p = rope_tables(jnp.arange(sp))
    cos_s, sin_s = (jnp.tile(t, (bs, 1)) for t in rope_tables(past + jnp.arange(ss)))

    xp = x_prompt.reshape(np_tok, d)
    xs = x_sample.reshape(ns_tok, d)
    outs_p, outs_s = [], []
    for l in range(depth):
        mod_p = mod[l, :bp]
        mod_s = jnp.repeat(mod[l, bp:bp + bs], ss, axis=0)
        pm = lambda i, j: mod_p[:, i, j][:, None, :]
        sm = lambda i, j: mod_s[:, i, j][None]
        npre = lambda i: row1(norm_pre[l, i])
        npost = lambda i: row1(norm_post[l, i])
        wi0, wo0 = ffn_wi[l, 0].astype(BF16), ffn_wo[l, 0].astype(BF16)
        wi1, wo1 = ffn_wi[l, 1].astype(BF16), ffn_wo[l, 1].astype(BF16)
        w_in_p = pack_w_in(w_in[l])
        w_out_b = w_out[l].astype(BF16)
        powre, powim, bb_re, bb_im = s5_discretise(s5_a_re[l], s5_a_im[l], s5_log_dt[l], s5_b_re[l], s5_b_im[l],
                                                   rows=S5_CHUNK)
        wb, wc = s5_matrices(bb_re, bb_im, s5_c_re[l], s5_c_im[l])
        s5_tail = (powre, powim, wb, wc, row1(s5_d[l]), s5_glu_w[l].astype(BF16), row1(s5_glu_b[l]))
        wlr = rwkv_lowrank_matrix(rw_w2[l], rw_a2[l], rw_g2[l])
        rw_pre = (row1(rw_mu[l]), row1(rw_w0[l]), row1(rw_a0[l]), wlr, row1(rw_k_k[l]), row1(rw_k_a[l]),
                  row1(rw_r_k[l]))
        ln = (row1(rw_ln_w[l]), row1(rw_ln_b[l]))

        xp = ffn_block(xp, pm(0, 0), pm(0, 1), pm(0, 2), npre(0), npost(0), wi0, wo0, PROMPT_TILE, tpb)
        u, q, k, kb, v, _, qi3, kiw, ki3, rw, vt = inproj_block(xp, pm(1, 0), pm(1, 1), npre(1), w_in_p, cos_p, sin_p,
                                                                PROMPT_TILE, tpb, tpb, Q_SCALE_EXP2)
        seq = lambda a: a.reshape(bp, sp, a.shape[-1])
        y_s5, h_s5 = s5_prompt(seq(u), jnp.zeros((bp, 1, 2 * S5_W), F32), *s5_tail)
        y_att = dsa_prompt(seq(q), seq(kb), vt, seq(qi3), seq(ki3), seq(kiw))
        cols = seq(rw)
        packed = rwkv_prepare(cols, jnp.zeros((bp, 1, N_RW_COLS), F32), *rw_pre, RW_PRE_TILE, RW_CHUNK, RW_CHUNK)
        y_rw, st_rw = rwkv_chunked(packed, jnp.zeros((bp, N_RW_HEADS, HEAD_DIM, HEAD_DIM), F32), *ln,
                                   RW_STEP_ROWS, RW_CHUNK)
        flat = lambda a: a.reshape(np_tok, a.shape[-1])
        xp = outproj_block(xp, flat(y_s5), flat(y_att), flat(y_rw), pm(1, 2), npost(1), w_out_b, PROMPT_TILE, tpb)
        xp = ffn_block(xp, pm(2, 0), pm(2, 1), pm(2, 2), npre(2), npost(2), wi1, wo1, PROMPT_TILE, tpb)
        outs_p.append((k.reshape(bp, sp, N_ATT_HEADS, HEAD_DIM), v.reshape(bp, sp, N_ATT_HEADS, HEAD_DIM),
                       seq(kiw)[:, :, :IDX_DIM],
                       h_s5[:, 0, :S5_W].reshape(bp, S5_GROUPS, S5_STATE),
                       h_s5[:, 0, S5_W:].reshape(bp, S5_GROUPS, S5_STATE), st_rw, cols[:, sp - 1]))

        xs = ffn_block(xs, sm(0, 0), sm(0, 1), sm(0, 2), npre(0), npost(0), wi0, wo0, ns_tok, 1)
        u, q, k, kb, v, vb, qi3, kiw, ki3, rw, _ = inproj_block(xs, sm(1, 0), sm(1, 1), npre(1), w_in_p, cos_s, sin_s,
                                                                ns_tok, 1, 1, Q_SCALE)
        seq = lambda a: a.reshape(bs, ss, a.shape[-1])
        h0 = jnp.concatenate([state_s5_re[l].reshape(bs, S5_W), state_s5_im[l].reshape(bs, S5_W)], axis=1)
        y_s5, h_s5 = s5_sample(jnp.swapaxes(seq(u), 0, 1), h0, *s5_tail)
        y_s5 = jnp.swapaxes(y_s5, 0, 1)
        sel, thr = dsa_sample_index(page_table, cache_kidx, l, _pad_rows(seq(qi3), SAMPLE_ROWS),
                                    _pad_rows(seq(kiw), SAMPLE_ROWS), _pad_rows(seq(ki3), PAGE_SIZE), ss)
        q_bd = block_diag_queries(_pad_rows(seq(q), SAMPLE_ROWS))
        y_att = dsa_sample_attend(page_table, cache_k, cache_v, l, q_bd, sel, thr, _pad_rows(seq(kb), PAGE_SIZE),
                                  _pad_rows(seq(vb), PAGE_SIZE), ss)[:, :ss]
        cols = seq(rw)
        packed = rwkv_prepare(_pad_rows(cols, RW_CHUNK), state_rwkv_shift[l].reshape(bs, 1, N_RW_COLS), *rw_pre,
                              RW_CHUNK, RW_CHUNK, ss)
        y_rw, st_rw = rwkv_chunked(packed, state_rwkv[l], *ln, RW_CHUNK, RW_CHUNK)
        flat = lambda a: a.reshape(ns_tok, a.shape[-1])
        xs = outproj_block(xs, flat(y_s5), flat(y_att), flat(y_rw[:, :ss]), sm(1, 2), npost(1), w_out_b, ns_tok, 1)
        xs = ffn_block(xs, sm(2, 0), sm(2, 1), sm(2, 2), npre(2), npost(2), wi1, wo1, ns_tok, 1)
        outs_s.append((k.reshape(bs, ss, N_ATT_HEADS, HEAD_DIM), v.reshape(bs, ss, N_ATT_HEADS, HEAD_DIM),
                       seq(kiw)[:, :, :IDX_DIM],
                       h_s5[:, :S5_W].reshape(bs, S5_GROUPS, S5_STATE),
                       h_s5[:, S5_W:].reshape(bs, S5_GROUPS, S5_STATE), st_rw, cols[:, ss - 1]))

    stack = lambda outs, i: jnp.stack([o[i] for o in outs])
    return ((xp.reshape(bp, sp, d), xs.reshape(bs, ss, d))
            + tuple(stack(outs_p, i) for i in range(7)) + tuple(stack(outs_s, i) for i in range(7)))
```

```python
import functools
import math

import jax
import jax.numpy as jnp
from jax import lax
from jax.experimental import pallas as pl
from jax.experimental.pallas import tpu as pltpu

F32 = jnp.float32
BF16 = jnp.bfloat16
I32 = jnp.int32

D_MODEL = 1024
PAGE_SIZE = 128
D_S5 = 256
D_ATT = 512
D_RW = 256
S5_GROUP = 16
S5_GROUPS = D_S5 // S5_GROUP
S5_STATE = 64
S5_W = S5_GROUPS * S5_STATE
HEAD_DIM = 64
N_ATT_HEADS = D_ATT // HEAD_DIM
N_IDX_HEADS = 4
IDX_DIM = 64
TOPK_MAX = 256
Q_BLOCK = 128
ROPE_THETA = 10000.0
N_RW_HEADS = D_RW // HEAD_DIM
RW_W_RANK = 32
RW_A_RANK = 32
RW_G_RANK = 64
RW_GN_EPS = 64e-5
N_RW_COLS = 3 * D_RW + RW_W_RANK + RW_A_RANK + RW_G_RANK
D_FF = 2816
HALF_STEP = 0.5
RMS_EPS = 1e-6
N_SUB = 3
NEG_INF = -1e30
OFF_S5 = 0
OFF_Q = OFF_S5 + D_S5
OFF_K = OFF_Q + D_ATT
OFF_V = OFF_K + D_ATT
OFF_QI = OFF_V + D_ATT
OFF_KI = OFF_QI + N_IDX_HEADS * IDX_DIM
OFF_WI = OFF_KI + IDX_DIM
OFF_RW = OFF_WI + N_IDX_HEADS
N_IN = OFF_RW + N_RW_COLS

LANES = 128
VMEM_LIMIT = 56 * 1024 * 1024
INT_MIN = -(2 ** 31)

ZP_U, ZP_Q, ZP_K, ZP_V, ZP_QI, ZP_RW, ZP_KIW, ZP_END = 0, 256, 768, 1280, 1792, 2048, 2944, 3072

RW_CHUNK = 64
S5_CHUNK = 128
KEY_CHUNK = 512


def _cparams(sem):
    return pltpu.CompilerParams(dimension_semantics=sem, vmem_limit_bytes=VMEM_LIMIT)


def _dot(a, b):
    return jnp.dot(a, b, preferred_element_type=F32)


def _dot_nt(a, b):
    return lax.dot_general(a, b, (((1,), (1,)), ((), ())), preferred_element_type=F32)


def _split(x):
    hi = x.astype(BF16)
    lo = (x - hi.astype(F32)).astype(BF16)
    return hi, lo


def _dot3(a, b):
    ah, al = _split(a)
    bh, bl = _split(b)
    return _dot(ah, bh) + (_dot(ah, bl) + _dot(al, bh))


def _dot3_nt(a, b):
    ah, al = _split(a)
    bh, bl = _split(b)
    return _dot_nt(ah, bh) + (_dot_nt(ah, bl) + _dot_nt(al, bh))


def _mm3(a, b):
    return _dot(a[0], b[0]) + (_dot(a[0], b[1]) + _dot(a[1], b[0]))


def _mm3_nt(a, b):
    return _dot_nt(a[0], b[0]) + (_dot_nt(a[0], b[1]) + _dot_nt(a[1], b[0]))


def _rms(x, g):
    return x * lax.rsqrt(jnp.mean(x * x, axis=-1, keepdims=True) + RMS_EPS) * g


def _sigmoid(x):
    return 1.0 / (1.0 + jnp.exp(-x))


def _ada_kernel(c_ref, w_ref, b_ref, o_ref):
    c = c_ref[...]
    h = (c * _sigmoid(c)).astype(BF16)
    o_ref[...] = _dot(h, w_ref[...].astype(BF16)) + b_ref[...]


def ada_mod(c_all, ada_w, ada_b, tn=1152):
    depth, d, n = ada_w.shape
    rows = c_all.shape[0]
    return pl.pallas_call(
        _ada_kernel,
        grid=(depth, n // tn),
        in_specs=[pl.BlockSpec((rows, d), lambda l, j: (0, 0)),
                  pl.BlockSpec((None, d, tn), lambda l, j: (l, 0, j)),
                  pl.BlockSpec((None, 1, tn), lambda l, j: (l, 0, j))],
        out_specs=pl.BlockSpec((None, rows, tn), lambda l, j: (l, 0, j)),
        out_shape=jax.ShapeDtypeStruct((depth, rows, n), F32),
        compiler_params=_cparams(("arbitrary", "arbitrary")),
        name="ada_mod",
    )(c_all, ada_w, ada_b.reshape(depth, 1, n))


def _mod_spec(mod, tiles_per_group):
    r = mod.shape[1]
    return pl.BlockSpec((None, r, mod.shape[2]), lambda i: (i // tiles_per_group, 0, 0))


def _ffn_kernel(x_ref, shift_ref, scale_ref, gate_ref, gpre_ref, gpost_ref, wi_ref, wo_ref, o_ref, *, res_w):
    x = x_ref[...]
    h = (_rms(x, gpre_ref[...]) * (1.0 + scale_ref[...]) + shift_ref[...]).astype(BF16)
    g = _dot(h, wi_ref[:, :D_FF])
    u = _dot(h, wi_ref[:, D_FF:])
    a = (g * _sigmoid(g) * u).astype(BF16)
    o = _dot(a, wo_ref[...])
    o_ref[...] = x + (res_w * gate_ref[...]) * _rms(o, gpost_ref[...])


def ffn_block(x, shift, scale, gate, g_pre, g_post, wi, wo, tm, tiles_per_group):
    n, d = x.shape
    row = pl.BlockSpec((tm, d), lambda i: (i, 0))
    const = lambda a: pl.BlockSpec(a.shape, lambda i: (0,) * a.ndim)
    return pl.pallas_call(
        functools.partial(_ffn_kernel, res_w=HALF_STEP),
        grid=(n // tm,),
        in_specs=[row, _mod_spec(shift, tiles_per_group), _mod_spec(scale, tiles_per_group),
                  _mod_spec(gate, tiles_per_group), const(g_pre), const(g_post), const(wi), const(wo)],
        out_specs=row,
        out_shape=jax.ShapeDtypeStruct((n, d), F32),
        compiler_params=_cparams(("arbitrary",)),
        name="ffn_block",
    )(x, shift, scale, gate, g_pre, g_post, wi, wo)


def _rot_block(blk, cos, sin, lane):
    partner = jnp.where((lane & 32) == 0, pltpu.roll(blk, LANES - 32, 1), pltpu.roll(blk, 32, 1))
    return blk * cos + partner * sin


def _inproj_kernel(x_ref, shift_ref, scale_ref, gpre_ref, w_ref, cos_ref, sin_ref,
                   u_ref, q_ref, k_ref, kb_ref, v_ref, vb_ref, qi3_ref, kiw_ref, ki3_ref, rw_ref, vt_ref, *, q_scale):
    x = x_ref[...]
    h = (_rms(x, gpre_ref[...]) * (1.0 + scale_ref[...]) + shift_ref[...]).astype(BF16)
    z = _dot(h, w_ref[...])
    cos = cos_ref[...]
    sin = sin_ref[...]
    lane = lax.broadcasted_iota(I32, cos.shape, 1)
    lo_half = lane < 64
    rot = lambda off: _rot_block(z[:, off:off + LANES], cos, sin, lane)

    u_ref[...] = z[:, ZP_U:ZP_Q]
    for j in range(D_ATT // LANES):
        q_ref[:, j * LANES:(j + 1) * LANES] = (rot(ZP_Q + j * LANES) * q_scale).astype(BF16)
        kr = rot(ZP_K + j * LANES)
        k_ref[:, j * LANES:(j + 1) * LANES] = kr
        kb_ref[:, j * LANES:(j + 1) * LANES] = kr.astype(BF16)
    v = z[:, ZP_V:ZP_QI]
    v_ref[...] = v
    vb_ref[...] = v.astype(BF16)
    vt_ref[...] = v.T.astype(BF16)
    for j in range(N_IDX_HEADS * IDX_DIM // LANES):
        qr = rot(ZP_QI + j * LANES)
        hi = qr.astype(BF16).astype(F32)
        lo = qr - hi
        hi_sw = pltpu.roll(hi, 64, 1)
        lo_sw = pltpu.roll(lo, 64, 1)
        zero = jnp.zeros_like(hi)
        for half in range(2):
            base = (2 * j + half) * 2 * LANES
            a, b = (hi, lo) if half == 0 else (hi_sw, lo_sw)
            a_sw = hi_sw if half == 0 else hi
            qi3_ref[:, base:base + LANES] = jnp.where(lo_half, a, a_sw).astype(BF16)
            qi3_ref[:, base + LANES:base + 2 * LANES] = jnp.where(lo_half, b, zero).astype(BF16)
    raw = z[:, ZP_KIW:ZP_END]
    kiw = jnp.where(lo_half, _rot_block(raw, cos, sin, lane), raw)
    kiw_ref[...] = kiw
    hi = kiw.astype(BF16).astype(F32)
    lo = kiw - hi
    ki3_ref[:, :LANES] = jnp.where(lo_half, hi, pltpu.roll(lo, 64, 1)).astype(BF16)
    ki3_ref[:, LANES:] = jnp.where(lo_half, hi, jnp.zeros_like(hi)).astype(BF16)
    rw_ref[...] = z[:, ZP_RW:ZP_KIW]


def inproj_block(x, shift, scale, g_pre, w, cos_t, sin_t, tm, tiles_per_group, pos_tiles, q_scale):
    n, d = x.shape
    row = lambda width: pl.BlockSpec((tm, width), lambda i: (i, 0))
    const = lambda a: pl.BlockSpec(a.shape, lambda i: (0,) * a.ndim)
    tab = pl.BlockSpec((tm, LANES), lambda i: (i % pos_tiles, 0))
    widths = [(D_S5, F32), (D_ATT, BF16), (D_ATT, F32), (D_ATT, BF16), (D_ATT, F32), (D_ATT, BF16),
              (N_IDX_HEADS * 2 * LANES, BF16), (LANES, F32), (2 * LANES, BF16), (N_RW_COLS, F32)]
    return pl.pallas_call(
        functools.partial(_inproj_kernel, q_scale=q_scale),
        grid=(n // tm,),
        in_specs=[row(d), _mod_spec(shift, tiles_per_group), _mod_spec(scale, tiles_per_group),
                  const(g_pre), const(w), tab, tab],
        out_specs=[row(wd) for wd, _ in widths]
        + [pl.BlockSpec((None, D_ATT, tm), lambda i: (i // pos_tiles, 0, i % pos_tiles))],
        out_shape=[jax.ShapeDtypeStruct((n, wd), dt) for wd, dt in widths]
        + [jax.ShapeDtypeStruct((n // (pos_tiles * tm), D_ATT, pos_tiles * tm), BF16)],
        compiler_params=_cparams(("arbitrary",)),
        name="inproj_block",
    )(x, shift, scale, g_pre, w, cos_t, sin_t)


def _outproj_kernel(x_ref, ys5_ref, yatt_ref, yrw_ref, gate_ref, gpost_ref, w_ref, o_ref):
    o = (_dot(ys5_ref[...].astype(BF16), w_ref[:D_S5, :])
         + _dot(yatt_ref[...].astype(BF16), w_ref[D_S5:D_S5 + D_ATT, :])
         + _dot(yrw_ref[...].astype(BF16), w_ref[D_S5 + D_ATT:, :]))
    o_ref[...] = x_ref[...] + gate_ref[...] * _rms(o, gpost_ref[...])


def outproj_block(x, y_s5, y_att, y_rw, gate, g_post, w, tm, tiles_per_group):
    n, d = x.shape
    row = lambda width: pl.BlockSpec((tm, width), lambda i: (i, 0))
    const = lambda a: pl.BlockSpec(a.shape, lambda i: (0,) * a.ndim)
    return pl.pallas_call(
        _outproj_kernel,
        grid=(n // tm,),
        in_specs=[row(d), row(D_S5), row(D_ATT), row(D_RW), _mod_spec(gate, tiles_per_group),
                  const(g_post), const(w)],
        out_specs=row(d),
        out_shape=jax.ShapeDtypeStruct((n, d), F32),
        compiler_params=_cparams(("arbitrary",)),
        name="outproj_block",
    )(x, y_s5, y_att, y_rw, gate, g_post, w)


def _gelu_tanh(x):
    return 0.5 * x * (1.0 + jnp.tanh(math.sqrt(2.0 / math.pi) * (x + 0.044715 * (x * x * x))))


def _s5_disc_kernel(are_ref, aim_ref, ldt_ref, arec_ref, aimc_ref, ldtc_ref, bre_ref, bim_ref,
                    powre_ref, powim_ref, bbre_ref, bbim_ref, *, rows):
    def zoh(ar, ai, ldt):
        dt = jnp.exp(ldt)
        mag = jnp.exp(ar * dt)
        abr, abi = mag * jnp.cos(ai * dt), mag * jnp.sin(ai * dt)
        den = ar * ar + ai * ai
        nr, ni = abr - 1.0, abi
        return abr, abi, (nr * ar + ni * ai) / den, (ni * ar - nr * ai) / den

    abr, abi, _, _ = zoh(are_ref[...], aim_ref[...], ldt_ref[...])
    pr = jnp.broadcast_to(abr, (rows, abr.shape[1]))
    pi = jnp.broadcast_to(abi, (rows, abr.shape[1]))
    row = lax.broadcasted_iota(I32, pr.shape, 0)
    d = 1
    while d < rows:
        sr = pltpu.roll(pr, d, 0)
        si = pltpu.roll(pi, d, 0)
        m = row >= d
        pr, pi = jnp.where(m, pr * sr - pi * si, pr), jnp.where(m, pr * si + pi * sr, pi)
        d *= 2
    powre_ref[...] = pr
    powim_ref[...] = pi
    _, _, cr, ci = zoh(arec_ref[...], aimc_ref[...], ldtc_ref[...])
    br, bi = bre_ref[...], bim_ref[...]
    bbre_ref[...] = cr * br - ci * bi
    bbim_ref[...] = cr * bi + ci * br


def s5_discretise(a_re, a_im, log_dt, b_re, b_im, rows):
    g, p = a_re.shape
    w = g * p
    ldt = jnp.broadcast_to(log_dt[:, None], (g, p))
    args = (a_re.reshape(1, w), a_im.reshape(1, w), ldt.reshape(1, w),
            a_re.reshape(w, 1), a_im.reshape(w, 1), ldt.reshape(w, 1),
            b_re.reshape(w, S5_GROUP), b_im.reshape(w, S5_GROUP))
    return pl.pallas_call(
        functools.partial(_s5_disc_kernel, rows=rows),
        out_shape=[jax.ShapeDtypeStruct((rows, w), F32), jax.ShapeDtypeStruct((rows, w), F32),
                   jax.ShapeDtypeStruct((w, S5_GROUP), F32), jax.ShapeDtypeStruct((w, S5_GROUP), F32)],
        name="s5_discretise",
    )(*args)


def _s5_head(y, u, d_ref, gluw_ref, glub_ref):
    y = _gelu_tanh(y + d_ref[...] * u)
    return y * _sigmoid(_dot(y.astype(BF16), gluw_ref[...]) + glub_ref[...])


def _s5_kernel(u_ref, h0_ref, powre_ref, powim_ref, wb_ref, wc_ref, d_ref, gluw_ref, glub_ref,
               y_ref, ht_ref, cre, cim):
    t = pl.program_id(1)
    rows = u_ref.shape[0]

    @pl.when(t == 0)
    def _():
        cre[...] = h0_ref[:, :S5_W]
        cim[...] = h0_ref[:, S5_W:]

    u = u_ref[...]
    bu = _dot(u.astype(BF16), wb_ref[...])
    hr, hi = bu[:, :S5_W], bu[:, S5_W:]
    row = lax.broadcasted_iota(I32, hr.shape, 0)
    d = 1
    while d < rows:
        ar, ai = powre_ref[d - 1:d, :], powim_ref[d - 1:d, :]
        sr, si = pltpu.roll(hr, d, 0), pltpu.roll(hi, d, 0)
        m = row >= d
        hr, hi = (hr + jnp.where(m, ar * sr - ai * si, 0.0), hi + jnp.where(m, ar * si + ai * sr, 0.0))
        d *= 2
    pr, pi = powre_ref[...], powim_ref[...]
    c_r, c_i = cre[...], cim[...]
    hr, hi = hr + (pr * c_r - pi * c_i), hi + (pr * c_i + pi * c_r)
    cre[...] = hr[rows - 1:rows, :]
    cim[...] = hi[rows - 1:rows, :]
    ht_ref[:, :S5_W] = hr[rows - 1:rows, :]
    ht_ref[:, S5_W:] = hi[rows - 1:rows, :]
    y = _dot(hr.astype(BF16), wc_ref[:S5_W, :]) + _dot(hi.astype(BF16), wc_ref[S5_W:, :])
    y_ref[...] = _s5_head(y, u, d_ref, gluw_ref, glub_ref)


def s5_prompt(u, h0, powre, powim, wb, wc, d, gluw, glub):
    b, s, _ = u.shape
    rows = powre.shape[0]
    const = lambda a: pl.BlockSpec(a.shape, lambda i, j: (0,) * a.ndim)
    return pl.pallas_call(
        _s5_kernel,
        grid=(b, s // rows),
        in_specs=[pl.BlockSpec((None, rows, D_S5), lambda i, j: (i, j, 0)),
                  pl.BlockSpec((None, 1, 2 * S5_W), lambda i, j: (i, 0, 0)),
                  const(powre), const(powim), const(wb), const(wc), const(d), const(gluw), const(glub)],
        out_specs=[pl.BlockSpec((None, rows, D_S5), lambda i, j: (i, j, 0)),
                   pl.BlockSpec((None, 1, 2 * S5_W), lambda i, j: (i, 0, 0))],
        out_shape=[jax.ShapeDtypeStruct((b, s, D_S5), F32), jax.ShapeDtypeStruct((b, 1, 2 * S5_W), F32)],
        scratch_shapes=[pltpu.VMEM((1, S5_W), F32), pltpu.VMEM((1, S5_W), F32)],
        compiler_params=_cparams(("arbitrary", "arbitrary")),
        name="s5_prompt",
    )(u, h0, powre, powim, wb, wc, d, gluw, glub)


def _s5_step_kernel(u_ref, h0_ref, powre_ref, powim_ref, wb_ref, wc_ref, d_ref, gluw_ref, glub_ref,
                    y_ref, ht_ref):
    ar, ai = powre_ref[0:1, :], powim_ref[0:1, :]
    hr, hi = h0_ref[:, :S5_W], h0_ref[:, S5_W:]
    for t in range(u_ref.shape[0]):
        u = u_ref[t]
        bu = _dot(u.astype(BF16), wb_ref[...])
        hr, hi = ar * hr - ai * hi + bu[:, :S5_W], ar * hi + ai * hr + bu[:, S5_W:]
        y = _dot(hr.astype(BF16), wc_ref[:S5_W, :]) + _dot(hi.astype(BF16), wc_ref[S5_W:, :])
        y_ref[t] = _s5_head(y, u, d_ref, gluw_ref, glub_ref)
    ht_ref[:, :S5_W] = hr
    ht_ref[:, S5_W:] = hi


def s5_sample(u_tm, h0, powre, powim, wb, wc, d, gluw, glub):
    s, b, _ = u_tm.shape
    return pl.pallas_call(
        _s5_step_kernel,
        out_shape=[jax.ShapeDtypeStruct((s, b, D_S5), F32), jax.ShapeDtypeStruct((b, 2 * S5_W), F32)],
        compiler_params=pltpu.CompilerParams(vmem_limit_bytes=VMEM_LIMIT),
        name="s5_sample",
    )(u_tm, h0, powre, powim, wb, wc, d, gluw, glub)


def s5_matrices(bb_re, bb_im, c_re, c_im):
    g, p, h = S5_GROUPS, S5_STATE, S5_GROUP
    eye = jnp.eye(g, dtype=F32)
    bd_in = lambda bb: jnp.einsum('gph,gk->ghkp', bb.reshape(g, p, h), eye).reshape(g * h, g * p)
    bd_out = lambda c: jnp.einsum('ghp,gk->gpkh', c, eye).reshape(g * p, g * h)
    wb = jnp.concatenate([bd_in(bb_re), bd_in(bb_im)], axis=1).astype(BF16)
    wc = jnp.concatenate([bd_out(c_re), -bd_out(c_im)], axis=0).astype(BF16)
    return wb, wc


RW_PACK = 8 * D_RW


def _head_ones(n):
    r = lax.broadcasted_iota(I32, (n, n), 0) // HEAD_DIM
    c = lax.broadcasted_iota(I32, (n, n), 1) // HEAD_DIM
    return jnp.where(r == c, 1.0, 0.0).astype(BF16)


def _seg_sum(x, ones_bd):
    hi, lo = _split(x)
    return _dot(hi, ones_bd) + _dot(lo, ones_bd)


def _softplus(x):
    return jnp.maximum(x, 0.0) + jnp.log(1.0 + jnp.exp(-jnp.abs(x)))


def _rw_pre_kernel(cols_ref, shift0_ref, mu_ref, w0_ref, a0_ref, wlr_ref, kk_ref, ka_ref, rk_ref,
                   o_ref, carry, *, chunk, valid):
    @pl.when(pl.program_id(1) == 0)
    def _():
        carry[...] = shift0_ref[...]

    cf = cols_ref[...]
    tm = cf.shape[0]
    row = lax.broadcasted_iota(I32, (tm, 1), 0)
    prev = jnp.where(row == 0, carry[...], pltpu.roll(cf, 1, 0))
    carry[...] = cf[tm - 1:tm, :]
    xs = cf + (prev - cf) * mu_ref[...]
    r, k, v = xs[:, :D_RW], xs[:, D_RW:2 * D_RW], xs[:, 2 * D_RW:3 * D_RW]
    lr = xs[:, 3 * D_RW:]
    lane = lax.broadcasted_iota(I32, lr.shape, 1)
    t = jnp.where(lane < RW_W_RANK, jnp.tanh(lr), jnp.where(lane < RW_W_RANK + RW_A_RANK, lr, _sigmoid(lr)))
    proj = _dot3(t, wlr_ref[...])
    w = -_softplus(-(w0_ref[...] + proj[:, :D_RW])) - 0.5
    logw = -jnp.exp(w)
    a = _sigmoid(a0_ref[...] + proj[:, D_RW:2 * D_RW])
    g = proj[:, 2 * D_RW:]
    ones_bd = _head_ones(D_RW)
    kk = k * kk_ref[...]
    kk = kk / jnp.maximum(jnp.sqrt(_seg_sum(kk * kk, ones_bd)), 1e-12)
    km = k * (1.0 + (a - 1.0) * ka_ref[...])
    bonus = _seg_sum(r * km * rk_ref[...], ones_bd) * v
    pos = row % chunk
    if valid < chunk:
        live = pos < valid
        zero = jnp.zeros_like(r)
        logw, kk, km, v, r = (jnp.where(live, logw, zero), jnp.where(live, kk, zero), jnp.where(live, km, zero),
                              jnp.where(live, v, zero), jnp.where(live, r, zero))
    gc = logw
    d = 1
    while d < chunk:
        gc = gc + jnp.where(pos >= d, pltpu.roll(gc, d, 0), 0.0)
        d *= 2
    eg = jnp.exp(gc)
    eng = jnp.exp(-gc)
    o_ref[:, 0 * D_RW:1 * D_RW] = r * eg
    o_ref[:, 1 * D_RW:2 * D_RW] = kk * jnp.exp(gc - logw)
    o_ref[:, 2 * D_RW:3 * D_RW] = kk * a * eng
    o_ref[:, 3 * D_RW:4 * D_RW] = km * eng
    o_ref[:, 4 * D_RW:5 * D_RW] = v
    o_ref[:, 5 * D_RW:6 * D_RW] = eg
    o_ref[:, 6 * D_RW:7 * D_RW] = bonus
    o_ref[:, 7 * D_RW:8 * D_RW] = g


def rwkv_prepare(cols, shift0, mu, w0, a0, wlr, k_k, k_a, r_k, tm, chunk, valid):
    b, s, c = cols.shape
    const = lambda a: pl.BlockSpec(a.shape, lambda i, j: (0,) * a.ndim)
    return pl.pallas_call(
        functools.partial(_rw_pre_kernel, chunk=chunk, valid=valid),
        grid=(b, s // tm),
        in_specs=[pl.BlockSpec((None, tm, c), lambda i, j: (i, j, 0)),
                  pl.BlockSpec((None, 1, c), lambda i, j: (i, 0, 0)),
                  const(mu), const(w0), const(a0), const(wlr), const(k_k), const(k_a), const(r_k)],
        out_specs=pl.BlockSpec((None, tm, RW_PACK), lambda i, j: (i, j, 0)),
        out_shape=jax.ShapeDtypeStruct((b, s, RW_PACK), F32),
        scratch_shapes=[pltpu.VMEM((1, c), F32)],
        compiler_params=_cparams(("arbitrary", "arbitrary")),
        name="rwkv_prepare",
    )(cols, shift0, mu, w0, a0, wlr, k_k, k_a, r_k)


def _rw_chunk_kernel(x_ref, s0_ref, lnw_ref, lnb_ref, y_ref, st_ref, s_scr, y_scr, *, chunk):
    @pl.when(pl.program_id(1) == 0)
    def _():
        s_scr[...] = s0_ref[...]

    rows = x_ref.shape[0]
    hd = HEAD_DIM
    ri = lax.broadcasted_iota(I32, (chunk, chunk), 0)
    ci = lax.broadcasted_iota(I32, (chunk, chunk), 1)
    strict = ri > ci
    incl = ri >= ci
    eye = jnp.where(ri == ci, 1.0, 0.0)
    ek = lax.broadcasted_iota(I32, (hd, hd), 0) == lax.broadcasted_iota(I32, (hd, hd), 1)
    eye_k = jnp.where(ek, 1.0, 0.0)
    n_sq = chunk.bit_length() - 2
    n_chunks = rows // chunk
    probs = [(c, h) for c in range(n_chunks) for h in range(N_RW_HEADS)]
    each = lambda f: [f(i) for i in range(len(probs))]

    def col(i, j):
        c, h = probs[i]
        return x_ref[c * chunk:(c + 1) * chunk, j * D_RW + h * hd:j * D_RW + (h + 1) * hd]

    rt, kt, bt, km, v = (each(lambda i: col(i, j)) for j in range(5))
    lhs = each(lambda i: _split(jnp.concatenate([kt[i], rt[i]], axis=0)))
    bts, kms, vs = each(lambda i: _split(bt[i])), each(lambda i: _split(km[i])), each(lambda i: _split(v[i]))
    gb = each(lambda i: _mm3_nt(lhs[i], bts[i]))
    gk = each(lambda i: _mm3_nt(lhs[i], kms[i]))
    a_bb = each(lambda i: jnp.where(strict, gb[i][:chunk], 0.0))
    a_rb = each(lambda i: _split(jnp.where(incl, gb[i][chunk:], 0.0)))
    a_kr = each(lambda i: _split(jnp.concatenate([jnp.where(strict, gk[i][:chunk], 0.0),
                                                  jnp.where(incl, gk[i][chunk:], 0.0)], axis=0)))
    av = each(lambda i: _mm3(a_kr[i], vs[i]))
    minv = each(lambda i: eye - a_bb[i])
    p = a_bb
    for _ in range(n_sq):
        ps = each(lambda i: _split(p[i]))
        p = each(lambda i: _mm3(ps[i], ps[i]))
        minv = each(lambda i: minv[i] + _mm3(_split(minv[i]), _split(p[i])))
    minvs = each(lambda i: _split(minv[i]))
    khat = each(lambda i: _mm3(minvs[i], _split(kt[i])))
    p1 = each(lambda i: _mm3(minvs[i], _split(av[i][:chunk])))
    rhat = each(lambda i: rt[i] - _mm3(a_rb[i], _split(khat[i])))
    y1 = each(lambda i: av[i][chunk:] - _mm3(a_rb[i], _split(p1[i])))
    tb = each(lambda i: _mm3(_split(jnp.concatenate([khat[i].T, p1[i].T], axis=0)), bts[i]))
    vk = each(lambda i: _mm3(_split(v[i].T), kms[i]))

    def eg_last(i):
        c, h = probs[i]
        return x_ref[(c + 1) * chunk - 1:(c + 1) * chunk, 5 * D_RW + h * hd:5 * D_RW + (h + 1) * hd]

    gmat = each(lambda i: (eye_k - tb[i][:hd]) * eg_last(i))
    umat = each(lambda i: (vk[i] - tb[i][hd:]) * eg_last(i))
    state = [s_scr[h] for h in range(N_RW_HEADS)]
    for i, (c, h) in enumerate(probs):
        y_scr[c * chunk:(c + 1) * chunk, h * hd:(h + 1) * hd] = _dot3_nt(rhat[i], state[h]) + y1[i]
        state[h] = _dot3(state[h], gmat[i]) + umat[i]
    for h in range(N_RW_HEADS):
        s_scr[h] = state[h]
    st_ref[...] = s_scr[...]
    y = y_scr[...]
    ones_bd = _head_ones(D_RW)
    mean = _seg_sum(y, ones_bd) * (1.0 / hd)
    yc = y - mean
    var = _seg_sum(yc * yc, ones_bd) * (1.0 / hd)
    yn = yc * lax.rsqrt(var + RW_GN_EPS) * lnw_ref[...] + lnb_ref[...]
    y_ref[...] = (yn + x_ref[:, 6 * D_RW:7 * D_RW]) * x_ref[:, 7 * D_RW:8 * D_RW]


def rwkv_chunked(packed, state0, ln_w, ln_b, rows, chunk):
    b, s, _ = packed.shape
    const = lambda a: pl.BlockSpec(a.shape, lambda i, j: (0,) * a.ndim)
    st_spec = pl.BlockSpec((None, N_RW_HEADS, HEAD_DIM, HEAD_DIM), lambda i, j: (i, 0, 0, 0))
    return pl.pallas_call(
        functools.partial(_rw_chunk_kernel, chunk=chunk),
        grid=(b, s // rows),
        in_specs=[pl.BlockSpec((None, rows, RW_PACK), lambda i, j: (i, j, 0)), st_spec, const(ln_w), const(ln_b)],
        out_specs=[pl.BlockSpec((None, rows, D_RW), lambda i, j: (i, j, 0)), st_spec],
        out_shape=[jax.ShapeDtypeStruct((b, s, D_RW), F32),
                   jax.ShapeDtypeStruct((b, N_RW_HEADS, HEAD_DIM, HEAD_DIM), F32)],
        scratch_shapes=[pltpu.VMEM((N_RW_HEADS, HEAD_DIM, HEAD_DIM), F32), pltpu.VMEM((rows, D_RW), F32)],
        compiler_params=_cparams(("arbitrary", "arbitrary")),
        name="rwkv_chunked",
    )(packed, state0, ln_w, ln_b)


def rwkv_lowrank_matrix(w2, a2, g2):
    z = lambda r: jnp.zeros((r, D_RW), F32)
    return jnp.concatenate([
        jnp.concatenate([w2, z(RW_W_RANK), z(RW_W_RANK)], axis=1),
        jnp.concatenate([z(RW_A_RANK), a2, z(RW_A_RANK)], axis=1),
        jnp.concatenate([z(RW_G_RANK), z(RW_G_RANK), g2], axis=1)], axis=0)


def _float_key(x):
    b = lax.bitcast_convert_type(x, I32)
    return jnp.where(b < 0, jnp.int32(INT_MIN) - b, b)


def _py_key(v):
    import numpy as np
    b = int(np.float32(v).view(np.int32))
    return -(b & 0x7FFFFFFF) if b < 0 else b


KEY_NEG_INF = _py_key(NEG_INF)


def _count_ge(sc_ref, nc, cand):
    parts = [jnp.where(sc_ref[c] >= cand, 1.0, 0.0) for c in range(nc)]
    while len(parts) > 1:
        parts = [parts[i] + parts[i + 1] for i in range(0, len(parts) - 1, 2)] + (parts[-1:] if len(parts) % 2 else [])
    return jnp.sum(parts[0], axis=1, keepdims=True)


def _select_threshold(sc_ref, nc, k):
    _, r, wc = sc_ref.shape

    def bit_body(i, carry):
        t, cge = carry
        cand = t + jnp.left_shift(jnp.int32(1), 31 - i)
        cnt = _count_ge(sc_ref, nc, cand)
        ok = cnt >= k
        return jnp.where(ok, cand, t), jnp.where(ok, cnt, cge)

    t0 = jnp.full((r, 1), INT_MIN, I32)
    cge0 = jnp.zeros((r, 1), F32) + jnp.asarray(nc * wc, F32)
    return lax.fori_loop(0, 32, bit_body, (t0, cge0))


def _resolve_ties(sc_ref, nc, t, need):
    _, r, wc = sc_ref.shape
    upper = jnp.where(lax.broadcasted_iota(I32, (wc, wc), 0) < lax.broadcasted_iota(I32, (wc, wc), 1), 1.0, 0.0)

    def body(c, offs):
        x = sc_ref[c]
        e = x == t
        ef = jnp.where(e, 1.0, 0.0)
        rank = _dot(ef, upper) + offs
        sc_ref[c] = jnp.where(e, jnp.where(rank >= need, jnp.int32(INT_MIN), x), x)
        return offs + jnp.sum(ef, axis=1, keepdims=True)

    lax.fori_loop(0, nc, body, jnp.zeros((r, 1), F32))


def _topk_mask_prepare(sc_ref, nc, k, valid_rows):
    t, cge = _select_threshold(sc_ref, nc, k)
    r = t.shape[0]
    live = lax.broadcasted_iota(I32, (r, 1), 0) < valid_rows
    tied = live & (cge > k) & (t != KEY_NEG_INF)

    @pl.when(jnp.max(jnp.where(tied, 1.0, 0.0)) > 0.0)
    def _():
        cgt = _count_ge(sc_ref, nc, t + 1)
        _resolve_ties(sc_ref, nc, t, k - cgt)

    return t


def _indexer_total(s4, wcols, rows):
    tot = jnp.maximum(s4[:rows], 0.0) * wcols[0]
    for h in range(1, N_IDX_HEADS):
        tot = tot + jnp.maximum(s4[h * rows:(h + 1) * rows], 0.0) * wcols[h]
    return tot


SUBLANES = 8


def _col_tree(x, op):
    parts = [x[i:i + SUBLANES] for i in range(0, x.shape[0], SUBLANES)]
    while len(parts) > 1:
        parts = [op(parts[i], parts[i + 1]) for i in range(0, len(parts) - 1, 2)] + (parts[-1:] if len(parts) % 2 else [])
    return parts[0]


def _col_sum(x):
    return _col_tree(x, jnp.add)


def _col_max(x):
    return _col_tree(x, jnp.maximum)


def _count_ge_t(sc_ref, nc, cand):
    _, wc, nq = sc_ref.shape

    def body(c, acc):
        return acc + _col_sum(jnp.where(sc_ref[c] >= cand, 1.0, 0.0))

    acc = lax.fori_loop(0, nc, body, jnp.zeros((SUBLANES, nq), F32))
    return jnp.sum(acc, axis=0, keepdims=True)


def _topk_mask_prepare_t(sc_ref, nc, k):
    _, wc, nq = sc_ref.shape

    def bit_body(i, carry):
        t, cge = carry
        cand = t + jnp.left_shift(jnp.int32(1), 31 - i)
        cnt = _count_ge_t(sc_ref, nc, cand)
        ok = cnt >= k
        return jnp.where(ok, cand, t), jnp.where(ok, cnt, cge)

    t0 = jnp.full((1, nq), INT_MIN, I32)
    cge0 = jnp.zeros((1, nq), F32) + jnp.asarray(nc * wc, F32)
    t, cge = lax.fori_loop(0, 32, bit_body, (t0, cge0))
    tied = (cge > k) & (t != KEY_NEG_INF)

    @pl.when(jnp.max(jnp.where(tied, 1.0, 0.0)) > 0.0)
    def _():
        need = k - _count_ge_t(sc_ref, nc, t + 1)
        ri = lax.broadcasted_iota(I32, (wc, wc), 0)
        ci = lax.broadcasted_iota(I32, (wc, wc), 1)
        lower = jnp.where(ci < ri, 1.0, 0.0)

        def body(c, offs):
            x = sc_ref[c]
            e = x == t
            ef = jnp.where(e, 1.0, 0.0)
            rank = _dot(lower, ef) + offs
            sc_ref[c] = jnp.where(e, jnp.where(rank >= need, jnp.int32(INT_MIN), x), x)
            return offs + jnp.sum(ef, axis=0, keepdims=True)

        lax.fori_loop(0, nc, body, jnp.zeros((1, nq), F32))

    return t


def _dsa_prompt_kernel(q_ref, kb_ref, vt_ref, qi3_ref, ki3_ref, kiw_ref, o_ref, sc, *, k_sel):
    j = pl.program_id(1)
    qb, wc = Q_BLOCK, KEY_CHUNK
    nck = (j * qb + qb + wc - 1) // wc
    qpos = j * qb + lax.broadcasted_iota(I32, (1, qb), 1)
    keybase = lax.broadcasted_iota(I32, (wc, 1), 0)

    lhs = jnp.concatenate([qi3_ref[:, 2 * LANES * h:2 * LANES * (h + 1)] for h in range(N_IDX_HEADS)], axis=0)
    kiw_t = kiw_ref[...].T
    wrows = [kiw_t[IDX_DIM + h:IDX_DIM + h + 1, :] for h in range(N_IDX_HEADS)]

    def score_chunk(c, diagonal):
        kc = ki3_ref[pl.ds(pl.multiple_of(c * wc, wc), wc), :]
        s4 = _dot_nt(kc, lhs)
        tot = jnp.maximum(s4[:, :qb], 0.0) * wrows[0]
        for h in range(1, N_IDX_HEADS):
            tot = tot + jnp.maximum(s4[:, h * qb:(h + 1) * qb], 0.0) * wrows[h]
        if diagonal:
            tot = jnp.where(c * wc + keybase <= qpos, tot, NEG_INF)
        sc[c] = _float_key(tot)

    def score_body(c, carry):
        score_chunk(c, False)
        return carry

    lax.fori_loop(0, nck - 1, score_body, 0)
    score_chunk(nck - 1, True)
    t = _topk_mask_prepare_t(sc, nck, k_sel)

    lane = lax.broadcasted_iota(I32, (qb, LANES), 1)
    n_pairs = N_ATT_HEADS // 2
    qpair = []
    for pr in range(n_pairs):
        blk = q_ref[:, pr * LANES:(pr + 1) * LANES]
        zero = jnp.zeros_like(blk)
        qpair.append(jnp.concatenate([jnp.where(lane < HEAD_DIM, blk, zero), jnp.where(lane < HEAD_DIM, zero, blk)],
                                     axis=0))

    def att_body(c, carry):
        ms, ls, accs = carry
        off = pl.multiple_of(c * wc, wc)
        keep = (sc[c] >= t) & (c * wc + keybase <= qpos)
        bias = jnp.where(keep, 0.0, NEG_INF)
        bias2 = jnp.concatenate([bias, bias], axis=1)
        ss = [_dot_nt(kb_ref[pl.ds(off, wc), pr * LANES:(pr + 1) * LANES], qpair[pr]) + bias2
              for pr in range(n_pairs)]
        new_m = [jnp.maximum(ms[pr], jnp.max(_col_max(ss[pr]), axis=0, keepdims=True)) for pr in range(n_pairs)]
        alphas = [jnp.exp2(ms[pr] - new_m[pr]) for pr in range(n_pairs)]
        ps = [jnp.exp2(ss[pr] - new_m[pr]) for pr in range(n_pairs)]
        new_l = [alphas[pr] * ls[pr] + jnp.sum(_col_sum(ps[pr]), axis=0, keepdims=True) for pr in range(n_pairs)]
        new_acc = [alphas[pr] * accs[pr]
                   + _dot(vt_ref[pr * LANES:(pr + 1) * LANES, pl.ds(off, wc)], ps[pr].astype(BF16))
                   for pr in range(n_pairs)]
        return tuple(new_m), tuple(new_l), tuple(new_acc)

    init = (tuple(jnp.full((1, 2 * qb), NEG_INF, F32) for _ in range(n_pairs)),
            tuple(jnp.zeros((1, 2 * qb), F32) for _ in range(n_pairs)),
            tuple(jnp.zeros((LANES, 2 * qb), F32) for _ in range(n_pairs)))
    _, ls, accs = lax.fori_loop(0, nck, att_body, init)
    sub = lax.broadcasted_iota(I32, (LANES, qb), 0)
    for pr in range(n_pairs):
        o2 = accs[pr] / ls[pr]
        o_ref[:, pr * LANES:(pr + 1) * LANES] = jnp.where(sub < HEAD_DIM, o2[:, :qb], o2[:, qb:]).T


def dsa_prompt(q, kb, vt, qi3, ki3, kiw):
    b, s, _ = q.shape
    k_sel = min(TOPK_MAX, s // 4)
    blk = lambda w: pl.BlockSpec((None, Q_BLOCK, w), lambda i, j: (i, j, 0))
    full = lambda w: pl.BlockSpec((None, s, w), lambda i, j: (i, 0, 0))
    return pl.pallas_call(
        functools.partial(_dsa_prompt_kernel, k_sel=k_sel),
        grid=(b, s // Q_BLOCK),
        in_specs=[blk(D_ATT), full(D_ATT), pl.BlockSpec((None, D_ATT, s), lambda i, j: (i, 0, 0)),
                  blk(N_IDX_HEADS * 2 * LANES), full(2 * LANES), blk(LANES)],
        out_specs=blk(D_ATT),
        out_shape=jax.ShapeDtypeStruct((b, s, D_ATT), F32),
        scratch_shapes=[pltpu.VMEM((s // KEY_CHUNK, KEY_CHUNK, Q_BLOCK), I32)],
        compiler_params=_cparams(("arbitrary", "arbitrary")),
        name="dsa_prompt",
    )(q, kb, vt, qi3, ki3, kiw)


SAMPLE_ROWS = 8
PAGE_GROUP = 8


def _dsa_sample_index_kernel(pt_ref, *refs, n_valid, k_sel):
    pages = refs[:PAGE_GROUP]
    qi3_ref, kiw_ref, ki3n_ref, sc_ref, thr_ref = refs[PAGE_GROUP:]
    g = pl.program_id(1)
    r = SAMPLE_ROWS
    n_chunks = sc_ref.shape[0]
    q3 = qi3_ref[...].astype(F32)
    qhi = jnp.concatenate([q3[:, 2 * LANES * h:2 * LANES * h + IDX_DIM] for h in range(N_IDX_HEADS)],
                          axis=0).astype(BF16)
    qlo = jnp.concatenate([q3[:, 2 * LANES * h + LANES:2 * LANES * h + LANES + IDX_DIM]
                           for h in range(N_IDX_HEADS)], axis=0).astype(BF16)
    wcols = [kiw_ref[:, IDX_DIM + h:IDX_DIM + h + 1] for h in range(N_IDX_HEADS)]

    def scores(kt):
        khi, klo = _split(kt)
        return _dot(qhi, khi) + (_dot(qhi, klo) + _dot(qlo, khi))

    for i in range(PAGE_GROUP):
        sc_ref[g * PAGE_GROUP + i] = _float_key(_indexer_total(scores(pages[i][...]), wcols, r))

    @pl.when(g == pl.num_programs(1) - 1)
    def _():
        tot = _indexer_total(scores(ki3n_ref[...]), wcols, r)
        row = lax.broadcasted_iota(I32, tot.shape, 0)
        col = lax.broadcasted_iota(I32, tot.shape, 1)
        tot = jnp.where((col <= row) & (col < n_valid), tot, NEG_INF)
        sc_ref[n_chunks - 1] = _float_key(tot)
        t = _topk_mask_prepare(sc_ref, n_chunks, k_sel, n_valid)
        thr_ref[...] = jnp.broadcast_to(t, thr_ref.shape)


def dsa_sample_index(page_table, cache_kidx_t, layer, qi3, kiw, ki_new_t, n_valid):
    b, n_pages = page_table.shape
    n_chunks = n_pages + 1
    k_sel = min(TOPK_MAX, (n_pages * PAGE_SIZE + n_valid) // 4)
    page_spec = lambda i: pl.BlockSpec((None, None, IDX_DIM, PAGE_SIZE),
                                       lambda bi, g, pt: (layer, pt[bi, g * PAGE_GROUP + i], 0, 0))
    per_b = lambda shape: pl.BlockSpec((None,) + shape, lambda bi, g, pt: (bi,) + (0,) * len(shape))
    r = SAMPLE_ROWS
    return pl.pallas_call(
        functools.partial(_dsa_sample_index_kernel, n_valid=n_valid, k_sel=k_sel),
        grid_spec=pltpu.PrefetchScalarGridSpec(
            num_scalar_prefetch=1,
            grid=(b, n_pages // PAGE_GROUP),
            in_specs=[page_spec(i) for i in range(PAGE_GROUP)]
            + [per_b((r, N_IDX_HEADS * 2 * LANES)), per_b((r, LANES)), per_b((IDX_DIM, PAGE_SIZE))],
            out_specs=[per_b((n_chunks, r, PAGE_SIZE)), per_b((r, PAGE_SIZE))],
        ),
        out_shape=[jax.ShapeDtypeStruct((b, n_chunks, r, PAGE_SIZE), I32),
                   jax.ShapeDtypeStruct((b, r, PAGE_SIZE), I32)],
        compiler_params=_cparams(("arbitrary", "arbitrary")),
        name="dsa_sample_index",
    )(page_table, *([cache_kidx_t] * PAGE_GROUP), qi3, kiw, ki_new_t)


def _dsa_sample_attend_kernel(pt_ref, *refs, n_valid):
    kpages = refs[:PAGE_GROUP]
    vpages = refs[PAGE_GROUP:2 * PAGE_GROUP]
    qbd_ref, sel_ref, seln_ref, thr_ref, kn_ref, vn_ref, o_ref, m_scr, l_scr, acc_scr = refs[2 * PAGE_GROUP:]
    g = pl.program_id(1)
    r = SAMPLE_ROWS

    @pl.when(g == 0)
    def _():
        m_scr[...] = jnp.full(m_scr.shape, NEG_INF, F32)
        l_scr[...] = jnp.zeros(l_scr.shape, F32)
        acc_scr[...] = jnp.zeros(acc_scr.shape, F32)

    qbd = qbd_ref[...]
    thr = thr_ref[...]
    per_head = lambda x: jnp.concatenate([x] * N_ATT_HEADS, axis=0)
    thr_all = per_head(thr)

    def update(sels, extra, kts, vts):
        n = len(kts)
        scores = jnp.concatenate([_dot(qbd, kt) for kt in kts], axis=1)
        mask = jnp.concatenate([per_head(sel) for sel in sels], axis=1) >= jnp.concatenate([thr_all] * n, axis=1)
        if extra is not None:
            mask = mask & (per_head(extra) > 0)
        s = jnp.where(mask, scores, NEG_INF)
        m_prev = m_scr[...]
        m_new = jnp.maximum(m_prev, jnp.max(s, axis=1, keepdims=True))
        alpha = jnp.exp(m_prev - m_new)
        p = jnp.exp(s - m_new)
        l_scr[...] = alpha * l_scr[...] + jnp.sum(p, axis=1, keepdims=True)
        p = p.astype(BF16)
        pv = _dot_nt(p[:, :PAGE_SIZE], vts[0])
        for i in range(1, n):
            pv = pv + _dot_nt(p[:, i * PAGE_SIZE:(i + 1) * PAGE_SIZE], vts[i])
        acc_scr[...] = alpha * acc_scr[...] + pv
        m_scr[...] = m_new

    update([sel_ref[i] for i in range(PAGE_GROUP)], None,
           [kpages[i][...].astype(BF16) for i in range(PAGE_GROUP)],
           [vpages[i][...].astype(BF16) for i in range(PAGE_GROUP)])

    @pl.when(g == pl.num_programs(1) - 1)
    def _():
        row = lax.broadcasted_iota(I32, thr.shape, 0)
        col = lax.broadcasted_iota(I32, thr.shape, 1)
        causal = jnp.where((col <= row) & (col < n_valid), 1, 0)
        update([seln_ref[...]], causal, [kn_ref[...]], [vn_ref[...]])
        out = acc_scr[...] / l_scr[...]
        lane = lax.broadcasted_iota(I32, (r, D_ATT), 1) // HEAD_DIM
        tot = jnp.where(lane == 0, out[:r], 0.0)
        for h in range(1, N_ATT_HEADS):
            tot = tot + jnp.where(lane == h, out[h * r:(h + 1) * r], 0.0)
        o_ref[...] = tot


def dsa_sample_attend(page_table, cache_kt, cache_vt, layer, q_bd, sel, thr, k_new_t, v_new_t, n_valid):
    b, n_pages = page_table.shape
    r = SAMPLE_ROWS
    page_spec = lambda i: pl.BlockSpec((None, None, D_ATT, PAGE_SIZE),
                                       lambda bi, g, pt: (layer, pt[bi, g * PAGE_GROUP + i], 0, 0))
    per_b = lambda shape: pl.BlockSpec((None,) + shape, lambda bi, g, pt: (bi,) + (0,) * len(shape))
    rows = N_ATT_HEADS * r
    return pl.pallas_call(
        functools.partial(_dsa_sample_attend_kernel, n_valid=n_valid),
        grid_spec=pltpu.PrefetchScalarGridSpec(
            num_scalar_prefetch=1,
            grid=(b, n_pages // PAGE_GROUP),
            in_specs=[page_spec(i) for i in range(PAGE_GROUP)] + [page_spec(i) for i in range(PAGE_GROUP)]
            + [per_b((rows, D_ATT)),
               pl.BlockSpec((None, PAGE_GROUP, r, PAGE_SIZE), lambda bi, g, pt: (bi, g, 0, 0)),
               pl.BlockSpec((None, None, r, PAGE_SIZE), lambda bi, g, pt: (bi, n_pages, 0, 0)),
               per_b((r, PAGE_SIZE)), per_b((D_ATT, PAGE_SIZE)), per_b((D_ATT, PAGE_SIZE))],
            out_specs=per_b((r, D_ATT)),
            scratch_shapes=[pltpu.VMEM((rows, 1), F32), pltpu.VMEM((rows, 1), F32), pltpu.VMEM((rows, D_ATT), F32)],
        ),
        out_shape=jax.ShapeDtypeStruct((b, r, D_ATT), F32),
        compiler_params=_cparams(("arbitrary", "arbitrary")),
        name="dsa_sample_attend",
    )(page_table, *([cache_kt] * PAGE_GROUP), *([cache_vt] * PAGE_GROUP), q_bd, sel, sel, thr, k_new_t, v_new_t)


def pack_w_in(w_in):
    d = w_in.shape[0]
    pad = jnp.zeros((d, ZP_END - ZP_KIW - (OFF_RW - OFF_KI)), w_in.dtype)
    return jnp.concatenate([w_in[:, OFF_S5:OFF_KI], w_in[:, OFF_RW:], w_in[:, OFF_KI:OFF_RW], pad], axis=1).astype(BF16)


def rope_tables(pos):
    half = HEAD_DIM // 2
    inv = ROPE_THETA ** (-jnp.arange(half, dtype=F32) / half)
    ang = pos.astype(F32)[:, None] * inv[None, :]
    cos, sin = jnp.cos(ang), jnp.sin(ang)
    return jnp.tile(jnp.concatenate([cos, cos], 1), (1, 2)), jnp.tile(jnp.concatenate([-sin, sin], 1), (1, 2))


def block_diag_queries(q):
    b, r, d = q.shape
    head_of_lane = jnp.arange(d) // HEAD_DIM
    keep = head_of_lane[None, :] == jnp.arange(N_ATT_HEADS)[:, None]
    return jnp.where(keep[None, :, None, :], q[:, None], jnp.zeros((), q.dtype)).reshape(b, N_ATT_HEADS * r, d)


def _pad_rows(a, rows):
    return jnp.pad(a, ((0, 0), (0, rows - a.shape[1])) + ((0, 0),) * (a.ndim - 2))


Q_SCALE = HEAD_DIM ** -0.5
Q_SCALE_EXP2 = Q_SCALE * math.log2(math.e)
PROMPT_TILE = 256
RW_PRE_TILE = 512
RW_STEP_ROWS = 256


def kernel(x_prompt, x_sample, c_prompt, c_sample, cache_k, cache_v, cache_kidx, state_s5_re, state_s5_im, state_rwkv, state_rwkv_shift, page_table, ada_w, ada_b, norm_pre, norm_post, ffn_wi, ffn_wo, w_in, w_out, s5_a_re, s5_a_im, s5_log_dt, s5_b_re, s5_b_im, s5_c_re, s5_c_im, s5_d, s5_glu_w, s5_glu_b, rw_mu, rw_w0, rw_w2, rw_a0, rw_a2, rw_g2, rw_k_k, rw_k_a, rw_r_k, rw_ln_w, rw_ln_b):
    bp, sp, d = x_prompt.shape
    bs, ss, _ = x_sample.shape
    depth = ada_w.shape[0]
    past = page_table.shape[1] * PAGE_SIZE
    np_tok, ns_tok = bp * sp, bs * ss
    tpb = sp // PROMPT_TILE
    row1 = lambda a: a.reshape(1, -1)

    c_all = _pad_rows(jnp.concatenate([c_prompt, c_sample], axis=0)[None], -(-(bp + bs) // 8) * 8)[0]
    mod = ada_mod(c_all, ada_w, ada_b).reshape(depth, c_all.shape[0], N_SUB, 3, d)

    cos_p, sin_p = rope_tables(jnp.arange(sp))
    cos_s, sin_s = (jnp.tile(t, (bs, 1)) for t in rope_tables(past + jnp.arange(ss)))
    n_pool = cache_k.shape[1]
    cache_kt = jnp.transpose(cache_k, (0, 1, 3, 4, 2)).reshape(depth, n_pool, D_ATT, PAGE_SIZE)
    cache_vt = jnp.transpose(cache_v, (0, 1, 3, 4, 2)).reshape(depth, n_pool, D_ATT, PAGE_SIZE)
    cache_kidx_t = jnp.swapaxes(cache_kidx, 2, 3)

    xp = x_prompt.reshape(np_tok, d)
    xs = x_sample.reshape(ns_tok, d)
    outs_p, outs_s = [], []
    for l in range(depth):
        mod_p = mod[l, :bp]
        mod_s = jnp.repeat(mod[l, bp:bp + bs], ss, axis=0)
        pm = lambda i, j: mod_p[:, i, j][:, None, :]
        sm = lambda i, j: mod_s[:, i, j][None]
        npre = lambda i: row1(norm_pre[l, i])
        npost = lambda i: row1(norm_post[l, i])
        wi0, wo0 = ffn_wi[l, 0].astype(BF16), ffn_wo[l, 0].astype(BF16)
        wi1, wo1 = ffn_wi[l, 1].astype(BF16), ffn_wo[l, 1].astype(BF16)
        w_in_p = pack_w_in(w_in[l])
        w_out_b = w_out[l].astype(BF16)
        powre, powim, bb_re, bb_im = s5_discretise(s5_a_re[l], s5_a_im[l], s5_log_dt[l], s5_b_re[l], s5_b_im[l],
                                                   rows=S5_CHUNK)
        wb, wc = s5_matrices(bb_re, bb_im, s5_c_re[l], s5_c_im[l])
        s5_tail = (powre, powim, wb, wc, row1(s5_d[l]), s5_glu_w[l].astype(BF16), row1(s5_glu_b[l]))
        wlr = rwkv_lowrank_matrix(rw_w2[l], rw_a2[l], rw_g2[l])
        rw_pre = (row1(rw_mu[l]), row1(rw_w0[l]), row1(rw_a0[l]), wlr, row1(rw_k_k[l]), row1(rw_k_a[l]),
                  row1(rw_r_k[l]))
        ln = (row1(rw_ln_w[l]), row1(rw_ln_b[l]))

        xp = ffn_block(xp, pm(0, 0), pm(0, 1), pm(0, 2), npre(0), npost(0), wi0, wo0, PROMPT_TILE, tpb)
        u, q, k, kb, v, _, qi3, kiw, ki3, rw, vt = inproj_block(xp, pm(1, 0), pm(1, 1), npre(1), w_in_p, cos_p, sin_p,
                                                                PROMPT_TILE, tpb, tpb, Q_SCALE_EXP2)
        seq = lambda a: a.reshape(bp, sp, a.shape[-1])
        y_s5, h_s5 = s5_prompt(seq(u), jnp.zeros((bp, 1, 2 * S5_W), F32), *s5_tail)
        y_att = dsa_prompt(seq(q), seq(kb), vt, seq(qi3), seq(ki3), seq(kiw))
        cols = seq(rw)
        packed = rwkv_prepare(cols, jnp.zeros((bp, 1, N_RW_COLS), F32), *rw_pre, RW_PRE_TILE, RW_CHUNK, RW_CHUNK)
        y_rw, st_rw = rwkv_chunked(packed, jnp.zeros((bp, N_RW_HEADS, HEAD_DIM, HEAD_DIM), F32), *ln,
                                   RW_STEP_ROWS, RW_CHUNK)
        flat = lambda a: a.reshape(np_tok, a.shape[-1])
        xp = outproj_block(xp, flat(y_s5), flat(y_att), flat(y_rw), pm(1, 2), npost(1), w_out_b, PROMPT_TILE, tpb)
        xp = ffn_block(xp, pm(2, 0), pm(2, 1), pm(2, 2), npre(2), npost(2), wi1, wo1, PROMPT_TILE, tpb)
        outs_p.append((k.reshape(bp, sp, N_ATT_HEADS, HEAD_DIM), v.reshape(bp, sp, N_ATT_HEADS, HEAD_DIM),
                       seq(kiw)[:, :, :IDX_DIM],
                       h_s5[:, 0, :S5_W].reshape(bp, S5_GROUPS, S5_STATE),
                       h_s5[:, 0, S5_W:].reshape(bp, S5_GROUPS, S5_STATE), st_rw, cols[:, sp - 1]))

        xs = ffn_block(xs, sm(0, 0), sm(0, 1), sm(0, 2), npre(0), npost(0), wi0, wo0, ns_tok, 1)
        u, q, k, kb, v, vb, qi3, kiw, ki3, rw, _ = inproj_block(xs, sm(1, 0), sm(1, 1), npre(1), w_in_p, cos_s, sin_s,
                                                                ns_tok, 1, 1, Q_SCALE)
        seq = lambda a: a.reshape(bs, ss, a.shape[-1])
        h0 = jnp.concatenate([state_s5_re[l].reshape(bs, S5_W), state_s5_im[l].reshape(bs, S5_W)], axis=1)
        y_s5, h_s5 = s5_sample(jnp.swapaxes(seq(u), 0, 1), h0, *s5_tail)
        y_s5 = jnp.swapaxes(y_s5, 0, 1)
        tok_t = lambda a: jnp.swapaxes(_pad_rows(a, PAGE_SIZE), 1, 2)
        sel, thr = dsa_sample_index(page_table, cache_kidx_t, l, _pad_rows(seq(qi3), SAMPLE_ROWS),
                                    _pad_rows(seq(kiw), SAMPLE_ROWS), tok_t(seq(kiw)[:, :, :IDX_DIM]), ss)
        q_bd = block_diag_queries(_pad_rows(seq(q), SAMPLE_ROWS))
        y_att = dsa_sample_attend(page_table, cache_kt, cache_vt, l, q_bd, sel, thr, tok_t(seq(kb)), tok_t(seq(vb)),
                                  ss)[:, :ss]
        cols = seq(rw)
        packed = rwkv_prepare(_pad_rows(cols, RW_CHUNK), state_rwkv_shift[l].reshape(bs, 1, N_RW_COLS), *rw_pre,
                              RW_CHUNK, RW_CHUNK, ss)
        y_rw, st_rw = rwkv_chunked(packed, state_rwkv[l], *ln, RW_CHUNK, RW_CHUNK)
        flat = lambda a: a.reshape(ns_tok, a.shape[-1])
        xs = outproj_block(xs, flat(y_s5), flat(y_att), flat(y_rw[:, :ss]), sm(1, 2), npost(1), w_out_b, ns_tok, 1)
        xs = ffn_block(xs, sm(2, 0), sm(2, 1), sm(2, 2), npre(2), npost(2), wi1, wo1, ns_tok, 1)
        outs_s.append((k.reshape(bs, ss, N_ATT_HEADS, HEAD_DIM), v.reshape(bs, ss, N_ATT_HEADS, HEAD_DIM),
                       seq(kiw)[:, :, :IDX_DIM],
                       h_s5[:, :S5_W].reshape(bs, S5_GROUPS, S5_STATE),
                       h_s5[:, S5_W:].reshape(bs, S5_GROUPS, S5_STATE), st_rw, cols[:, ss - 1]))

    stack = lambda outs, i: jnp.stack([o[i] for o in outs])
    return ((xp.reshape(bp, sp, d), xs.reshape(bs, ss, d))
            + tuple(stack(outs_p, i) for i in range(7)) + tuple(stack(outs_s, i) for i in range(7)))
```

```python
import functools
import math

import jax
import jax.numpy as jnp
from jax import lax
from jax.experimental import pallas as pl
from jax.experimental.pallas import tpu as pltpu

F32 = jnp.float32
BF16 = jnp.bfloat16
I32 = jnp.int32

D_MODEL = 1024
PAGE_SIZE = 128
D_S5 = 256
D_ATT = 512
D_RW = 256
S5_GROUP = 16
S5_GROUPS = D_S5 // S5_GROUP
S5_STATE = 64
S5_W = S5_GROUPS * S5_STATE
HEAD_DIM = 64
N_ATT_HEADS = D_ATT // HEAD_DIM
N_IDX_HEADS = 4
IDX_DIM = 64
TOPK_MAX = 256
Q_BLOCK = 128
ROPE_THETA = 10000.0
N_RW_HEADS = D_RW // HEAD_DIM
RW_W_RANK = 32
RW_A_RANK = 32
RW_G_RANK = 64
RW_GN_EPS = 64e-5
N_RW_COLS = 3 * D_RW + RW_W_RANK + RW_A_RANK + RW_G_RANK
D_FF = 2816
HALF_STEP = 0.5
RMS_EPS = 1e-6
N_SUB = 3
NEG_INF = -1e30
OFF_S5 = 0
OFF_Q = OFF_S5 + D_S5
OFF_K = OFF_Q + D_ATT
OFF_V = OFF_K + D_ATT
OFF_QI = OFF_V + D_ATT
OFF_KI = OFF_QI + N_IDX_HEADS * IDX_DIM
OFF_WI = OFF_KI + IDX_DIM
OFF_RW = OFF_WI + N_IDX_HEADS
N_IN = OFF_RW + N_RW_COLS

LANES = 128
VMEM_LIMIT = 56 * 1024 * 1024
INT_MIN = -(2 ** 31)

ZP_U, ZP_Q, ZP_K, ZP_V, ZP_QI, ZP_RW, ZP_KIW, ZP_END = 0, 256, 768, 1280, 1792, 2048, 2944, 3072

RW_CHUNK = 64
S5_CHUNK = 128
KEY_CHUNK = 512


def _cparams(sem):
    return pltpu.CompilerParams(dimension_semantics=sem, vmem_limit_bytes=VMEM_LIMIT)


def _dot(a, b):
    return jnp.dot(a, b, preferred_element_type=F32)


def _dot_nt(a, b):
    return lax.dot_general(a, b, (((1,), (1,)), ((), ())), preferred_element_type=F32)


def _split(x):
    hi = x.astype(BF16)
    lo = (x - hi.astype(F32)).astype(BF16)
    return hi, lo


def _dot3(a, b):
    ah, al = _split(a)
    bh, bl = _split(b)
    return _dot(ah, bh) + (_dot(ah, bl) + _dot(al, bh))


def _dot3_nt(a, b):
    ah, al = _split(a)
    bh, bl = _split(b)
    return _dot_nt(ah, bh) + (_dot_nt(ah, bl) + _dot_nt(al, bh))


def _mm3(a, b):
    return _dot(a[0], b[0]) + (_dot(a[0], b[1]) + _dot(a[1], b[0]))


def _mm3_nt(a, b):
    return _dot_nt(a[0], b[0]) + (_dot_nt(a[0], b[1]) + _dot_nt(a[1], b[0]))


def _rms(x, g):
    return x * lax.rsqrt(jnp.mean(x * x, axis=-1, keepdims=True) + RMS_EPS) * g


def _sigmoid(x):
    return 1.0 / (1.0 + jnp.exp(-x))


def _ada_kernel(c_ref, w_ref, b_ref, o_ref):
    c = c_ref[...]
    h = (c * _sigmoid(c)).astype(BF16)
    o_ref[...] = _dot(h, w_ref[...].astype(BF16)) + b_ref[...]


def ada_mod(c_all, ada_w, ada_b, tn=1152):
    depth, d, n = ada_w.shape
    rows = c_all.shape[0]
    return pl.pallas_call(
        _ada_kernel,
        grid=(depth, n // tn),
        in_specs=[pl.BlockSpec((rows, d), lambda l, j: (0, 0)),
                  pl.BlockSpec((None, d, tn), lambda l, j: (l, 0, j)),
                  pl.BlockSpec((None, 1, tn), lambda l, j: (l, 0, j))],
        out_specs=pl.BlockSpec((None, rows, tn), lambda l, j: (l, 0, j)),
        out_shape=jax.ShapeDtypeStruct((depth, rows, n), F32),
        compiler_params=_cparams(("arbitrary", "arbitrary")),
        name="ada_mod",
    )(c_all, ada_w, ada_b.reshape(depth, 1, n))


def _mod_spec(mod, tiles_per_group):
    r = mod.shape[1]
    return pl.BlockSpec((None, r, mod.shape[2]), lambda i: (i // tiles_per_group, 0, 0))


def _ffn_kernel(x_ref, shift_ref, scale_ref, gate_ref, gpre_ref, gpost_ref, wi_ref, wo_ref, o_ref, *, res_w):
    x = x_ref[...]
    h = (_rms(x, gpre_ref[...]) * (1.0 + scale_ref[...]) + shift_ref[...]).astype(BF16)
    g = _dot(h, wi_ref[:, :D_FF])
    u = _dot(h, wi_ref[:, D_FF:])
    a = (g * _sigmoid(g) * u).astype(BF16)
    o = _dot(a, wo_ref[...])
    o_ref[...] = x + (res_w * gate_ref[...]) * _rms(o, gpost_ref[...])


def ffn_block(x, shift, scale, gate, g_pre, g_post, wi, wo, tm, tiles_per_group):
    n, d = x.shape
    row = pl.BlockSpec((tm, d), lambda i: (i, 0))
    const = lambda a: pl.BlockSpec(a.shape, lambda i: (0,) * a.ndim)
    return pl.pallas_call(
        functools.partial(_ffn_kernel, res_w=HALF_STEP),
        grid=(n // tm,),
        in_specs=[row, _mod_spec(shift, tiles_per_group), _mod_spec(scale, tiles_per_group),
                  _mod_spec(gate, tiles_per_group), const(g_pre), const(g_post), const(wi), const(wo)],
        out_specs=row,
        out_shape=jax.ShapeDtypeStruct((n, d), F32),
        compiler_params=_cparams(("arbitrary",)),
        name="ffn_block",
    )(x, shift, scale, gate, g_pre, g_post, wi, wo)


def _rot_block(blk, cos, sin, lane):
    partner = jnp.where((lane & 32) == 0, pltpu.roll(blk, LANES - 32, 1), pltpu.roll(blk, 32, 1))
    return blk * cos + partner * sin


def _inproj_kernel(x_ref, shift_ref, scale_ref, gpre_ref, w_ref, cos_ref, sin_ref,
                   u_ref, q_ref, k_ref, kb_ref, v_ref, vb_ref, qi3_ref, kiw_ref, ki3_ref, rw_ref, vt_ref, *, q_scale):
    x = x_ref[...]
    h = (_rms(x, gpre_ref[...]) * (1.0 + scale_ref[...]) + shift_ref[...]).astype(BF16)
    z = _dot(h, w_ref[...])
    cos = cos_ref[...]
    sin = sin_ref[...]
    lane = lax.broadcasted_iota(I32, cos.shape, 1)
    lo_half = lane < 64
    rot = lambda off: _rot_block(z[:, off:off + LANES], cos, sin, lane)

    u_ref[...] = z[:, ZP_U:ZP_Q]
    for j in range(D_ATT // LANES):
        q_ref[:, j * LANES:(j + 1) * LANES] = (rot(ZP_Q + j * LANES) * q_scale).astype(BF16)
        kr = rot(ZP_K + j * LANES)
        k_ref[:, j * LANES:(j + 1) * LANES] = kr
        kb_ref[:, j * LANES:(j + 1) * LANES] = kr.astype(BF16)
    v = z[:, ZP_V:ZP_QI]
    v_ref[...] = v
    vb_ref[...] = v.astype(BF16)
    vt_ref[...] = v.T.astype(BF16)
    for j in range(N_IDX_HEADS * IDX_DIM // LANES):
        qr = rot(ZP_QI + j * LANES)
        hi = qr.astype(BF16).astype(F32)
        lo = qr - hi
        hi_sw = pltpu.roll(hi, 64, 1)
        lo_sw = pltpu.roll(lo, 64, 1)
        zero = jnp.zeros_like(hi)
        for half in range(2):
            base = (2 * j + half) * 2 * LANES
            a, b = (hi, lo) if half == 0 else (hi_sw, lo_sw)
            a_sw = hi_sw if half == 0 else hi
            qi3_ref[:, base:base + LANES] = jnp.where(lo_half, a, a_sw).astype(BF16)
            qi3_ref[:, base + LANES:base + 2 * LANES] = jnp.where(lo_half, b, zero).astype(BF16)
    raw = z[:, ZP_KIW:ZP_END]
    kiw = jnp.where(lo_half, _rot_block(raw, cos, sin, lane), raw)
    kiw_ref[...] = kiw
    hi = kiw.astype(BF16).astype(F32)
    lo = kiw - hi
    ki3_ref[:, :LANES] = jnp.where(lo_half, hi, pltpu.roll(lo, 64, 1)).astype(BF16)
    ki3_ref[:, LANES:] = jnp.where(lo_half, hi, jnp.zeros_like(hi)).astype(BF16)
    rw_ref[...] = z[:, ZP_RW:ZP_KIW]


def inproj_block(x, shift, scale, g_pre, w, cos_t, sin_t, tm, tiles_per_group, pos_tiles, q_scale):
    n, d = x.shape
    row = lambda width: pl.BlockSpec((tm, width), lambda i: (i, 0))
    const = lambda a: pl.BlockSpec(a.shape, lambda i: (0,) * a.ndim)
    tab = pl.BlockSpec((tm, LANES), lambda i: (i % pos_tiles, 0))
    widths = [(D_S5, F32), (D_ATT, BF16), (D_ATT, F32), (D_ATT, BF16), (D_ATT, F32), (D_ATT, BF16),
              (N_IDX_HEADS * 2 * LANES, BF16), (LANES, F32), (2 * LANES, BF16), (N_RW_COLS, F32)]
    return pl.pallas_call(
        functools.partial(_inproj_kernel, q_scale=q_scale),
        grid=(n // tm,),
        in_specs=[row(d), _mod_spec(shift, tiles_per_group), _mod_spec(scale, tiles_per_group),
                  const(g_pre), const(w), tab, tab],
        out_specs=[row(wd) for wd, _ in widths]
        + [pl.BlockSpec((None, D_ATT, tm), lambda i: (i // pos_tiles, 0, i % pos_tiles))],
        out_shape=[jax.ShapeDtypeStruct((n, wd), dt) for wd, dt in widths]
        + [jax.ShapeDtypeStruct((n // (pos_tiles * tm), D_ATT, pos_tiles * tm), BF16)],
        compiler_params=_cparams(("arbitrary",)),
        name="inproj_block",
    )(x, shift, scale, g_pre, w, cos_t, sin_t)


def _outproj_kernel(x_ref, ys5_ref, yatt_ref, yrw_ref, gate_ref, gpost_ref, w_ref, o_ref):
    o = (_dot(ys5_ref[...].astype(BF16), w_ref[:D_S5, :])
         + _dot(yatt_ref[...].astype(BF16), w_ref[D_S5:D_S5 + D_ATT, :])
         + _dot(yrw_ref[...].astype(BF16), w_ref[D_S5 + D_ATT:, :]))
    o_ref[...] = x_ref[...] + gate_ref[...] * _rms(o, gpost_ref[...])


def outproj_block(x, y_s5, y_att, y_rw, gate, g_post, w, tm, tiles_per_group):
    n, d = x.shape
    row = lambda width: pl.BlockSpec((tm, width), lambda i: (i, 0))
    const = lambda a: pl.BlockSpec(a.shape, lambda i: (0,) * a.ndim)
    return pl.pallas_call(
        _outproj_kernel,
        grid=(n // tm,),
        in_specs=[row(d), row(D_S5), row(D_ATT), row(D_RW), _mod_spec(gate, tiles_per_group),
                  const(g_post), const(w)],
        out_specs=row(d),
        out_shape=jax.ShapeDtypeStruct((n, d), F32),
        compiler_params=_cparams(("arbitrary",)),
        name="outproj_block",
    )(x, y_s5, y_att, y_rw, gate, g_post, w)


def _gelu_tanh(x):
    return 0.5 * x * (1.0 + jnp.tanh(math.sqrt(2.0 / math.pi) * (x + 0.044715 * (x * x * x))))


def _s5_disc_kernel(are_ref, aim_ref, ldt_ref, arec_ref, aimc_ref, ldtc_ref, bre_ref, bim_ref,
                    powre_ref, powim_ref, bbre_ref, bbim_ref, *, rows):
    def zoh(ar, ai, ldt):
        dt = jnp.exp(ldt)
        mag = jnp.exp(ar * dt)
        abr, abi = mag * jnp.cos(ai * dt), mag * jnp.sin(ai * dt)
        den = ar * ar + ai * ai
        nr, ni = abr - 1.0, abi
        return abr, abi, (nr * ar + ni * ai) / den, (ni * ar - nr * ai) / den

    abr, abi, _, _ = zoh(are_ref[...], aim_ref[...], ldt_ref[...])
    pr = jnp.broadcast_to(abr, (rows, abr.shape[1]))
    pi = jnp.broadcast_to(abi, (rows, abr.shape[1]))
    row = lax.broadcasted_iota(I32, pr.shape, 0)
    d = 1
    while d < rows:
        sr = pltpu.roll(pr, d, 0)
        si = pltpu.roll(pi, d, 0)
        m = row >= d
        pr, pi = jnp.where(m, pr * sr - pi * si, pr), jnp.where(m, pr * si + pi * sr, pi)
        d *= 2
    powre_ref[...] = pr
    powim_ref[...] = pi
    _, _, cr, ci = zoh(arec_ref[...], aimc_ref[...], ldtc_ref[...])
    br, bi = bre_ref[...], bim_ref[...]
    bbre_ref[...] = cr * br - ci * bi
    bbim_ref[...] = cr * bi + ci * br


def s5_discretise(a_re, a_im, log_dt, b_re, b_im, rows):
    g, p = a_re.shape
    w = g * p
    ldt = jnp.broadcast_to(log_dt[:, None], (g, p))
    args = (a_re.reshape(1, w), a_im.reshape(1, w), ldt.reshape(1, w),
            a_re.reshape(w, 1), a_im.reshape(w, 1), ldt.reshape(w, 1),
            b_re.reshape(w, S5_GROUP), b_im.reshape(w, S5_GROUP))
    return pl.pallas_call(
        functools.partial(_s5_disc_kernel, rows=rows),
        out_shape=[jax.ShapeDtypeStruct((rows, w), F32), jax.ShapeDtypeStruct((rows, w), F32),
                   jax.ShapeDtypeStruct((w, S5_GROUP), F32), jax.ShapeDtypeStruct((w, S5_GROUP), F32)],
        name="s5_discretise",
    )(*args)


def _s5_head(y, u, d_ref, gluw_ref, glub_ref):
    y = _gelu_tanh(y + d_ref[...] * u)
    return y * _sigmoid(_dot(y.astype(BF16), gluw_ref[...]) + glub_ref[...])


def _s5_kernel(u_ref, h0_ref, powre_ref, powim_ref, wb_ref, wc_ref, d_ref, gluw_ref, glub_ref,
               y_ref, ht_ref, cre, cim):
    t = pl.program_id(1)
    rows = u_ref.shape[0]

    @pl.when(t == 0)
    def _():
        cre[...] = h0_ref[:, :S5_W]
        cim[...] = h0_ref[:, S5_W:]

    u = u_ref[...]
    bu = _dot(u.astype(BF16), wb_ref[...])
    hr, hi = bu[:, :S5_W], bu[:, S5_W:]
    row = lax.broadcasted_iota(I32, hr.shape, 0)
    d = 1
    while d < rows:
        ar, ai = powre_ref[d - 1:d, :], powim_ref[d - 1:d, :]
        sr, si = pltpu.roll(hr, d, 0), pltpu.roll(hi, d, 0)
        m = row >= d
        hr, hi = (hr + jnp.where(m, ar * sr - ai * si, 0.0), hi + jnp.where(m, ar * si + ai * sr, 0.0))
        d *= 2
    pr, pi = powre_ref[...], powim_ref[...]
    c_r, c_i = cre[...], cim[...]
    hr, hi = hr + (pr * c_r - pi * c_i), hi + (pr * c_i + pi * c_r)
    cre[...] = hr[rows - 1:rows, :]
    cim[...] = hi[rows - 1:rows, :]
    ht_ref[:, :S5_W] = hr[rows - 1:rows, :]
    ht_ref[:, S5_W:] = hi[rows - 1:rows, :]
    y = _dot(hr.astype(BF16), wc_ref[:S5_W, :]) + _dot(hi.astype(BF16), wc_ref[S5_W:, :])
    y_ref[...] = _s5_head(y, u, d_ref, gluw_ref, glub_ref)


def s5_prompt(u, h0, powre, powim, wb, wc, d, gluw, glub):
    b, s, _ = u.shape
    rows = powre.shape[0]
    const = lambda a: pl.BlockSpec(a.shape, lambda i, j: (0,) * a.ndim)
    return pl.pallas_call(
        _s5_kernel,
        grid=(b, s // rows),
        in_specs=[pl.BlockSpec((None, rows, D_S5), lambda i, j: (i, j, 0)),
                  pl.BlockSpec((None, 1, 2 * S5_W), lambda i, j: (i, 0, 0)),
                  const(powre), const(powim), const(wb), const(wc), const(d), const(gluw), const(glub)],
        out_specs=[pl.BlockSpec((None, rows, D_S5), lambda i, j: (i, j, 0)),
                   pl.BlockSpec((None, 1, 2 * S5_W), lambda i, j: (i, 0, 0))],
        out_shape=[jax.ShapeDtypeStruct((b, s, D_S5), F32), jax.ShapeDtypeStruct((b, 1, 2 * S5_W), F32)],
        scratch_shapes=[pltpu.VMEM((1, S5_W), F32), pltpu.VMEM((1, S5_W), F32)],
        compiler_params=_cparams(("arbitrary", "arbitrary")),
        name="s5_prompt",
    )(u, h0, powre, powim, wb, wc, d, gluw, glub)


def _s5_step_kernel(u_ref, h0_ref, powre_ref, powim_ref, wb_ref, wc_ref, d_ref, gluw_ref, glub_ref,
                    y_ref, ht_ref):
    ar, ai = powre_ref[0:1, :], powim_ref[0:1, :]
    hr, hi = h0_ref[:, :S5_W], h0_ref[:, S5_W:]
    for t in range(u_ref.shape[0]):
        u = u_ref[t]
        bu = _dot(u.astype(BF16), wb_ref[...])
        hr, hi = ar * hr - ai * hi + bu[:, :S5_W], ar * hi + ai * hr + bu[:, S5_W:]
        y = _dot(hr.astype(BF16), wc_ref[:S5_W, :]) + _dot(hi.astype(BF16), wc_ref[S5_W:, :])
        y_ref[t] = _s5_head(y, u, d_ref, gluw_ref, glub_ref)
    ht_ref[:, :S5_W] = hr
    ht_ref[:, S5_W:] = hi


def s5_sample(u_tm, h0, powre, powim, wb, wc, d, gluw, glub):
    s, b, _ = u_tm.shape
    return pl.pallas_call(
        _s5_step_kernel,
        out_shape=[jax.ShapeDtypeStruct((s, b, D_S5), F32), jax.ShapeDtypeStruct((b, 2 * S5_W), F32)],
        compiler_params=pltpu.CompilerParams(vmem_limit_bytes=VMEM_LIMIT),
        name="s5_sample",
    )(u_tm, h0, powre, powim, wb, wc, d, gluw, glub)


def s5_matrices(bb_re, bb_im, c_re, c_im):
    g, p, h = S5_GROUPS, S5_STATE, S5_GROUP
    eye = jnp.eye(g, dtype=F32)
    bd_in = lambda bb: jnp.einsum('gph,gk->ghkp', bb.reshape(g, p, h), eye).reshape(g * h, g * p)
    bd_out = lambda c: jnp.einsum('ghp,gk->gpkh', c, eye).reshape(g * p, g * h)
    wb = jnp.concatenate([bd_in(bb_re), bd_in(bb_im)], axis=1).astype(BF16)
    wc = jnp.concatenate([bd_out(c_re), -bd_out(c_im)], axis=0).astype(BF16)
    return wb, wc


RW_PACK = 8 * D_RW


def _head_ones(n):
    r = lax.broadcasted_iota(I32, (n, n), 0) // HEAD_DIM
    c = lax.broadcasted_iota(I32, (n, n), 1) // HEAD_DIM
    return jnp.where(r == c, 1.0, 0.0).astype(BF16)


def _seg_sum(x, ones_bd):
    hi, lo = _split(x)
    return _dot(hi, ones_bd) + _dot(lo, ones_bd)


def _softplus(x):
    return jnp.maximum(x, 0.0) + jnp.log(1.0 + jnp.exp(-jnp.abs(x)))


def _rw_pre_kernel(cols_ref, shift0_ref, mu_ref, w0_ref, a0_ref, wlr_ref, kk_ref, ka_ref, rk_ref,
                   o_ref, carry, *, chunk, valid):
    @pl.when(pl.program_id(1) == 0)
    def _():
        carry[...] = shift0_ref[...]

    cf = cols_ref[...]
    tm = cf.shape[0]
    row = lax.broadcasted_iota(I32, (tm, 1), 0)
    prev = jnp.where(row == 0, carry[...], pltpu.roll(cf, 1, 0))
    carry[...] = cf[tm - 1:tm, :]
    xs = cf + (prev - cf) * mu_ref[...]
    r, k, v = xs[:, :D_RW], xs[:, D_RW:2 * D_RW], xs[:, 2 * D_RW:3 * D_RW]
    lr = xs[:, 3 * D_RW:]
    lane = lax.broadcasted_iota(I32, lr.shape, 1)
    t = jnp.where(lane < RW_W_RANK, jnp.tanh(lr), jnp.where(lane < RW_W_RANK + RW_A_RANK, lr, _sigmoid(lr)))
    proj = _dot3(t, wlr_ref[...])
    w = -_softplus(-(w0_ref[...] + proj[:, :D_RW])) - 0.5
    logw = -jnp.exp(w)
    a = _sigmoid(a0_ref[...] + proj[:, D_RW:2 * D_RW])
    g = proj[:, 2 * D_RW:]
    ones_bd = _head_ones(D_RW)
    kk = k * kk_ref[...]
    kk = kk / jnp.maximum(jnp.sqrt(_seg_sum(kk * kk, ones_bd)), 1e-12)
    km = k * (1.0 + (a - 1.0) * ka_ref[...])
    bonus = _seg_sum(r * km * rk_ref[...], ones_bd) * v
    pos = row % chunk
    if valid < chunk:
        live = pos < valid
        zero = jnp.zeros_like(r)
        logw, kk, km, v, r = (jnp.where(live, logw, zero), jnp.where(live, kk, zero), jnp.where(live, km, zero),
                              jnp.where(live, v, zero), jnp.where(live, r, zero))
    gc = logw
    d = 1
    while d < chunk:
        gc = gc + jnp.where(pos >= d, pltpu.roll(gc, d, 0), 0.0)
        d *= 2
    eg = jnp.exp(gc)
    eng = jnp.exp(-gc)
    o_ref[:, 0 * D_RW:1 * D_RW] = r * eg
    o_ref[:, 1 * D_RW:2 * D_RW] = kk * jnp.exp(gc - logw)
    o_ref[:, 2 * D_RW:3 * D_RW] = kk * a * eng
    o_ref[:, 3 * D_RW:4 * D_RW] = km * eng
    o_ref[:, 4 * D_RW:5 * D_RW] = v
    o_ref[:, 5 * D_RW:6 * D_RW] = eg
    o_ref[:, 6 * D_RW:7 * D_RW] = bonus
    o_ref[:, 7 * D_RW:8 * D_RW] = g


def rwkv_prepare(cols, shift0, mu, w0, a0, wlr, k_k, k_a, r_k, tm, chunk, valid):
    b, s, c = cols.shape
    const = lambda a: pl.BlockSpec(a.shape, lambda i, j: (0,) * a.ndim)
    return pl.pallas_call(
        functools.partial(_rw_pre_kernel, chunk=chunk, valid=valid),
        grid=(b, s // tm),
        in_specs=[pl.BlockSpec((None, tm, c), lambda i, j: (i, j, 0)),
                  pl.BlockSpec((None, 1, c), lambda i, j: (i, 0, 0)),
                  const(mu), const(w0), const(a0), const(wlr), const(k_k), const(k_a), const(r_k)],
        out_specs=pl.BlockSpec((None, tm, RW_PACK), lambda i, j: (i, j, 0)),
        out_shape=jax.ShapeDtypeStruct((b, s, RW_PACK), F32),
        scratch_shapes=[pltpu.VMEM((1, c), F32)],
        compiler_params=_cparams(("arbitrary", "arbitrary")),
        name="rwkv_prepare",
    )(cols, shift0, mu, w0, a0, wlr, k_k, k_a, r_k)


def _rw_chunk_kernel(x_ref, s0_ref, lnw_ref, lnb_ref, y_ref, st_ref, s_scr, y_scr, *, chunk):
    @pl.when(pl.program_id(1) == 0)
    def _():
        s_scr[...] = s0_ref[...]

    rows = x_ref.shape[0]
    hd = HEAD_DIM
    ri = lax.broadcasted_iota(I32, (chunk, chunk), 0)
    ci = lax.broadcasted_iota(I32, (chunk, chunk), 1)
    strict = ri > ci
    incl = ri >= ci
    eye = jnp.where(ri == ci, 1.0, 0.0)
    ek = lax.broadcasted_iota(I32, (hd, hd), 0) == lax.broadcasted_iota(I32, (hd, hd), 1)
    eye_k = jnp.where(ek, 1.0, 0.0)
    n_sq = chunk.bit_length() - 2
    n_chunks = rows // chunk
    probs = [(c, h) for c in range(n_chunks) for h in range(N_RW_HEADS)]
    each = lambda f: [f(i) for i in range(len(probs))]

    def col(i, j):
        c, h = probs[i]
        return x_ref[c * chunk:(c + 1) * chunk, j * D_RW + h * hd:j * D_RW + (h + 1) * hd]

    rt, kt, bt, km, v = (each(lambda i: col(i, j)) for j in range(5))
    lhs = each(lambda i: _split(jnp.concatenate([kt[i], rt[i]], axis=0)))
    bts, kms, vs = each(lambda i: _split(bt[i])), each(lambda i: _split(km[i])), each(lambda i: _split(v[i]))
    gb = each(lambda i: _mm3_nt(lhs[i], bts[i]))
    gk = each(lambda i: _mm3_nt(lhs[i], kms[i]))
    a_bb = each(lambda i: jnp.where(strict, gb[i][:chunk], 0.0))
    a_rb = each(lambda i: _split(jnp.where(incl, gb[i][chunk:], 0.0)))
    a_kr = each(lambda i: _split(jnp.concatenate([jnp.where(strict, gk[i][:chunk], 0.0),
                                                  jnp.where(incl, gk[i][chunk:], 0.0)], axis=0)))
    av = each(lambda i: _mm3(a_kr[i], vs[i]))
    minv = each(lambda i: eye - a_bb[i])
    p = a_bb
    for _ in range(n_sq):
        ps = each(lambda i: _split(p[i]))
        p = each(lambda i: _mm3(ps[i], ps[i]))
        minv = each(lambda i: minv[i] + _mm3(_split(minv[i]), _split(p[i])))
    minvs = each(lambda i: _split(minv[i]))
    khat = each(lambda i: _mm3(minvs[i], _split(kt[i])))
    p1 = each(lambda i: _mm3(minvs[i], _split(av[i][:chunk])))
    rhat = each(lambda i: rt[i] - _mm3(a_rb[i], _split(khat[i])))
    y1 = each(lambda i: av[i][chunk:] - _mm3(a_rb[i], _split(p1[i])))
    tb = each(lambda i: _mm3(_split(jnp.concatenate([khat[i].T, p1[i].T], axis=0)), bts[i]))
    vk = each(lambda i: _mm3(_split(v[i].T), kms[i]))

    def eg_last(i):
        c, h = probs[i]
        return x_ref[(c + 1) * chunk - 1:(c + 1) * chunk, 5 * D_RW + h * hd:5 * D_RW + (h + 1) * hd]

    gmat = each(lambda i: (eye_k - tb[i][:hd]) * eg_last(i))
    umat = each(lambda i: (vk[i] - tb[i][hd:]) * eg_last(i))
    state = [s_scr[h] for h in range(N_RW_HEADS)]
    for i, (c, h) in enumerate(probs):
        y_scr[c * chunk:(c + 1) * chunk, h * hd:(h + 1) * hd] = _dot3_nt(rhat[i], state[h]) + y1[i]
        state[h] = _dot3(state[h], gmat[i]) + umat[i]
    for h in range(N_RW_HEADS):
        s_scr[h] = state[h]
    st_ref[...] = s_scr[...]
    y = y_scr[...]
    ones_bd = _head_ones(D_RW)
    mean = _seg_sum(y, ones_bd) * (1.0 / hd)
    yc = y - mean
    var = _seg_sum(yc * yc, ones_bd) * (1.0 / hd)
    yn = yc * lax.rsqrt(var + RW_GN_EPS) * lnw_ref[...] + lnb_ref[...]
    y_ref[...] = (yn + x_ref[:, 6 * D_RW:7 * D_RW]) * x_ref[:, 7 * D_RW:8 * D_RW]


def rwkv_chunked(packed, state0, ln_w, ln_b, rows, chunk):
    b, s, _ = packed.shape
    const = lambda a: pl.BlockSpec(a.shape, lambda i, j: (0,) * a.ndim)
    st_spec = pl.BlockSpec((None, N_RW_HEADS, HEAD_DIM, HEAD_DIM), lambda i, j: (i, 0, 0, 0))
    return pl.pallas_call(
        functools.partial(_rw_chunk_kernel, chunk=chunk),
        grid=(b, s // rows),
        in_specs=[pl.BlockSpec((None, rows, RW_PACK), lambda i, j: (i, j, 0)), st_spec, const(ln_w), const(ln_b)],
        out_specs=[pl.BlockSpec((None, rows, D_RW), lambda i, j: (i, j, 0)), st_spec],
        out_shape=[jax.ShapeDtypeStruct((b, s, D_RW), F32),
                   jax.ShapeDtypeStruct((b, N_RW_HEADS, HEAD_DIM, HEAD_DIM), F32)],
        scratch_shapes=[pltpu.VMEM((N_RW_HEADS, HEAD_DIM, HEAD_DIM), F32), pltpu.VMEM((rows, D_RW), F32)],
        compiler_params=_cparams(("arbitrary", "arbitrary")),
        name="rwkv_chunked",
    )(packed, state0, ln_w, ln_b)


def rwkv_lowrank_matrix(w2, a2, g2):
    z = lambda r: jnp.zeros((r, D_RW), F32)
    return jnp.concatenate([
        jnp.concatenate([w2, z(RW_W_RANK), z(RW_W_RANK)], axis=1),
        jnp.concatenate([z(RW_A_RANK), a2, z(RW_A_RANK)], axis=1),
        jnp.concatenate([z(RW_G_RANK), z(RW_G_RANK), g2], axis=1)], axis=0)


def _float_key(x):
    b = lax.bitcast_convert_type(x, I32)
    return jnp.where(b < 0, jnp.int32(INT_MIN) - b, b)


def _py_key(v):
    import numpy as np
    b = int(np.float32(v).view(np.int32))
    return -(b & 0x7FFFFFFF) if b < 0 else b


KEY_NEG_INF = _py_key(NEG_INF)


def _count_ge(sc_ref, nc, cand):
    parts = [jnp.where(sc_ref[c] >= cand, 1.0, 0.0) for c in range(nc)]
    while len(parts) > 1:
        parts = [parts[i] + parts[i + 1] for i in range(0, len(parts) - 1, 2)] + (parts[-1:] if len(parts) % 2 else [])
    return jnp.sum(parts[0], axis=1, keepdims=True)


def _select_threshold(sc_ref, nc, k):
    _, r, wc = sc_ref.shape

    def bit_body(i, carry):
        t, cge = carry
        cand = t + jnp.left_shift(jnp.int32(1), 31 - i)
        cnt = _count_ge(sc_ref, nc, cand)
        ok = cnt >= k
        return jnp.where(ok, cand, t), jnp.where(ok, cnt, cge)

    t0 = jnp.full((r, 1), INT_MIN, I32)
    cge0 = jnp.zeros((r, 1), F32) + jnp.asarray(nc * wc, F32)
    return lax.fori_loop(0, 32, bit_body, (t0, cge0))


def _resolve_ties(sc_ref, nc, t, need):
    _, r, wc = sc_ref.shape
    upper = jnp.where(lax.broadcasted_iota(I32, (wc, wc), 0) < lax.broadcasted_iota(I32, (wc, wc), 1), 1.0, 0.0)

    def body(c, offs):
        x = sc_ref[c]
        e = x == t
        ef = jnp.where(e, 1.0, 0.0)
        rank = _dot(ef, upper) + offs
        sc_ref[c] = jnp.where(e, jnp.where(rank >= need, jnp.int32(INT_MIN), x), x)
        return offs + jnp.sum(ef, axis=1, keepdims=True)

    lax.fori_loop(0, nc, body, jnp.zeros((r, 1), F32))


def _topk_mask_prepare(sc_ref, nc, k, valid_rows):
    t, cge = _select_threshold(sc_ref, nc, k)
    r = t.shape[0]
    live = lax.broadcasted_iota(I32, (r, 1), 0) < valid_rows
    tied = live & (cge > k) & (t != KEY_NEG_INF)

    @pl.when(jnp.max(jnp.where(tied, 1.0, 0.0)) > 0.0)
    def _():
        cgt = _count_ge(sc_ref, nc, t + 1)
        _resolve_ties(sc_ref, nc, t, k - cgt)

    return t


def _indexer_total(s4, wcols, rows):
    tot = jnp.maximum(s4[:rows], 0.0) * wcols[0]
    for h in range(1, N_IDX_HEADS):
        tot = tot + jnp.maximum(s4[h * rows:(h + 1) * rows], 0.0) * wcols[h]
    return tot


SUBLANES = 8


def _col_tree(x, op, rows=SUBLANES):
    parts = [x[i:i + rows] for i in range(0, x.shape[0], rows)]
    while len(parts) > 1:
        parts = [op(parts[i], parts[i + 1]) for i in range(0, len(parts) - 1, 2)] + (parts[-1:] if len(parts) % 2 else [])
    return parts[0]


def _col_sum(x):
    return _col_tree(x, jnp.add)


def _col_max(x):
    return _col_tree(x, jnp.maximum)


def _count_ge_t(sc_ref, nc, cand):
    _, wc, nq = sc_ref.shape

    def body(c, acc):
        return acc + _col_sum(jnp.where(sc_ref[c] >= cand, 1.0, 0.0))

    acc = lax.fori_loop(0, nc, body, jnp.zeros((SUBLANES, nq), F32))
    return jnp.sum(acc, axis=0, keepdims=True)


def _key16_value(k16):
    pattern = jnp.where(k16 < 0, jnp.int32(0x8000) - k16, k16)
    return lax.bitcast_convert_type(jnp.left_shift(pattern, 16), F32)


def _floor_bf16(x):
    b = lax.bitcast_convert_type(x, I32)
    trunc = b & jnp.int32(-65536)
    down = jnp.where((b < 0) & ((b & jnp.int32(0xFFFF)) != 0), trunc + jnp.int32(0x10000), trunc)
    return lax.bitcast_convert_type(down, F32).astype(BF16)


def _topk_mask_prepare_t(sc_ref, sd_ref, nc, k):
    _, wc, nq = sc_ref.shape
    packed = 2 * SUBLANES

    def count16(cand):
        def body(c, acc):
            hit = jnp.where(sd_ref[c] >= cand, jnp.ones((), BF16), jnp.zeros((), BF16))
            return acc + _col_tree(hit, jnp.add, packed).astype(F32)

        acc = lax.fori_loop(0, nc, body, jnp.zeros((packed, nq), F32))
        return jnp.sum(acc, axis=0, keepdims=True)

    def bit16_body(i, carry):
        t, cge = carry
        cand = t + jnp.left_shift(jnp.int32(1), 15 - i)
        first_normal = 1 << 7
        probe = jnp.where((cand > 0) & (cand < first_normal), first_normal,
                          jnp.where((cand < 0) & (cand > -first_normal), 0, cand))
        cnt = count16(_key16_value(probe).astype(BF16))
        ok = cnt >= k
        return jnp.where(ok, cand, t), jnp.where(ok, cnt, cge)

    key16_min = -(2 ** 15)
    cge0 = jnp.zeros((1, nq), F32) + jnp.asarray(nc * wc, F32)
    t16, cge = lax.fori_loop(0, 16, bit16_body, (jnp.full((1, nq), key16_min, I32), cge0))

    def bit_body(i, carry):
        t, cge = carry
        cand = t + jnp.left_shift(jnp.int32(1), 15 - i)
        cnt = _count_ge_t(sc_ref, nc, cand)
        ok = cnt >= k
        return jnp.where(ok, cand, t), jnp.where(ok, cnt, cge)

    base = jnp.where(t16 == key16_min, jnp.int32(INT_MIN), _float_key(_key16_value(t16)))
    t, cge = lax.fori_loop(0, 16, bit_body, (base, cge))
    tied = (cge > k) & (t != KEY_NEG_INF)

    @pl.when(jnp.max(jnp.where(tied, 1.0, 0.0)) > 0.0)
    def _():
        need = k - _count_ge_t(sc_ref, nc, t + 1)
        ri = lax.broadcasted_iota(I32, (wc, wc), 0)
        ci = lax.broadcasted_iota(I32, (wc, wc), 1)
        lower = jnp.where(ci < ri, 1.0, 0.0)

        def body(c, offs):
            x = sc_ref[c]
            e = x == t
            ef = jnp.where(e, 1.0, 0.0)
            rank = _dot(lower, ef) + offs
            sc_ref[c] = jnp.where(e, jnp.where(rank >= need, jnp.int32(INT_MIN), x), x)
            return offs + jnp.sum(ef, axis=0, keepdims=True)

        lax.fori_loop(0, nc, body, jnp.zeros((1, nq), F32))

    return t


def _dsa_prompt_kernel(q_ref, kb_ref, vt_ref, qi3_ref, ki3_ref, kiw_ref, o_ref, sc, sd, *, k_sel):
    j = pl.program_id(1)
    qb, wc = Q_BLOCK, KEY_CHUNK
    nck = (j * qb + qb + wc - 1) // wc
    qpos = j * qb + lax.broadcasted_iota(I32, (1, qb), 1)
    keybase = lax.broadcasted_iota(I32, (wc, 1), 0)

    lhs = jnp.concatenate([qi3_ref[:, 2 * LANES * h:2 * LANES * (h + 1)] for h in range(N_IDX_HEADS)], axis=0)
    kiw_t = kiw_ref[...].T
    wrows = [kiw_t[IDX_DIM + h:IDX_DIM + h + 1, :] for h in range(N_IDX_HEADS)]

    def score_chunk(c, diagonal):
        kc = ki3_ref[pl.ds(pl.multiple_of(c * wc, wc), wc), :]
        s4 = _dot_nt(kc, lhs)
        tot = jnp.maximum(s4[:, :qb], 0.0) * wrows[0]
        for h in range(1, N_IDX_HEADS):
            tot = tot + jnp.maximum(s4[:, h * qb:(h + 1) * qb], 0.0) * wrows[h]
        if diagonal:
            tot = jnp.where(c * wc + keybase <= qpos, tot, NEG_INF)
        sc[c] = _float_key(tot)
        sd[c] = _floor_bf16(tot)

    def score_body(c, carry):
        score_chunk(c, False)
        return carry

    lax.fori_loop(0, nck - 1, score_body, 0)
    score_chunk(nck - 1, True)
    t = _topk_mask_prepare_t(sc, sd, nck, k_sel)
    t_sel = jnp.maximum(t, KEY_NEG_INF + 1)

    lane = lax.broadcasted_iota(I32, (qb, LANES), 1)
    n_pairs = N_ATT_HEADS // 2
    qpair = []
    for pr in range(n_pairs):
        blk = q_ref[:, pr * LANES:(pr + 1) * LANES]
        zero = jnp.zeros_like(blk)
        qpair.append(jnp.concatenate([jnp.where(lane < HEAD_DIM, blk, zero), jnp.where(lane < HEAD_DIM, zero, blk)],
                                     axis=0))

    def att_body(c, carry):
        ms, ls, accs = carry
        off = pl.multiple_of(c * wc, wc)
        bias = jnp.where(sc[c] >= t_sel, 0.0, NEG_INF)
        bias2 = jnp.concatenate([bias, bias], axis=1)
        ss = [_dot_nt(kb_ref[pl.ds(off, wc), pr * LANES:(pr + 1) * LANES], qpair[pr]) + bias2
              for pr in range(n_pairs)]
        new_m = [jnp.maximum(ms[pr], jnp.max(_col_max(ss[pr]), axis=0, keepdims=True)) for pr in range(n_pairs)]
        alphas = [jnp.exp2(ms[pr] - new_m[pr]) for pr in range(n_pairs)]
        ps = [jnp.exp2(ss[pr] - new_m[pr]) for pr in range(n_pairs)]
        new_l = [alphas[pr] * ls[pr] + jnp.sum(_col_sum(ps[pr]), axis=0, keepdims=True) for pr in range(n_pairs)]
        new_acc = [alphas[pr] * accs[pr]
                   + _dot(vt_ref[pr * LANES:(pr + 1) * LANES, pl.ds(off, wc)], ps[pr].astype(BF16))
                   for pr in range(n_pairs)]
        return tuple(new_m), tuple(new_l), tuple(new_acc)

    init = (tuple(jnp.full((1, 2 * qb), NEG_INF, F32) for _ in range(n_pairs)),
            tuple(jnp.zeros((1, 2 * qb), F32) for _ in range(n_pairs)),
            tuple(jnp.zeros((LANES, 2 * qb), F32) for _ in range(n_pairs)))
    _, ls, accs = lax.fori_loop(0, nck, att_body, init)
    sub = lax.broadcasted_iota(I32, (LANES, qb), 0)
    for pr in range(n_pairs):
        o2 = accs[pr] / ls[pr]
        o_ref[:, pr * LANES:(pr + 1) * LANES] = jnp.where(sub < HEAD_DIM, o2[:, :qb], o2[:, qb:]).T


def dsa_prompt(q, kb, vt, qi3, ki3, kiw):
    b, s, _ = q.shape
    k_sel = min(TOPK_MAX, s // 4)
    blk = lambda w: pl.BlockSpec((None, Q_BLOCK, w), lambda i, j: (i, j, 0))
    full = lambda w: pl.BlockSpec((None, s, w), lambda i, j: (i, 0, 0))
    return pl.pallas_call(
        functools.partial(_dsa_prompt_kernel, k_sel=k_sel),
        grid=(b, s // Q_BLOCK),
        in_specs=[blk(D_ATT), full(D_ATT), pl.BlockSpec((None, D_ATT, s), lambda i, j: (i, 0, 0)),
                  blk(N_IDX_HEADS * 2 * LANES), full(2 * LANES), blk(LANES)],
        out_specs=blk(D_ATT),
        out_shape=jax.ShapeDtypeStruct((b, s, D_ATT), F32),
        scratch_shapes=[pltpu.VMEM((s // KEY_CHUNK, KEY_CHUNK, Q_BLOCK), I32),
                        pltpu.VMEM((s // KEY_CHUNK, KEY_CHUNK, Q_BLOCK), BF16)],
        compiler_params=_cparams(("arbitrary", "arbitrary")),
        name="dsa_prompt",
    )(q, kb, vt, qi3, ki3, kiw)


SAMPLE_ROWS = 8
PAGE_GROUP = 16
IDX_PAGE_GROUP = 32


def _dsa_sample_index_kernel(pt_ref, *refs, n_valid, k_sel):
    pages = refs[:IDX_PAGE_GROUP]
    qi3_ref, kiw_ref, ki3n_ref, sc_ref, thr_ref = refs[IDX_PAGE_GROUP:]
    g = pl.program_id(1)
    r = SAMPLE_ROWS
    n_chunks = sc_ref.shape[0]
    q3 = qi3_ref[...].astype(F32)
    qhi = jnp.concatenate([q3[:, 2 * LANES * h:2 * LANES * h + IDX_DIM] for h in range(N_IDX_HEADS)],
                          axis=0).astype(BF16)
    qlo = jnp.concatenate([q3[:, 2 * LANES * h + LANES:2 * LANES * h + LANES + IDX_DIM]
                           for h in range(N_IDX_HEADS)], axis=0).astype(BF16)
    wcols = [kiw_ref[:, IDX_DIM + h:IDX_DIM + h + 1] for h in range(N_IDX_HEADS)]

    def scores(kt):
        khi, klo = _split(kt)
        return _dot(qhi, khi) + (_dot(qhi, klo) + _dot(qlo, khi))

    for i in range(IDX_PAGE_GROUP):
        sc_ref[g * IDX_PAGE_GROUP + i] = _float_key(_indexer_total(scores(pages[i][...]), wcols, r))

    @pl.when(g == pl.num_programs(1) - 1)
    def _():
        tot = _indexer_total(scores(ki3n_ref[...]), wcols, r)
        row = lax.broadcasted_iota(I32, tot.shape, 0)
        col = lax.broadcasted_iota(I32, tot.shape, 1)
        tot = jnp.where((col <= row) & (col < n_valid), tot, NEG_INF)
        sc_ref[n_chunks - 1] = _float_key(tot)
        t = _topk_mask_prepare(sc_ref, n_chunks, k_sel, n_valid)
        thr_ref[...] = jnp.broadcast_to(t, thr_ref.shape)


def dsa_sample_index(page_table, cache_kidx_t, layer, qi3, kiw, ki_new_t, n_valid):
    b, n_pages = page_table.shape
    n_chunks = n_pages + 1
    k_sel = min(TOPK_MAX, (n_pages * PAGE_SIZE + n_valid) // 4)
    page_spec = lambda i: pl.BlockSpec((None, None, IDX_DIM, PAGE_SIZE),
                                       lambda bi, g, pt: (layer, pt[bi, g * IDX_PAGE_GROUP + i], 0, 0))
    per_b = lambda shape: pl.BlockSpec((None,) + shape, lambda bi, g, pt: (bi,) + (0,) * len(shape))
    r = SAMPLE_ROWS
    return pl.pallas_call(
        functools.partial(_dsa_sample_index_kernel, n_valid=n_valid, k_sel=k_sel),
        grid_spec=pltpu.PrefetchScalarGridSpec(
            num_scalar_prefetch=1,
            grid=(b, n_pages // IDX_PAGE_GROUP),
            in_specs=[page_spec(i) for i in range(IDX_PAGE_GROUP)]
            + [per_b((r, N_IDX_HEADS * 2 * LANES)), per_b((r, LANES)), per_b((IDX_DIM, PAGE_SIZE))],
            out_specs=[per_b((n_chunks, r, PAGE_SIZE)), per_b((r, PAGE_SIZE))],
        ),
        out_shape=[jax.ShapeDtypeStruct((b, n_chunks, r, PAGE_SIZE), I32),
                   jax.ShapeDtypeStruct((b, r, PAGE_SIZE), I32)],
        compiler_params=_cparams(("arbitrary", "arbitrary")),
        name="dsa_sample_index",
    )(page_table, *([cache_kidx_t] * IDX_PAGE_GROUP), qi3, kiw, ki_new_t)


def _dsa_sample_attend_kernel(pt_ref, *refs, n_valid):
    kpages = refs[:PAGE_GROUP]
    vpages = refs[PAGE_GROUP:2 * PAGE_GROUP]
    qbd_ref, sel_ref, seln_ref, thr_ref, kn_ref, vn_ref, o_ref, m_scr, l_scr, acc_scr = refs[2 * PAGE_GROUP:]
    g = pl.program_id(1)
    r = SAMPLE_ROWS

    @pl.when(g == 0)
    def _():
        m_scr[...] = jnp.full(m_scr.shape, NEG_INF, F32)
        l_scr[...] = jnp.zeros(l_scr.shape, F32)
        acc_scr[...] = jnp.zeros(acc_scr.shape, F32)

    qbd = qbd_ref[...]
    thr = thr_ref[...]
    per_head = lambda x: jnp.concatenate([x] * N_ATT_HEADS, axis=0)
    thr_all = per_head(thr)

    def update(sels, extra, kts, vts):
        n = len(kts)
        scores = jnp.concatenate([_dot(qbd, kt) for kt in kts], axis=1)
        mask = jnp.concatenate([per_head(sel) for sel in sels], axis=1) >= jnp.concatenate([thr_all] * n, axis=1)
        if extra is not None:
            mask = mask & (per_head(extra) > 0)
        s = jnp.where(mask, scores, NEG_INF)
        m_prev = m_scr[...]
        m_new = jnp.maximum(m_prev, jnp.max(s, axis=1, keepdims=True))
        alpha = jnp.exp(m_prev - m_new)
        p = jnp.exp(s - m_new)
        l_scr[...] = alpha * l_scr[...] + jnp.sum(p, axis=1, keepdims=True)
        p = p.astype(BF16)
        pv = _dot_nt(p[:, :PAGE_SIZE], vts[0])
        for i in range(1, n):
            pv = pv + _dot_nt(p[:, i * PAGE_SIZE:(i + 1) * PAGE_SIZE], vts[i])
        acc_scr[...] = alpha * acc_scr[...] + pv
        m_scr[...] = m_new

    update([sel_ref[i] for i in range(PAGE_GROUP)], None,
           [kpages[i][...].astype(BF16) for i in range(PAGE_GROUP)],
           [vpages[i][...].astype(BF16) for i in range(PAGE_GROUP)])

    @pl.when(g == pl.num_programs(1) - 1)
    def _():
        row = lax.broadcasted_iota(I32, thr.shape, 0)
        col = lax.broadcasted_iota(I32, thr.shape, 1)
        causal = jnp.where((col <= row) & (col < n_valid), 1, 0)
        update([seln_ref[...]], causal, [kn_ref[...]], [vn_ref[...]])
        out = acc_scr[...] / l_scr[...]
        lane = lax.broadcasted_iota(I32, (r, D_ATT), 1) // HEAD_DIM
        tot = jnp.where(lane == 0, out[:r], 0.0)
        for h in range(1, N_ATT_HEADS):
            tot = tot + jnp.where(lane == h, out[h * r:(h + 1) * r], 0.0)
        o_ref[...] = tot


def dsa_sample_attend(page_table, cache_kt, cache_vt, layer, q_bd, sel, thr, k_new_t, v_new_t, n_valid):
    b, n_pages = page_table.shape
    r = SAMPLE_ROWS
    page_spec = lambda i: pl.BlockSpec((None, None, D_ATT, PAGE_SIZE),
                                       lambda bi, g, pt: (layer, pt[bi, g * PAGE_GROUP + i], 0, 0))
    per_b = lambda shape: pl.BlockSpec((None,) + shape, lambda bi, g, pt: (bi,) + (0,) * len(shape))
    rows = N_ATT_HEADS * r
    return pl.pallas_call(
        functools.partial(_dsa_sample_attend_kernel, n_valid=n_valid),
        grid_spec=pltpu.PrefetchScalarGridSpec(
            num_scalar_prefetch=1,
            grid=(b, n_pages // PAGE_GROUP),
            in_specs=[page_spec(i) for i in range(PAGE_GROUP)] + [page_spec(i) for i in range(PAGE_GROUP)]
            + [per_b((rows, D_ATT)),
               pl.BlockSpec((None, PAGE_GROUP, r, PAGE_SIZE), lambda bi, g, pt: (bi, g, 0, 0)),
               pl.BlockSpec((None, None, r, PAGE_SIZE), lambda bi, g, pt: (bi, n_pages, 0, 0)),
               per_b((r, PAGE_SIZE)), per_b((D_ATT, PAGE_SIZE)), per_b((D_ATT, PAGE_SIZE))],
            out_specs=per_b((r, D_ATT)),
            scratch_shapes=[pltpu.VMEM((rows, 1), F32), pltpu.VMEM((rows, 1), F32), pltpu.VMEM((rows, D_ATT), F32)],
        ),
        out_shape=jax.ShapeDtypeStruct((b, r, D_ATT), F32),
        compiler_params=_cparams(("arbitrary", "arbitrary")),
        name="dsa_sample_attend",
    )(page_table, *([cache_kt] * PAGE_GROUP), *([cache_vt] * PAGE_GROUP), q_bd, sel, sel, thr, k_new_t, v_new_t)


def pack_w_in(w_in):
    d = w_in.shape[0]
    pad = jnp.zeros((d, ZP_END - ZP_KIW - (OFF_RW - OFF_KI)), w_in.dtype)
    return jnp.concatenate([w_in[:, OFF_S5:OFF_KI], w_in[:, OFF_RW:], w_in[:, OFF_KI:OFF_RW], pad], axis=1).astype(BF16)


def rope_tables(pos):
    half = HEAD_DIM // 2
    inv = ROPE_THETA ** (-jnp.arange(half, dtype=F32) / half)
    ang = pos.astype(F32)[:, None] * inv[None, :]
    cos, sin = jnp.cos(ang), jnp.sin(ang)
    return jnp.tile(jnp.concatenate([cos, cos], 1), (1, 2)), jnp.tile(jnp.concatenate([-sin, sin], 1), (1, 2))


def block_diag_queries(q):
    b, r, d = q.shape
    head_of_lane = jnp.arange(d) // HEAD_DIM
    keep = head_of_lane[None, :] == jnp.arange(N_ATT_HEADS)[:, None]
    return jnp.where(keep[None, :, None, :], q[:, None], jnp.zeros((), q.dtype)).reshape(b, N_ATT_HEADS * r, d)


def _pad_rows(a, rows):
    return jnp.pad(a, ((0, 0), (0, rows - a.shape[1])) + ((0, 0),) * (a.ndim - 2))


Q_SCALE = HEAD_DIM ** -0.5
Q_SCALE_EXP2 = Q_SCALE * math.log2(math.e)
PROMPT_TILE = 256
RW_PRE_TILE = 512
RW_STEP_ROWS = 256


def kernel(x_prompt, x_sample, c_prompt, c_sample, cache_k, cache_v, cache_kidx, state_s5_re, state_s5_im, state_rwkv, state_rwkv_shift, page_table, ada_w, ada_b, norm_pre, norm_post, ffn_wi, ffn_wo, w_in, w_out, s5_a_re, s5_a_im, s5_log_dt, s5_b_re, s5_b_im, s5_c_re, s5_c_im, s5_d, s5_glu_w, s5_glu_b, rw_mu, rw_w0, rw_w2, rw_a0, rw_a2, rw_g2, rw_k_k, rw_k_a, rw_r_k, rw_ln_w, rw_ln_b):
    bp, sp, d = x_prompt.shape
    bs, ss, _ = x_sample.shape
    depth = ada_w.shape[0]
    past = page_table.shape[1] * PAGE_SIZE
    np_tok, ns_tok = bp * sp, bs * ss
    tpb = sp // PROMPT_TILE
    row1 = lambda a: a.reshape(1, -1)

    c_all = _pad_rows(jnp.concatenate([c_prompt, c_sample], axis=0)[None], -(-(bp + bs) // 8) * 8)[0]
    mod = ada_mod(c_all, ada_w, ada_b).reshape(depth, c_all.shape[0], N_SUB, 3, d)

    cos_p, sin_p = rope_tables(jnp.arange(sp))
    cos_s, sin_s = (jnp.tile(t, (bs, 1)) for t in rope_tables(past + jnp.arange(ss)))
    n_pool = cache_k.shape[1]
    cache_kt = jnp.transpose(cache_k, (0, 1, 3, 4, 2)).reshape(depth, n_pool, D_ATT, PAGE_SIZE)
    cache_vt = jnp.transpose(cache_v, (0, 1, 3, 4, 2)).reshape(depth, n_pool, D_ATT, PAGE_SIZE)
    cache_kidx_t = jnp.swapaxes(cache_kidx, 2, 3)

    xp = x_prompt.reshape(np_tok, d)
    xs = x_sample.reshape(ns_tok, d)
    outs_p, outs_s = [], []
    for l in range(depth):
        mod_p = mod[l, :bp]
        mod_s = jnp.repeat(mod[l, bp:bp + bs], ss, axis=0)
        pm = lambda i, j: mod_p[:, i, j][:, None, :]
        sm = lambda i, j: mod_s[:, i, j][None]
        npre = lambda i: row1(norm_pre[l, i])
        npost = lambda i: row1(norm_post[l, i])
        wi0, wo0 = ffn_wi[l, 0].astype(BF16), ffn_wo[l, 0].astype(BF16)
        wi1, wo1 = ffn_wi[l, 1].astype(BF16), ffn_wo[l, 1].astype(BF16)
        w_in_p = pack_w_in(w_in[l])
        w_out_b = w_out[l].astype(BF16)
        powre, powim, bb_re, bb_im = s5_discretise(s5_a_re[l], s5_a_im[l], s5_log_dt[l], s5_b_re[l], s5_b_im[l],
                                                   rows=S5_CHUNK)
        wb, wc = s5_matrices(bb_re, bb_im, s5_c_re[l], s5_c_im[l])
        s5_tail = (powre, powim, wb, wc, row1(s5_d[l]), s5_glu_w[l].astype(BF16), row1(s5_glu_b[l]))
        wlr = rwkv_lowrank_matrix(rw_w2[l], rw_a2[l], rw_g2[l])
        rw_pre = (row1(rw_mu[l]), row1(rw_w0[l]), row1(rw_a0[l]), wlr, row1(rw_k_k[l]), row1(rw_k_a[l]),
                  row1(rw_r_k[l]))
        ln = (row1(rw_ln_w[l]), row1(rw_ln_b[l]))

        xp = ffn_block(xp, pm(0, 0), pm(0, 1), pm(0, 2), npre(0), npost(0), wi0, wo0, PROMPT_TILE, tpb)
        u, q, k, kb, v, _, qi3, kiw, ki3, rw, vt = inproj_block(xp, pm(1, 0), pm(1, 1), npre(1), w_in_p, cos_p, sin_p,
                                                                PROMPT_TILE, tpb, tpb, Q_SCALE_EXP2)
        seq = lambda a: a.reshape(bp, sp, a.shape[-1])
        y_s5, h_s5 = s5_prompt(seq(u), jnp.zeros((bp, 1, 2 * S5_W), F32), *s5_tail)
        y_att = dsa_prompt(seq(q), seq(kb), vt, seq(qi3), seq(ki3), seq(kiw))
        cols = seq(rw)
        packed = rwkv_prepare(cols, jnp.zeros((bp, 1, N_RW_COLS), F32), *rw_pre, RW_PRE_TILE, RW_CHUNK, RW_CHUNK)
        y_rw, st_rw = rwkv_chunked(packed, jnp.zeros((bp, N_RW_HEADS, HEAD_DIM, HEAD_DIM), F32), *ln,
                                   RW_STEP_ROWS, RW_CHUNK)
        flat = lambda a: a.reshape(np_tok, a.shape[-1])
        xp = outproj_block(xp, flat(y_s5), flat(y_att), flat(y_rw), pm(1, 2), npost(1), w_out_b, PROMPT_TILE, tpb)
        xp = ffn_block(xp, pm(2, 0), pm(2, 1), pm(2, 2), npre(2), npost(2), wi1, wo1, PROMPT_TILE, tpb)
        outs_p.append((k.reshape(bp, sp, N_ATT_HEADS, HEAD_DIM), v.reshape(bp, sp, N_ATT_HEADS, HEAD_DIM),
                       seq(kiw)[:, :, :IDX_DIM],
                       h_s5[:, 0, :S5_W].reshape(bp, S5_GROUPS, S5_STATE),
                       h_s5[:, 0, S5_W:].reshape(bp, S5_GROUPS, S5_STATE), st_rw, cols[:, sp - 1]))

        xs = ffn_block(xs, sm(0, 0), sm(0, 1), sm(0, 2), npre(0), npost(0), wi0, wo0, ns_tok, 1)
        u, q, k, kb, v, vb, qi3, kiw, ki3, rw, _ = inproj_block(xs, sm(1, 0), sm(1, 1), npre(1), w_in_p, cos_s, sin_s,
                                                                ns_tok, 1, 1, Q_SCALE)
        seq = lambda a: a.reshape(bs, ss, a.shape[-1])
        h0 = jnp.concatenate([state_s5_re[l].reshape(bs, S5_W), state_s5_im[l].reshape(bs, S5_W)], axis=1)
        y_s5, h_s5 = s5_sample(jnp.swapaxes(seq(u), 0, 1), h0, *s5_tail)
        y_s5 = jnp.swapaxes(y_s5, 0, 1)
        tok_t = lambda a: jnp.swapaxes(_pad_rows(a, PAGE_SIZE), 1, 2)
        sel, thr = dsa_sample_index(page_table, cache_kidx_t, l, _pad_rows(seq(qi3), SAMPLE_ROWS),
                                    _pad_rows(seq(kiw), SAMPLE_ROWS), tok_t(seq(kiw)[:, :, :IDX_DIM]), ss)
        q_bd = block_diag_queries(_pad_rows(seq(q), SAMPLE_ROWS))
        y_att = dsa_sample_attend(page_table, cache_kt, cache_vt, l, q_bd, sel, thr, tok_t(seq(kb)), tok_t(seq(vb)),
                                  ss)[:, :ss]
        cols = seq(rw)
        packed = rwkv_prepare(_pad_rows(cols, RW_CHUNK), state_rwkv_shift[l].reshape(bs, 1, N_RW_COLS), *rw_pre,
                              RW_CHUNK, RW_CHUNK, ss)
        y_rw, st_rw = rwkv_chunked(packed, state_rwkv[l], *ln, RW_CHUNK, RW_CHUNK)
        flat = lambda a: a.reshape(ns_tok, a.shape[-1])
        xs = outproj_block(xs, flat(y_s5), flat(y_att), flat(y_rw[:, :ss]), sm(1, 2), npost(1), w_out_b, ns_tok, 1)
        xs = ffn_block(xs, sm(2, 0), sm(2, 1), sm(2, 2), npre(2), npost(2), wi1, wo1, ns_tok, 1)
        outs_s.append((k.reshape(bs, ss, N_ATT_HEADS, HEAD_DIM), v.reshape(bs, ss, N_ATT_HEADS, HEAD_DIM),
                       seq(kiw)[:, :, :IDX_DIM],
                       h_s5[:, :S5_W].reshape(bs, S5_GROUPS, S5_STATE),
                       h_s5[:, S5_W:].reshape(bs, S5_GROUPS, S5_STATE), st_rw, cols[:, ss - 1]))

    stack = lambda outs, i: jnp.stack([o[i] for o in outs])
    return ((xp.reshape(bp, sp, d), xs.reshape(bs, ss, d))
            + tuple(stack(outs_p, i) for i in range(7)) + tuple(stack(outs_s, i) for i in range(7)))
```

```python
import functools
import math

import jax
import jax.numpy as jnp
from jax import lax
from jax.experimental import pallas as pl
from jax.experimental.pallas import tpu as pltpu

F32 = jnp.float32
BF16 = jnp.bfloat16
I32 = jnp.int32

D_MODEL = 1024
PAGE_SIZE = 128
D_S5 = 256
D_ATT = 512
D_RW = 256
S5_GROUP = 16
S5_GROUPS = D_S5 // S5_GROUP
S5_STATE = 64
S5_W = S5_GROUPS * S5_STATE
HEAD_DIM = 64
N_ATT_HEADS = D_ATT // HEAD_DIM
N_IDX_HEADS = 4
IDX_DIM = 64
TOPK_MAX = 256
Q_BLOCK = 128
ROPE_THETA = 10000.0
N_RW_HEADS = D_RW // HEAD_DIM
RW_W_RANK = 32
RW_A_RANK = 32
RW_G_RANK = 64
RW_GN_EPS = 64e-5
N_RW_COLS = 3 * D_RW + RW_W_RANK + RW_A_RANK + RW_G_RANK
D_FF = 2816
HALF_STEP = 0.5
RMS_EPS = 1e-6
N_SUB = 3
NEG_INF = -1e30
OFF_S5 = 0
OFF_Q = OFF_S5 + D_S5
OFF_K = OFF_Q + D_ATT
OFF_V = OFF_K + D_ATT
OFF_QI = OFF_V + D_ATT
OFF_KI = OFF_QI + N_IDX_HEADS * IDX_DIM
OFF_WI = OFF_KI + IDX_DIM
OFF_RW = OFF_WI + N_IDX_HEADS
N_IN = OFF_RW + N_RW_COLS

LANES = 128
VMEM_LIMIT = 56 * 1024 * 1024
INT_MIN = -(2 ** 31)

ZP_U, ZP_Q, ZP_K, ZP_V, ZP_QI, ZP_RW, ZP_KIW, ZP_END = 0, 256, 768, 1280, 1792, 2048, 2944, 3072

RW_CHUNK = 64
S5_CHUNK = 128
KEY_CHUNK = 512


def _cparams(sem):
    return pltpu.CompilerParams(dimension_semantics=sem, vmem_limit_bytes=VMEM_LIMIT)


def _dot(a, b):
    return jnp.dot(a, b, preferred_element_type=F32)


def _dot_nt(a, b):
    return lax.dot_general(a, b, (((1,), (1,)), ((), ())), preferred_element_type=F32)


def _split(x):
    hi = x.astype(BF16)
    lo = (x - hi.astype(F32)).astype(BF16)
    return hi, lo


def _dot3(a, b):
    ah, al = _split(a)
    bh, bl = _split(b)
    return _dot(ah, bh) + (_dot(ah, bl) + _dot(al, bh))


def _dot3_nt(a, b):
    ah, al = _split(a)
    bh, bl = _split(b)
    return _dot_nt(ah, bh) + (_dot_nt(ah, bl) + _dot_nt(al, bh))


def _mm3(a, b):
    return _dot(a[0], b[0]) + (_dot(a[0], b[1]) + _dot(a[1], b[0]))


def _mm3_nt(a, b):
    return _dot_nt(a[0], b[0]) + (_dot_nt(a[0], b[1]) + _dot_nt(a[1], b[0]))


def _rms(x, g):
    return x * lax.rsqrt(jnp.mean(x * x, axis=-1, keepdims=True) + RMS_EPS) * g


def _sigmoid(x):
    return 1.0 / (1.0 + jnp.exp(-x))


def _ada_kernel(c_ref, w_ref, b_ref, o_ref):
    c = c_ref[...]
    h = (c * _sigmoid(c)).astype(BF16)
    o_ref[...] = _dot(h, w_ref[...].astype(BF16)) + b_ref[...]


def ada_mod(c_all, ada_w, ada_b, tn=1152):
    depth, d, n = ada_w.shape
    rows = c_all.shape[0]
    return pl.pallas_call(
        _ada_kernel,
        grid=(depth, n // tn),
        in_specs=[pl.BlockSpec((rows, d), lambda l, j: (0, 0)),
                  pl.BlockSpec((None, d, tn), lambda l, j: (l, 0, j)),
                  pl.BlockSpec((None, 1, tn), lambda l, j: (l, 0, j))],
        out_specs=pl.BlockSpec((None, rows, tn), lambda l, j: (l, 0, j)),
        out_shape=jax.ShapeDtypeStruct((depth, rows, n), F32),
        compiler_params=_cparams(("arbitrary", "arbitrary")),
        name="ada_mod",
    )(c_all, ada_w, ada_b.reshape(depth, 1, n))


def _mod_spec(mod, tiles_per_group):
    r = mod.shape[1]
    return pl.BlockSpec((None, r, mod.shape[2]), lambda i: (i // tiles_per_group, 0, 0))


def _ffn_kernel(x_ref, shift_ref, scale_ref, gate_ref, gpre_ref, gpost_ref, wi_ref, wo_ref, o_ref, *, res_w):
    x = x_ref[...]
    h = (_rms(x, gpre_ref[...]) * (1.0 + scale_ref[...]) + shift_ref[...]).astype(BF16)
    g = _dot(h, wi_ref[:, :D_FF])
    u = _dot(h, wi_ref[:, D_FF:])
    a = (g * _sigmoid(g) * u).astype(BF16)
    o = _dot(a, wo_ref[...])
    o_ref[...] = x + (res_w * gate_ref[...]) * _rms(o, gpost_ref[...])


def ffn_block(x, shift, scale, gate, g_pre, g_post, wi, wo, tm, tiles_per_group):
    n, d = x.shape
    row = pl.BlockSpec((tm, d), lambda i: (i, 0))
    const = lambda a: pl.BlockSpec(a.shape, lambda i: (0,) * a.ndim)
    return pl.pallas_call(
        functools.partial(_ffn_kernel, res_w=HALF_STEP),
        grid=(n // tm,),
        in_specs=[row, _mod_spec(shift, tiles_per_group), _mod_spec(scale, tiles_per_group),
                  _mod_spec(gate, tiles_per_group), const(g_pre), const(g_post), const(wi), const(wo)],
        out_specs=row,
        out_shape=jax.ShapeDtypeStruct((n, d), F32),
        compiler_params=_cparams(("arbitrary",)),
        name="ffn_block",
    )(x, shift, scale, gate, g_pre, g_post, wi, wo)


def _rot_block(blk, cos, sin, lane):
    partner = jnp.where((lane & 32) == 0, pltpu.roll(blk, LANES - 32, 1), pltpu.roll(blk, 32, 1))
    return blk * cos + partner * sin


def _inproj_kernel(x_ref, shift_ref, scale_ref, gpre_ref, w_ref, cos_ref, sin_ref,
                   u_ref, q_ref, k_ref, kb_ref, v_ref, vb_ref, qi3_ref, kiw_ref, ki3_ref, rw_ref,
                   vt_ref, ktf_ref, vtf_ref, kiwt_ref, *, q_scale):
    x = x_ref[...]
    h = (_rms(x, gpre_ref[...]) * (1.0 + scale_ref[...]) + shift_ref[...]).astype(BF16)
    z = _dot(h, w_ref[...])
    cos = cos_ref[...]
    sin = sin_ref[...]
    lane = lax.broadcasted_iota(I32, cos.shape, 1)
    lo_half = lane < 64
    rot = lambda off: _rot_block(z[:, off:off + LANES], cos, sin, lane)

    u_ref[...] = z[:, ZP_U:ZP_Q]
    for j in range(D_ATT // LANES):
        q_ref[:, j * LANES:(j + 1) * LANES] = (rot(ZP_Q + j * LANES) * q_scale).astype(BF16)
        kr = rot(ZP_K + j * LANES)
        k_ref[:, j * LANES:(j + 1) * LANES] = kr
        kb_ref[:, j * LANES:(j + 1) * LANES] = kr.astype(BF16)
        ktf_ref[j * LANES:(j + 1) * LANES, :] = kr.T
    v = z[:, ZP_V:ZP_QI]
    v_ref[...] = v
    vb_ref[...] = v.astype(BF16)
    vt = v.T
    vtf_ref[...] = vt
    vt_ref[...] = vt.astype(BF16)
    for j in range(N_IDX_HEADS * IDX_DIM // LANES):
        qr = rot(ZP_QI + j * LANES)
        hi = qr.astype(BF16).astype(F32)
        lo = qr - hi
        hi_sw = pltpu.roll(hi, 64, 1)
        lo_sw = pltpu.roll(lo, 64, 1)
        zero = jnp.zeros_like(hi)
        for half in range(2):
            base = (2 * j + half) * 2 * LANES
            a, b = (hi, lo) if half == 0 else (hi_sw, lo_sw)
            a_sw = hi_sw if half == 0 else hi
            qi3_ref[:, base:base + LANES] = jnp.where(lo_half, a, a_sw).astype(BF16)
            qi3_ref[:, base + LANES:base + 2 * LANES] = jnp.where(lo_half, b, zero).astype(BF16)
    raw = z[:, ZP_KIW:ZP_END]
    kiw = jnp.where(lo_half, _rot_block(raw, cos, sin, lane), raw)
    kiw_ref[...] = kiw
    kiwt_ref[...] = kiw.T
    hi = kiw.astype(BF16).astype(F32)
    lo = kiw - hi
    ki3_ref[:, :LANES] = jnp.where(lo_half, hi, pltpu.roll(lo, 64, 1)).astype(BF16)
    ki3_ref[:, LANES:] = jnp.where(lo_half, hi, jnp.zeros_like(hi)).astype(BF16)
    rw_ref[...] = z[:, ZP_RW:ZP_KIW]


def inproj_block(x, shift, scale, g_pre, w, cos_t, sin_t, tm, tiles_per_group, pos_tiles, q_scale):
    n, d = x.shape
    row = lambda width: pl.BlockSpec((tm, width), lambda i: (i, 0))
    const = lambda a: pl.BlockSpec(a.shape, lambda i: (0,) * a.ndim)
    tab = pl.BlockSpec((tm, LANES), lambda i: (i % pos_tiles, 0))
    widths = [(D_S5, F32), (D_ATT, BF16), (D_ATT, F32), (D_ATT, BF16), (D_ATT, F32), (D_ATT, BF16),
              (N_IDX_HEADS * 2 * LANES, BF16), (LANES, F32), (2 * LANES, BF16), (N_RW_COLS, F32)]
    t_widths = [(D_ATT, BF16), (D_ATT, F32), (D_ATT, F32), (LANES, F32)]
    return pl.pallas_call(
        functools.partial(_inproj_kernel, q_scale=q_scale),
        grid=(n // tm,),
        in_specs=[row(d), _mod_spec(shift, tiles_per_group), _mod_spec(scale, tiles_per_group),
                  const(g_pre), const(w), tab, tab],
        out_specs=[row(wd) for wd, _ in widths]
        + [pl.BlockSpec((None, wd, tm), lambda i: (i // pos_tiles, 0, i % pos_tiles)) for wd, _ in t_widths],
        out_shape=[jax.ShapeDtypeStruct((n, wd), dt) for wd, dt in widths]
        + [jax.ShapeDtypeStruct((n // (pos_tiles * tm), wd, pos_tiles * tm), dt) for wd, dt in t_widths],
        compiler_params=_cparams(("arbitrary",)),
        name="inproj_block",
    )(x, shift, scale, g_pre, w, cos_t, sin_t)


def _outproj_kernel(x_ref, ys5_ref, yatt_ref, yrw_ref, gate_ref, gpost_ref, w_ref, o_ref):
    o = (_dot(ys5_ref[...].astype(BF16), w_ref[:D_S5, :])
         + _dot(yatt_ref[...].astype(BF16), w_ref[D_S5:D_S5 + D_ATT, :])
         + _dot(yrw_ref[...].astype(BF16), w_ref[D_S5 + D_ATT:, :]))
    o_ref[...] = x_ref[...] + gate_ref[...] * _rms(o, gpost_ref[...])


def outproj_block(x, y_s5, y_att, y_rw, gate, g_post, w, tm, tiles_per_group):
    n, d = x.shape
    row = lambda width: pl.BlockSpec((tm, width), lambda i: (i, 0))
    const = lambda a: pl.BlockSpec(a.shape, lambda i: (0,) * a.ndim)
    return pl.pallas_call(
        _outproj_kernel,
        grid=(n // tm,),
        in_specs=[row(d), row(D_S5), row(D_ATT), row(D_RW), _mod_spec(gate, tiles_per_group),
                  const(g_post), const(w)],
        out_specs=row(d),
        out_shape=jax.ShapeDtypeStruct((n, d), F32),
        compiler_params=_cparams(("arbitrary",)),
        name="outproj_block",
    )(x, y_s5, y_att, y_rw, gate, g_post, w)


def _gelu_tanh(x):
    return 0.5 * x * (1.0 + jnp.tanh(math.sqrt(2.0 / math.pi) * (x + 0.044715 * (x * x * x))))


def _s5_disc_kernel(are_ref, aim_ref, ldt_ref, arec_ref, aimc_ref, ldtc_ref, bre_ref, bim_ref,
                    powre_ref, powim_ref, bbre_ref, bbim_ref, *, rows):
    def zoh(ar, ai, ldt):
        dt = jnp.exp(ldt)
        mag = jnp.exp(ar * dt)
        abr, abi = mag * jnp.cos(ai * dt), mag * jnp.sin(ai * dt)
        den = ar * ar + ai * ai
        nr, ni = abr - 1.0, abi
        return abr, abi, (nr * ar + ni * ai) / den, (ni * ar - nr * ai) / den

    abr, abi, _, _ = zoh(are_ref[...], aim_ref[...], ldt_ref[...])
    pr = jnp.broadcast_to(abr, (rows, abr.shape[1]))
    pi = jnp.broadcast_to(abi, (rows, abr.shape[1]))
    row = lax.broadcasted_iota(I32, pr.shape, 0)
    d = 1
    while d < rows:
        sr = pltpu.roll(pr, d, 0)
        si = pltpu.roll(pi, d, 0)
        m = row >= d
        pr, pi = jnp.where(m, pr * sr - pi * si, pr), jnp.where(m, pr * si + pi * sr, pi)
        d *= 2
    powre_ref[...] = pr
    powim_ref[...] = pi
    _, _, cr, ci = zoh(arec_ref[...], aimc_ref[...], ldtc_ref[...])
    br, bi = bre_ref[...], bim_ref[...]
    bbre_ref[...] = cr * br - ci * bi
    bbim_ref[...] = cr * bi + ci * br


def s5_discretise(a_re, a_im, log_dt, b_re, b_im, rows):
    g, p = a_re.shape
    w = g * p
    ldt = jnp.broadcast_to(log_dt[:, None], (g, p))
    args = (a_re.reshape(1, w), a_im.reshape(1, w), ldt.reshape(1, w),
            a_re.reshape(w, 1), a_im.reshape(w, 1), ldt.reshape(w, 1),
            b_re.reshape(w, S5_GROUP), b_im.reshape(w, S5_GROUP))
    return pl.pallas_call(
        functools.partial(_s5_disc_kernel, rows=rows),
        out_shape=[jax.ShapeDtypeStruct((rows, w), F32), jax.ShapeDtypeStruct((rows, w), F32),
                   jax.ShapeDtypeStruct((w, S5_GROUP), F32), jax.ShapeDtypeStruct((w, S5_GROUP), F32)],
        name="s5_discretise",
    )(*args)


def _s5_head(y, u, d_ref, gluw_ref, glub_ref):
    y = _gelu_tanh(y + d_ref[...] * u)
    return y * _sigmoid(_dot(y.astype(BF16), gluw_ref[...]) + glub_ref[...])


def _s5_kernel(u_ref, h0_ref, powre_ref, powim_ref, wb_ref, wc_ref, d_ref, gluw_ref, glub_ref,
               y_ref, ht_ref, cre, cim):
    t = pl.program_id(1)
    rows = u_ref.shape[0]

    @pl.when(t == 0)
    def _():
        cre[...] = h0_ref[:, :S5_W]
        cim[...] = h0_ref[:, S5_W:]

    u = u_ref[...]
    bu = _dot(u.astype(BF16), wb_ref[...])
    hr, hi = bu[:, :S5_W], bu[:, S5_W:]
    row = lax.broadcasted_iota(I32, hr.shape, 0)
    d = 1
    while d < rows:
        ar, ai = powre_ref[d - 1:d, :], powim_ref[d - 1:d, :]
        sr, si = pltpu.roll(hr, d, 0), pltpu.roll(hi, d, 0)
        m = row >= d
        hr, hi = (hr + jnp.where(m, ar * sr - ai * si, 0.0), hi + jnp.where(m, ar * si + ai * sr, 0.0))
        d *= 2
    pr, pi = powre_ref[...], powim_ref[...]
    c_r, c_i = cre[...], cim[...]
    hr, hi = hr + (pr * c_r - pi * c_i), hi + (pr * c_i + pi * c_r)
    cre[...] = hr[rows - 1:rows, :]
    cim[...] = hi[rows - 1:rows, :]
    ht_ref[:, :S5_W] = hr[rows - 1:rows, :]
    ht_ref[:, S5_W:] = hi[rows - 1:rows, :]
    y = _dot(hr.astype(BF16), wc_ref[:S5_W, :]) + _dot(hi.astype(BF16), wc_ref[S5_W:, :])
    y_ref[...] = _s5_head(y, u, d_ref, gluw_ref, glub_ref)


def s5_prompt(u, h0, powre, powim, wb, wc, d, gluw, glub):
    b, s, _ = u.shape
    rows = powre.shape[0]
    const = lambda a: pl.BlockSpec(a.shape, lambda i, j: (0,) * a.ndim)
    return pl.pallas_call(
        _s5_kernel,
        grid=(b, s // rows),
        in_specs=[pl.BlockSpec((None, rows, D_S5), lambda i, j: (i, j, 0)),
                  pl.BlockSpec((None, 1, 2 * S5_W), lambda i, j: (i, 0, 0)),
                  const(powre), const(powim), const(wb), const(wc), const(d), const(gluw), const(glub)],
        out_specs=[pl.BlockSpec((None, rows, D_S5), lambda i, j: (i, j, 0)),
                   pl.BlockSpec((None, 1, 2 * S5_W), lambda i, j: (i, 0, 0))],
        out_shape=[jax.ShapeDtypeStruct((b, s, D_S5), F32), jax.ShapeDtypeStruct((b, 1, 2 * S5_W), F32)],
        scratch_shapes=[pltpu.VMEM((1, S5_W), F32), pltpu.VMEM((1, S5_W), F32)],
        compiler_params=_cparams(("arbitrary", "arbitrary")),
        name="s5_prompt",
    )(u, h0, powre, powim, wb, wc, d, gluw, glub)


def _s5_step_kernel(u_ref, h0_ref, powre_ref, powim_ref, wb_ref, wc_ref, d_ref, gluw_ref, glub_ref,
                    y_ref, ht_ref):
    ar, ai = powre_ref[0:1, :], powim_ref[0:1, :]
    hr, hi = h0_ref[:, :S5_W], h0_ref[:, S5_W:]
    for t in range(u_ref.shape[0]):
        u = u_ref[t]
        bu = _dot(u.astype(BF16), wb_ref[...])
        hr, hi = ar * hr - ai * hi + bu[:, :S5_W], ar * hi + ai * hr + bu[:, S5_W:]
        y = _dot(hr.astype(BF16), wc_ref[:S5_W, :]) + _dot(hi.astype(BF16), wc_ref[S5_W:, :])
        y_ref[t] = _s5_head(y, u, d_ref, gluw_ref, glub_ref)
    ht_ref[:, :S5_W] = hr
    ht_ref[:, S5_W:] = hi


def s5_sample(u_tm, h0, powre, powim, wb, wc, d, gluw, glub):
    s, b, _ = u_tm.shape
    return pl.pallas_call(
        _s5_step_kernel,
        out_shape=[jax.ShapeDtypeStruct((s, b, D_S5), F32), jax.ShapeDtypeStruct((b, 2 * S5_W), F32)],
        compiler_params=pltpu.CompilerParams(vmem_limit_bytes=VMEM_LIMIT),
        name="s5_sample",
    )(u_tm, h0, powre, powim, wb, wc, d, gluw, glub)


def s5_matrices(bb_re, bb_im, c_re, c_im):
    g, p, h = S5_GROUPS, S5_STATE, S5_GROUP
    eye = jnp.eye(g, dtype=F32)
    bd_in = lambda bb: jnp.einsum('gph,gk->ghkp', bb.reshape(g, p, h), eye).reshape(g * h, g * p)
    bd_out = lambda c: jnp.einsum('ghp,gk->gpkh', c, eye).reshape(g * p, g * h)
    wb = jnp.concatenate([bd_in(bb_re), bd_in(bb_im)], axis=1).astype(BF16)
    wc = jnp.concatenate([bd_out(c_re), -bd_out(c_im)], axis=0).astype(BF16)
    return wb, wc


RW_PACK = 8 * D_RW


def _head_ones(n):
    r = lax.broadcasted_iota(I32, (n, n), 0) // HEAD_DIM
    c = lax.broadcasted_iota(I32, (n, n), 1) // HEAD_DIM
    return jnp.where(r == c, 1.0, 0.0).astype(BF16)


def _seg_sum(x, ones_bd):
    hi, lo = _split(x)
    return _dot(hi, ones_bd) + _dot(lo, ones_bd)


def _softplus(x):
    return jnp.maximum(x, 0.0) + jnp.log(1.0 + jnp.exp(-jnp.abs(x)))


def _rw_pre_kernel(cols_ref, shift0_ref, mu_ref, w0_ref, a0_ref, wlr_ref, kk_ref, ka_ref, rk_ref,
                   o_ref, carry, *, chunk, valid):
    @pl.when(pl.program_id(1) == 0)
    def _():
        carry[...] = shift0_ref[...]

    cf = cols_ref[...]
    tm = cf.shape[0]
    row = lax.broadcasted_iota(I32, (tm, 1), 0)
    prev = jnp.where(row == 0, carry[...], pltpu.roll(cf, 1, 0))
    carry[...] = cf[tm - 1:tm, :]
    xs = cf + (prev - cf) * mu_ref[...]
    r, k, v = xs[:, :D_RW], xs[:, D_RW:2 * D_RW], xs[:, 2 * D_RW:3 * D_RW]
    lr = xs[:, 3 * D_RW:]
    lane = lax.broadcasted_iota(I32, lr.shape, 1)
    t = jnp.where(lane < RW_W_RANK, jnp.tanh(lr), jnp.where(lane < RW_W_RANK + RW_A_RANK, lr, _sigmoid(lr)))
    proj = _dot3(t, wlr_ref[...])
    w = -_softplus(-(w0_ref[...] + proj[:, :D_RW])) - 0.5
    logw = -jnp.exp(w)
    a = _sigmoid(a0_ref[...] + proj[:, D_RW:2 * D_RW])
    g = proj[:, 2 * D_RW:]
    ones_bd = _head_ones(D_RW)
    kk = k * kk_ref[...]
    kk = kk / jnp.maximum(jnp.sqrt(_seg_sum(kk * kk, ones_bd)), 1e-12)
    km = k * (1.0 + (a - 1.0) * ka_ref[...])
    bonus = _seg_sum(r * km * rk_ref[...], ones_bd) * v
    pos = row % chunk
    if valid < chunk:
        live = pos < valid
        zero = jnp.zeros_like(r)
        logw, kk, km, v, r = (jnp.where(live, logw, zero), jnp.where(live, kk, zero), jnp.where(live, km, zero),
                              jnp.where(live, v, zero), jnp.where(live, r, zero))
    gc = logw
    d = 1
    while d < chunk:
        gc = gc + jnp.where(pos >= d, pltpu.roll(gc, d, 0), 0.0)
        d *= 2
    eg = jnp.exp(gc)
    eng = jnp.exp(-gc)
    o_ref[:, 0 * D_RW:1 * D_RW] = r * eg
    o_ref[:, 1 * D_RW:2 * D_RW] = kk * jnp.exp(gc - logw)
    o_ref[:, 2 * D_RW:3 * D_RW] = kk * a * eng
    o_ref[:, 3 * D_RW:4 * D_RW] = km * eng
    o_ref[:, 4 * D_RW:5 * D_RW] = v
    o_ref[:, 5 * D_RW:6 * D_RW] = eg
    o_ref[:, 6 * D_RW:7 * D_RW] = bonus
    o_ref[:, 7 * D_RW:8 * D_RW] = g


def rwkv_prepare(cols, shift0, mu, w0, a0, wlr, k_k, k_a, r_k, tm, chunk, valid):
    b, s, c = cols.shape
    const = lambda a: pl.BlockSpec(a.shape, lambda i, j: (0,) * a.ndim)
    return pl.pallas_call(
        functools.partial(_rw_pre_kernel, chunk=chunk, valid=valid),
        grid=(b, s // tm),
        in_specs=[pl.BlockSpec((None, tm, c), lambda i, j: (i, j, 0)),
                  pl.BlockSpec((None, 1, c), lambda i, j: (i, 0, 0)),
                  const(mu), const(w0), const(a0), const(wlr), const(k_k), const(k_a), const(r_k)],
        out_specs=pl.BlockSpec((None, tm, RW_PACK), lambda i, j: (i, j, 0)),
        out_shape=jax.ShapeDtypeStruct((b, s, RW_PACK), F32),
        scratch_shapes=[pltpu.VMEM((1, c), F32)],
        compiler_params=_cparams(("arbitrary", "arbitrary")),
        name="rwkv_prepare",
    )(cols, shift0, mu, w0, a0, wlr, k_k, k_a, r_k)


def _rw_chunk_kernel(x_ref, s0_ref, lnw_ref, lnb_ref, y_ref, st_ref, s_scr, y_scr, *, chunk):
    @pl.when(pl.program_id(1) == 0)
    def _():
        s_scr[...] = s0_ref[...]

    rows = x_ref.shape[0]
    hd = HEAD_DIM
    ri = lax.broadcasted_iota(I32, (chunk, chunk), 0)
    ci = lax.broadcasted_iota(I32, (chunk, chunk), 1)
    strict = ri > ci
    incl = ri >= ci
    eye = jnp.where(ri == ci, 1.0, 0.0)
    ek = lax.broadcasted_iota(I32, (hd, hd), 0) == lax.broadcasted_iota(I32, (hd, hd), 1)
    eye_k = jnp.where(ek, 1.0, 0.0)
    n_sq = chunk.bit_length() - 2
    n_chunks = rows // chunk
    probs = [(c, h) for c in range(n_chunks) for h in range(N_RW_HEADS)]
    each = lambda f: [f(i) for i in range(len(probs))]

    def col(i, j):
        c, h = probs[i]
        return x_ref[c * chunk:(c + 1) * chunk, j * D_RW + h * hd:j * D_RW + (h + 1) * hd]

    rt, kt, bt, km, v = (each(lambda i: col(i, j)) for j in range(5))
    lhs = each(lambda i: _split(jnp.concatenate([kt[i], rt[i]], axis=0)))
    bts, kms, vs = each(lambda i: _split(bt[i])), each(lambda i: _split(km[i])), each(lambda i: _split(v[i]))
    gb = each(lambda i: _mm3_nt(lhs[i], bts[i]))
    gk = each(lambda i: _mm3_nt(lhs[i], kms[i]))
    a_bb = each(lambda i: jnp.where(strict, gb[i][:chunk], 0.0))
    a_rb = each(lambda i: _split(jnp.where(incl, gb[i][chunk:], 0.0)))
    a_kr = each(lambda i: _split(jnp.concatenate([jnp.where(strict, gk[i][:chunk], 0.0),
                                                  jnp.where(incl, gk[i][chunk:], 0.0)], axis=0)))
    av = each(lambda i: _mm3(a_kr[i], vs[i]))
    minv = each(lambda i: eye - a_bb[i])
    p = a_bb
    for _ in range(n_sq):
        ps = each(lambda i: _split(p[i]))
        p = each(lambda i: _mm3(ps[i], ps[i]))
        minv = each(lambda i: minv[i] + _mm3(_split(minv[i]), _split(p[i])))
    minvs = each(lambda i: _split(minv[i]))
    khat = each(lambda i: _mm3(minvs[i], _split(kt[i])))
    p1 = each(lambda i: _mm3(minvs[i], _split(av[i][:chunk])))
    rhat = each(lambda i: rt[i] - _mm3(a_rb[i], _split(khat[i])))
    y1 = each(lambda i: av[i][chunk:] - _mm3(a_rb[i], _split(p1[i])))
    tb = each(lambda i: _mm3(_split(jnp.concatenate([khat[i].T, p1[i].T], axis=0)), bts[i]))
    vk = each(lambda i: _mm3(_split(v[i].T), kms[i]))

    def eg_last(i):
        c, h = probs[i]
        return x_ref[(c + 1) * chunk - 1:(c + 1) * chunk, 5 * D_RW + h * hd:5 * D_RW + (h + 1) * hd]

    gmat = each(lambda i: (eye_k - tb[i][:hd]) * eg_last(i))
    umat = each(lambda i: (vk[i] - tb[i][hd:]) * eg_last(i))
    state = [s_scr[h] for h in range(N_RW_HEADS)]
    for i, (c, h) in enumerate(probs):
        y_scr[c * chunk:(c + 1) * chunk, h * hd:(h + 1) * hd] = _dot3_nt(rhat[i], state[h]) + y1[i]
        state[h] = _dot3(state[h], gmat[i]) + umat[i]
    for h in range(N_RW_HEADS):
        s_scr[h] = state[h]
    st_ref[...] = s_scr[...]
    y = y_scr[...]
    ones_bd = _head_ones(D_RW)
    mean = _seg_sum(y, ones_bd) * (1.0 / hd)
    yc = y - mean
    var = _seg_sum(yc * yc, ones_bd) * (1.0 / hd)
    yn = yc * lax.rsqrt(var + RW_GN_EPS) * lnw_ref[...] + lnb_ref[...]
    y_ref[...] = (yn + x_ref[:, 6 * D_RW:7 * D_RW]) * x_ref[:, 7 * D_RW:8 * D_RW]


def rwkv_chunked(packed, state0, ln_w, ln_b, rows, chunk):
    b, s, _ = packed.shape
    const = lambda a: pl.BlockSpec(a.shape, lambda i, j: (0,) * a.ndim)
    st_spec = pl.BlockSpec((None, N_RW_HEADS, HEAD_DIM, HEAD_DIM), lambda i, j: (i, 0, 0, 0))
    return pl.pallas_call(
        functools.partial(_rw_chunk_kernel, chunk=chunk),
        grid=(b, s // rows),
        in_specs=[pl.BlockSpec((None, rows, RW_PACK), lambda i, j: (i, j, 0)), st_spec, const(ln_w), const(ln_b)],
        out_specs=[pl.BlockSpec((None, rows, D_RW), lambda i, j: (i, j, 0)), st_spec],
        out_shape=[jax.ShapeDtypeStruct((b, s, D_RW), F32),
                   jax.ShapeDtypeStruct((b, N_RW_HEADS, HEAD_DIM, HEAD_DIM), F32)],
        scratch_shapes=[pltpu.VMEM((N_RW_HEADS, HEAD_DIM, HEAD_DIM), F32), pltpu.VMEM((rows, D_RW), F32)],
        compiler_params=_cparams(("arbitrary", "arbitrary")),
        name="rwkv_chunked",
    )(packed, state0, ln_w, ln_b)


def rwkv_lowrank_matrix(w2, a2, g2):
    z = lambda r: jnp.zeros((r, D_RW), F32)
    return jnp.concatenate([
        jnp.concatenate([w2, z(RW_W_RANK), z(RW_W_RANK)], axis=1),
        jnp.concatenate([z(RW_A_RANK), a2, z(RW_A_RANK)], axis=1),
        jnp.concatenate([z(RW_G_RANK), z(RW_G_RANK), g2], axis=1)], axis=0)


def _float_key(x):
    b = lax.bitcast_convert_type(x, I32)
    return jnp.where(b < 0, jnp.int32(INT_MIN) - b, b)


def _py_key(v):
    import numpy as np
    b = int(np.float32(v).view(np.int32))
    return -(b & 0x7FFFFFFF) if b < 0 else b


KEY_NEG_INF = _py_key(NEG_INF)


def _count_ge(sc_ref, nc, cand):
    parts = [jnp.where(sc_ref[c] >= cand, 1.0, 0.0) for c in range(nc)]
    while len(parts) > 1:
        parts = [parts[i] + parts[i + 1] for i in range(0, len(parts) - 1, 2)] + (parts[-1:] if len(parts) % 2 else [])
    return jnp.sum(parts[0], axis=1, keepdims=True)


def _select_threshold(sc_ref, nc, k):
    _, r, wc = sc_ref.shape

    def bit_body(i, carry):
        t, cge = carry
        cand = t + jnp.left_shift(jnp.int32(1), 31 - i)
        cnt = _count_ge(sc_ref, nc, cand)
        ok = cnt >= k
        return jnp.where(ok, cand, t), jnp.where(ok, cnt, cge)

    t0 = jnp.full((r, 1), INT_MIN, I32)
    cge0 = jnp.zeros((r, 1), F32) + jnp.asarray(nc * wc, F32)
    return lax.fori_loop(0, 32, bit_body, (t0, cge0))


def _resolve_ties(sc_ref, nc, t, need):
    _, r, wc = sc_ref.shape
    upper = jnp.where(lax.broadcasted_iota(I32, (wc, wc), 0) < lax.broadcasted_iota(I32, (wc, wc), 1), 1.0, 0.0)

    def body(c, offs):
        x = sc_ref[c]
        e = x == t
        ef = jnp.where(e, 1.0, 0.0)
        rank = _dot(ef, upper) + offs
        sc_ref[c] = jnp.where(e, jnp.where(rank >= need, jnp.int32(INT_MIN), x), x)
        return offs + jnp.sum(ef, axis=1, keepdims=True)

    lax.fori_loop(0, nc, body, jnp.zeros((r, 1), F32))


def _topk_mask_prepare(sc_ref, nc, k, valid_rows):
    t, cge = _select_threshold(sc_ref, nc, k)
    r = t.shape[0]
    live = lax.broadcasted_iota(I32, (r, 1), 0) < valid_rows
    tied = live & (cge > k) & (t != KEY_NEG_INF)

    @pl.when(jnp.max(jnp.where(tied, 1.0, 0.0)) > 0.0)
    def _():
        cgt = _count_ge(sc_ref, nc, t + 1)
        _resolve_ties(sc_ref, nc, t, k - cgt)

    return t


def _indexer_total(s4, wcols, rows):
    tot = jnp.maximum(s4[:rows], 0.0) * wcols[0]
    for h in range(1, N_IDX_HEADS):
        tot = tot + jnp.maximum(s4[h * rows:(h + 1) * rows], 0.0) * wcols[h]
    return tot


SUBLANES = 8


def _col_tree(x, op, rows=SUBLANES):
    parts = [x[i:i + rows] for i in range(0, x.shape[0], rows)]
    while len(parts) > 1:
        parts = [op(parts[i], parts[i + 1]) for i in range(0, len(parts) - 1, 2)] + (parts[-1:] if len(parts) % 2 else [])
    return parts[0]


def _col_sum(x):
    return _col_tree(x, jnp.add)


def _col_max(x):
    return _col_tree(x, jnp.maximum)


def _count_ge_t(sc_ref, nc, cand):
    _, wc, nq = sc_ref.shape
    hits = lambda c: _col_sum(jnp.where(sc_ref[c] >= cand, 1.0, 0.0))

    def pair_body(i, acc):
        return acc + (hits(2 * i) + hits(2 * i + 1))

    acc = lax.fori_loop(0, nc // 2, pair_body, jnp.zeros((SUBLANES, nq), F32))
    acc = lax.cond(nc % 2 == 1, lambda a: a + hits(nc - 1), lambda a: a, acc)
    return jnp.sum(acc, axis=0, keepdims=True)


def _topk_mask_prepare_t(sc_ref, nc, k):
    _, wc, nq = sc_ref.shape

    def bit_body(i, carry):
        t, cge = carry
        cand = t + jnp.left_shift(jnp.int32(1), 31 - i)
        cnt = _count_ge_t(sc_ref, nc, cand)
        ok = cnt >= k
        return jnp.where(ok, cand, t), jnp.where(ok, cnt, cge)

    t0 = jnp.full((1, nq), INT_MIN, I32)
    cge0 = jnp.zeros((1, nq), F32) + jnp.asarray(nc * wc, F32)
    t, cge = lax.fori_loop(0, 32, bit_body, (t0, cge0))
    tied = (cge > k) & (t != KEY_NEG_INF)

    @pl.when(jnp.max(jnp.where(tied, 1.0, 0.0)) > 0.0)
    def _():
        need = k - _count_ge_t(sc_ref, nc, t + 1)
        ri = lax.broadcasted_iota(I32, (wc, wc), 0)
        ci = lax.broadcasted_iota(I32, (wc, wc), 1)
        lower = jnp.where(ci < ri, 1.0, 0.0)

        def body(c, offs):
            x = sc_ref[c]
            e = x == t
            ef = jnp.where(e, 1.0, 0.0)
            rank = _dot(lower, ef) + offs
            sc_ref[c] = jnp.where(e, jnp.where(rank >= need, jnp.int32(INT_MIN), x), x)
            return offs + jnp.sum(ef, axis=0, keepdims=True)

        lax.fori_loop(0, nc, body, jnp.zeros((1, nq), F32))

    return t


def _dsa_prompt_kernel(q_ref, kb_ref, vt_ref, qi3_ref, ki3_ref, kiw_ref, o_ref, sc, *, k_sel):
    j = pl.program_id(1)
    qb, wc = Q_BLOCK, KEY_CHUNK
    nck = (j * qb + qb + wc - 1) // wc
    qpos = j * qb + lax.broadcasted_iota(I32, (1, qb), 1)
    keybase = lax.broadcasted_iota(I32, (wc, 1), 0)

    lhs = jnp.concatenate([qi3_ref[:, 2 * LANES * h:2 * LANES * (h + 1)] for h in range(N_IDX_HEADS)], axis=0)
    wrows = [kiw_ref[IDX_DIM + h:IDX_DIM + h + 1, :] for h in range(N_IDX_HEADS)]

    def score_chunk(c, diagonal):
        kc = ki3_ref[pl.ds(pl.multiple_of(c * wc, wc), wc), :]
        s4 = _dot_nt(kc, lhs)
        tot = jnp.maximum(s4[:, :qb], 0.0) * wrows[0]
        for h in range(1, N_IDX_HEADS):
            tot = tot + jnp.maximum(s4[:, h * qb:(h + 1) * qb], 0.0) * wrows[h]
        if diagonal:
            tot = jnp.where(c * wc + keybase <= qpos, tot, NEG_INF)
        sc[c] = _float_key(tot)

    def score_body(c, carry):
        score_chunk(c, False)
        return carry

    lax.fori_loop(0, nck - 1, score_body, 0)
    score_chunk(nck - 1, True)
    t = _topk_mask_prepare_t(sc, nck, k_sel)
    t_sel = jnp.maximum(t, KEY_NEG_INF + 1)

    lane = lax.broadcasted_iota(I32, (qb, LANES), 1)
    n_pairs = N_ATT_HEADS // 2
    qpair = []
    for pr in range(n_pairs):
        blk = q_ref[:, pr * LANES:(pr + 1) * LANES]
        zero = jnp.zeros_like(blk)
        qpair.append(jnp.concatenate([jnp.where(lane < HEAD_DIM, blk, zero), jnp.where(lane < HEAD_DIM, zero, blk)],
                                     axis=0))

    def att_body(c, carry):
        ms, ls, accs = carry
        off = pl.multiple_of(c * wc, wc)
        bias = jnp.where(sc[c] >= t_sel, 0.0, NEG_INF)
        bias2 = jnp.concatenate([bias, bias], axis=1)
        ss = [_dot_nt(kb_ref[pl.ds(off, wc), pr * LANES:(pr + 1) * LANES], qpair[pr]) + bias2
              for pr in range(n_pairs)]
        new_m = [jnp.maximum(ms[pr], jnp.max(_col_max(ss[pr]), axis=0, keepdims=True)) for pr in range(n_pairs)]
        alphas = [jnp.exp2(ms[pr] - new_m[pr]) for pr in range(n_pairs)]
        ps = [jnp.exp2(ss[pr] - new_m[pr]) for pr in range(n_pairs)]
        new_l = [alphas[pr] * ls[pr] + jnp.sum(_col_sum(ps[pr]), axis=0, keepdims=True) for pr in range(n_pairs)]
        new_acc = [alphas[pr] * accs[pr]
                   + _dot(vt_ref[pr * LANES:(pr + 1) * LANES, pl.ds(off, wc)], ps[pr].astype(BF16))
                   for pr in range(n_pairs)]
        return tuple(new_m), tuple(new_l), tuple(new_acc)

    init = (tuple(jnp.full((1, 2 * qb), NEG_INF, F32) for _ in range(n_pairs)),
            tuple(jnp.zeros((1, 2 * qb), F32) for _ in range(n_pairs)),
            tuple(jnp.zeros((LANES, 2 * qb), F32) for _ in range(n_pairs)))
    _, ls, accs = lax.fori_loop(0, nck, att_body, init)
    sub = lax.broadcasted_iota(I32, (LANES, qb), 0)
    for pr in range(n_pairs):
        o2 = accs[pr] / ls[pr]
        o_ref[:, pr * LANES:(pr + 1) * LANES] = jnp.where(sub < HEAD_DIM, o2[:, :qb], o2[:, qb:]).T


def dsa_prompt(q, kb, vt, qi3, ki3, kiw_t):
    b, s, _ = q.shape
    k_sel = min(TOPK_MAX, s // 4)
    blk = lambda w: pl.BlockSpec((None, Q_BLOCK, w), lambda i, j: (i, j, 0))
    full = lambda w: pl.BlockSpec((None, s, w), lambda i, j: (i, 0, 0))
    return pl.pallas_call(
        functools.partial(_dsa_prompt_kernel, k_sel=k_sel),
        grid=(b, s // Q_BLOCK),
        in_specs=[blk(D_ATT), full(D_ATT), pl.BlockSpec((None, D_ATT, s), lambda i, j: (i, 0, 0)),
                  blk(N_IDX_HEADS * 2 * LANES), full(2 * LANES),
                  pl.BlockSpec((None, LANES, Q_BLOCK), lambda i, j: (i, 0, j))],
        out_specs=blk(D_ATT),
        out_shape=jax.ShapeDtypeStruct((b, s, D_ATT), F32),
        scratch_shapes=[pltpu.VMEM((s // KEY_CHUNK, KEY_CHUNK, Q_BLOCK), I32)],
        compiler_params=_cparams(("arbitrary", "arbitrary")),
        name="dsa_prompt",
    )(q, kb, vt, qi3, ki3, kiw_t)


SAMPLE_ROWS = 8
PAGE_GROUP = 16
IDX_PAGE_GROUP = 32


def _dsa_sample_index_kernel(pt_ref, *refs, n_valid, k_sel):
    pages = refs[:IDX_PAGE_GROUP]
    qi3_ref, kiw_ref, ki3n_ref, sc_ref, thr_ref = refs[IDX_PAGE_GROUP:]
    g = pl.program_id(1)
    r = SAMPLE_ROWS
    n_chunks = sc_ref.shape[0]
    q3 = qi3_ref[...].astype(F32)
    qhi = jnp.concatenate([q3[:, 2 * LANES * h:2 * LANES * h + IDX_DIM] for h in range(N_IDX_HEADS)],
                          axis=0).astype(BF16)
    qlo = jnp.concatenate([q3[:, 2 * LANES * h + LANES:2 * LANES * h + LANES + IDX_DIM]
                           for h in range(N_IDX_HEADS)], axis=0).astype(BF16)
    wcols = [kiw_ref[:, IDX_DIM + h:IDX_DIM + h + 1] for h in range(N_IDX_HEADS)]

    def scores(kt):
        khi, klo = _split(kt)
        return _dot(qhi, khi) + (_dot(qhi, klo) + _dot(qlo, khi))

    for i in range(IDX_PAGE_GROUP):
        sc_ref[g * IDX_PAGE_GROUP + i] = _float_key(_indexer_total(scores(pages[i][...]), wcols, r))

    @pl.when(g == pl.num_programs(1) - 1)
    def _():
        tot = _indexer_total(scores(ki3n_ref[...]), wcols, r)
        row = lax.broadcasted_iota(I32, tot.shape, 0)
        col = lax.broadcasted_iota(I32, tot.shape, 1)
        tot = jnp.where((col <= row) & (col < n_valid), tot, NEG_INF)
        sc_ref[n_chunks - 1] = _float_key(tot)
        t = _topk_mask_prepare(sc_ref, n_chunks, k_sel, n_valid)
        thr_ref[...] = jnp.broadcast_to(t, thr_ref.shape)


def dsa_sample_index(page_table, cache_kidx_t, layer, qi3, kiw, ki_new_t, n_valid):
    b, n_pages = page_table.shape
    n_chunks = n_pages + 1
    k_sel = min(TOPK_MAX, (n_pages * PAGE_SIZE + n_valid) // 4)
    page_spec = lambda i: pl.BlockSpec((None, None, IDX_DIM, PAGE_SIZE),
                                       lambda bi, g, pt: (layer, pt[bi, g * IDX_PAGE_GROUP + i], 0, 0))
    per_b = lambda shape: pl.BlockSpec((None,) + shape, lambda bi, g, pt: (bi,) + (0,) * len(shape))
    r = SAMPLE_ROWS
    return pl.pallas_call(
        functools.partial(_dsa_sample_index_kernel, n_valid=n_valid, k_sel=k_sel),
        grid_spec=pltpu.PrefetchScalarGridSpec(
            num_scalar_prefetch=1,
            grid=(b, n_pages // IDX_PAGE_GROUP),
            in_specs=[page_spec(i) for i in range(IDX_PAGE_GROUP)]
            + [per_b((r, N_IDX_HEADS * 2 * LANES)), per_b((r, LANES)), per_b((IDX_DIM, PAGE_SIZE))],
            out_specs=[per_b((n_chunks, r, PAGE_SIZE)), per_b((r, PAGE_SIZE))],
        ),
        out_shape=[jax.ShapeDtypeStruct((b, n_chunks, r, PAGE_SIZE), I32),
                   jax.ShapeDtypeStruct((b, r, PAGE_SIZE), I32)],
        compiler_params=_cparams(("arbitrary", "arbitrary")),
        name="dsa_sample_index",
    )(page_table, *([cache_kidx_t] * IDX_PAGE_GROUP), qi3, kiw, ki_new_t)


def _dsa_sample_attend_kernel(pt_ref, *refs, n_valid):
    kpages = refs[:PAGE_GROUP]
    vpages = refs[PAGE_GROUP:2 * PAGE_GROUP]
    qbd_ref, sel_ref, seln_ref, thr_ref, kn_ref, vn_ref, o_ref, m_scr, l_scr, acc_scr = refs[2 * PAGE_GROUP:]
    g = pl.program_id(1)
    r = SAMPLE_ROWS

    @pl.when(g == 0)
    def _():
        m_scr[...] = jnp.full(m_scr.shape, NEG_INF, F32)
        l_scr[...] = jnp.zeros(l_scr.shape, F32)
        acc_scr[...] = jnp.zeros(acc_scr.shape, F32)

    qbd = qbd_ref[...]
    thr = thr_ref[...]
    per_head = lambda x: jnp.concatenate([x] * N_ATT_HEADS, axis=0)
    thr_all = per_head(thr)

    def update(sels, extra, kts, vts):
        n = len(kts)
        scores = jnp.concatenate([_dot(qbd, kt) for kt in kts], axis=1)
        mask = jnp.concatenate([per_head(sel) for sel in sels], axis=1) >= jnp.concatenate([thr_all] * n, axis=1)
        if extra is not None:
            mask = mask & (per_head(extra) > 0)
        s = jnp.where(mask, scores, NEG_INF)
        m_prev = m_scr[...]
        m_new = jnp.maximum(m_prev, jnp.max(s, axis=1, keepdims=True))
        alpha = jnp.exp(m_prev - m_new)
        p = jnp.exp(s - m_new)
        l_scr[...] = alpha * l_scr[...] + jnp.sum(p, axis=1, keepdims=True)
        p = p.astype(BF16)
        pv = _dot_nt(p[:, :PAGE_SIZE], vts[0])
        for i in range(1, n):
            pv = pv + _dot_nt(p[:, i * PAGE_SIZE:(i + 1) * PAGE_SIZE], vts[i])
        acc_scr[...] = alpha * acc_scr[...] + pv
        m_scr[...] = m_new

    update([sel_ref[i] for i in range(PAGE_GROUP)], None,
           [kpages[i][...].astype(BF16) for i in range(PAGE_GROUP)],
           [vpages[i][...].astype(BF16) for i in range(PAGE_GROUP)])

    @pl.when(g == pl.num_programs(1) - 1)
    def _():
        row = lax.broadcasted_iota(I32, thr.shape, 0)
        col = lax.broadcasted_iota(I32, thr.shape, 1)
        causal = jnp.where((col <= row) & (col < n_valid), 1, 0)
        update([seln_ref[...]], causal, [kn_ref[...]], [vn_ref[...]])
        out = acc_scr[...] / l_scr[...]
        lane = lax.broadcasted_iota(I32, (r, D_ATT), 1) // HEAD_DIM
        tot = jnp.where(lane == 0, out[:r], 0.0)
        for h in range(1, N_ATT_HEADS):
            tot = tot + jnp.where(lane == h, out[h * r:(h + 1) * r], 0.0)
        o_ref[...] = tot


def dsa_sample_attend(page_table, cache_kt, cache_vt, layer, q_bd, sel, thr, k_new_t, v_new_t, n_valid):
    b, n_pages = page_table.shape
    r = SAMPLE_ROWS
    page_spec = lambda i: pl.BlockSpec((None, None, D_ATT, PAGE_SIZE),
                                       lambda bi, g, pt: (layer, pt[bi, g * PAGE_GROUP + i], 0, 0))
    per_b = lambda shape: pl.BlockSpec((None,) + shape, lambda bi, g, pt: (bi,) + (0,) * len(shape))
    rows = N_ATT_HEADS * r
    return pl.pallas_call(
        functools.partial(_dsa_sample_attend_kernel, n_valid=n_valid),
        grid_spec=pltpu.PrefetchScalarGridSpec(
            num_scalar_prefetch=1,
            grid=(b, n_pages // PAGE_GROUP),
            in_specs=[page_spec(i) for i in range(PAGE_GROUP)] + [page_spec(i) for i in range(PAGE_GROUP)]
            + [per_b((rows, D_ATT)),
               pl.BlockSpec((None, PAGE_GROUP, r, PAGE_SIZE), lambda bi, g, pt: (bi, g, 0, 0)),
               pl.BlockSpec((None, None, r, PAGE_SIZE), lambda bi, g, pt: (bi, n_pages, 0, 0)),
               per_b((r, PAGE_SIZE)), per_b((D_ATT, PAGE_SIZE)), per_b((D_ATT, PAGE_SIZE))],
            out_specs=per_b((r, D_ATT)),
            scratch_shapes=[pltpu.VMEM((rows, 1), F32), pltpu.VMEM((rows, 1), F32), pltpu.VMEM((rows, D_ATT), F32)],
        ),
        out_shape=jax.ShapeDtypeStruct((b, r, D_ATT), F32),
        compiler_params=_cparams(("arbitrary", "arbitrary")),
        name="dsa_sample_attend",
    )(page_table, *([cache_kt] * PAGE_GROUP), *([cache_vt] * PAGE_GROUP), q_bd, sel, sel, thr, k_new_t, v_new_t)


def pack_w_in(w_in):
    d = w_in.shape[0]
    pad = jnp.zeros((d, ZP_END - ZP_KIW - (OFF_RW - OFF_KI)), w_in.dtype)
    return jnp.concatenate([w_in[:, OFF_S5:OFF_KI], w_in[:, OFF_RW:], w_in[:, OFF_KI:OFF_RW], pad], axis=1).astype(BF16)


def rope_tables(pos):
    half = HEAD_DIM // 2
    inv = ROPE_THETA ** (-jnp.arange(half, dtype=F32) / half)
    ang = pos.astype(F32)[:, None] * inv[None, :]
    cos, sin = jnp.cos(ang), jnp.sin(ang)
    return jnp.tile(jnp.concatenate([cos, cos], 1), (1, 2)), jnp.tile(jnp.concatenate([-sin, sin], 1), (1, 2))


def block_diag_queries(q):
    b, r, d = q.shape
    head_of_lane = jnp.arange(d) // HEAD_DIM
    keep = head_of_lane[None, :] == jnp.arange(N_ATT_HEADS)[:, None]
    return jnp.where(keep[None, :, None, :], q[:, None], jnp.zeros((), q.dtype)).reshape(b, N_ATT_HEADS * r, d)


def _pad_rows(a, rows):
    return jnp.pad(a, ((0, 0), (0, rows - a.shape[1])) + ((0, 0),) * (a.ndim - 2))


Q_SCALE = HEAD_DIM ** -0.5
Q_SCALE_EXP2 = Q_SCALE * math.log2(math.e)
PROMPT_TILE = 256
RW_PRE_TILE = 512
RW_STEP_ROWS = 256


def kernel(x_prompt, x_sample, c_prompt, c_sample, cache_k, cache_v, cache_kidx, state_s5_re, state_s5_im, state_rwkv, state_rwkv_shift, page_table, ada_w, ada_b, norm_pre, norm_post, ffn_wi, ffn_wo, w_in, w_out, s5_a_re, s5_a_im, s5_log_dt, s5_b_re, s5_b_im, s5_c_re, s5_c_im, s5_d, s5_glu_w, s5_glu_b, rw_mu, rw_w0, rw_w2, rw_a0, rw_a2, rw_g2, rw_k_k, rw_k_a, rw_r_k, rw_ln_w, rw_ln_b):
    bp, sp, d = x_prompt.shape
    bs, ss, _ = x_sample.shape
    depth = ada_w.shape[0]
    past = page_table.shape[1] * PAGE_SIZE
    np_tok, ns_tok = bp * sp, bs * ss
    tpb = sp // PROMPT_TILE
    row1 = lambda a: a.reshape(1, -1)

    c_all = _pad_rows(jnp.concatenate([c_prompt, c_sample], axis=0)[None], -(-(bp + bs) // 8) * 8)[0]
    mod = ada_mod(c_all, ada_w, ada_b).reshape(depth, c_all.shape[0], N_SUB, 3, d)

    cos_p, sin_p = rope_tables(jnp.arange(sp))
    cos_s, sin_s = (jnp.tile(t, (bs, 1)) for t in rope_tables(past + jnp.arange(ss)))
    n_pool = cache_k.shape[1]
    cache_kt = jnp.transpose(cache_k, (0, 1, 3, 4, 2)).reshape(depth, n_pool, D_ATT, PAGE_SIZE)
    cache_vt = jnp.transpose(cache_v, (0, 1, 3, 4, 2)).reshape(depth, n_pool, D_ATT, PAGE_SIZE)
    cache_kidx_t = jnp.swapaxes(cache_kidx, 2, 3)

    xp = x_prompt.reshape(np_tok, d)
    xs = x_sample.reshape(ns_tok, d)
    outs_p, outs_s = [], []
    for l in range(depth):
        mod_p = mod[l, :bp]
        mod_s = jnp.repeat(mod[l, bp:bp + bs], ss, axis=0)
        pm = lambda i, j: mod_p[:, i, j][:, None, :]
        sm = lambda i, j: mod_s[:, i, j][None]
        npre = lambda i: row1(norm_pre[l, i])
        npost = lambda i: row1(norm_post[l, i])
        wi0, wo0 = ffn_wi[l, 0].astype(BF16), ffn_wo[l, 0].astype(BF16)
        wi1, wo1 = ffn_wi[l, 1].astype(BF16), ffn_wo[l, 1].astype(BF16)
        w_in_p = pack_w_in(w_in[l])
        w_out_b = w_out[l].astype(BF16)
        powre, powim, bb_re, bb_im = s5_discretise(s5_a_re[l], s5_a_im[l], s5_log_dt[l], s5_b_re[l], s5_b_im[l],
                                                   rows=S5_CHUNK)
        wb, wc = s5_matrices(bb_re, bb_im, s5_c_re[l], s5_c_im[l])
        s5_tail = (powre, powim, wb, wc, row1(s5_d[l]), s5_glu_w[l].astype(BF16), row1(s5_glu_b[l]))
        wlr = rwkv_lowrank_matrix(rw_w2[l], rw_a2[l], rw_g2[l])
        rw_pre = (row1(rw_mu[l]), row1(rw_w0[l]), row1(rw_a0[l]), wlr, row1(rw_k_k[l]), row1(rw_k_a[l]),
                  row1(rw_r_k[l]))
        ln = (row1(rw_ln_w[l]), row1(rw_ln_b[l]))

        xp = ffn_block(xp, pm(0, 0), pm(0, 1), pm(0, 2), npre(0), npost(0), wi0, wo0, PROMPT_TILE, tpb)
        (u, q, _, kb, _, _, qi3, _, ki3, rw, vt, kt_f, vt_f, kiw_t) = inproj_block(
            xp, pm(1, 0), pm(1, 1), npre(1), w_in_p, cos_p, sin_p, PROMPT_TILE, tpb, tpb, Q_SCALE_EXP2)
        seq = lambda a: a.reshape(bp, sp, a.shape[-1])
        y_s5, h_s5 = s5_prompt(seq(u), jnp.zeros((bp, 1, 2 * S5_W), F32), *s5_tail)
        y_att = dsa_prompt(seq(q), seq(kb), vt, seq(qi3), seq(ki3), kiw_t)
        cols = seq(rw)
        packed = rwkv_prepare(cols, jnp.zeros((bp, 1, N_RW_COLS), F32), *rw_pre, RW_PRE_TILE, RW_CHUNK, RW_CHUNK)
        y_rw, st_rw = rwkv_chunked(packed, jnp.zeros((bp, N_RW_HEADS, HEAD_DIM, HEAD_DIM), F32), *ln,
                                   RW_STEP_ROWS, RW_CHUNK)
        flat = lambda a: a.reshape(np_tok, a.shape[-1])
        xp = outproj_block(xp, flat(y_s5), flat(y_att), flat(y_rw), pm(1, 2), npost(1), w_out_b, PROMPT_TILE, tpb)
        xp = ffn_block(xp, pm(2, 0), pm(2, 1), pm(2, 2), npre(2), npost(2), wi1, wo1, PROMPT_TILE, tpb)
        heads_last = lambda a: jnp.transpose(a.reshape(bp, N_ATT_HEADS, HEAD_DIM, sp), (0, 3, 1, 2))
        outs_p.append((heads_last(kt_f), heads_last(vt_f), jnp.swapaxes(kiw_t[:, :IDX_DIM, :], 1, 2),
                       h_s5[:, 0, :S5_W].reshape(bp, S5_GROUPS, S5_STATE),
                       h_s5[:, 0, S5_W:].reshape(bp, S5_GROUPS, S5_STATE), st_rw, cols[:, sp - 1]))

        xs = ffn_block(xs, sm(0, 0), sm(0, 1), sm(0, 2), npre(0), npost(0), wi0, wo0, ns_tok, 1)
        u, q, k, kb, v, vb, qi3, kiw, ki3, rw = inproj_block(xs, sm(1, 0), sm(1, 1), npre(1), w_in_p, cos_s, sin_s,
                                                             ns_tok, 1, 1, Q_SCALE)[:10]
        seq = lambda a: a.reshape(bs, ss, a.shape[-1])
        h0 = jnp.concatenate([state_s5_re[l].reshape(bs, S5_W), state_s5_im[l].reshape(bs, S5_W)], axis=1)
        y_s5, h_s5 = s5_sample(jnp.swapaxes(seq(u), 0, 1), h0, *s5_tail)
        y_s5 = jnp.swapaxes(y_s5, 0, 1)
        tok_t = lambda a: jnp.swapaxes(_pad_rows(a, PAGE_SIZE), 1, 2)
        sel, thr = dsa_sample_index(page_table, cache_kidx_t, l, _pad_rows(seq(qi3), SAMPLE_ROWS),
                                    _pad_rows(seq(kiw), SAMPLE_ROWS), tok_t(seq(kiw)[:, :, :IDX_DIM]), ss)
        q_bd = block_diag_queries(_pad_rows(seq(q), SAMPLE_ROWS))
        y_att = dsa_sample_attend(page_table, cache_kt, cache_vt, l, q_bd, sel, thr, tok_t(seq(kb)), tok_t(seq(vb)),
                                  ss)[:, :ss]
        cols = seq(rw)
        packed = rwkv_prepare(_pad_rows(cols, RW_CHUNK), state_rwkv_shift[l].reshape(bs, 1, N_RW_COLS), *rw_pre,
                              RW_CHUNK, RW_CHUNK, ss)
        y_rw, st_rw = rwkv_chunked(packed, state_rwkv[l], *ln, RW_CHUNK, RW_CHUNK)
        flat = lambda a: a.reshape(ns_tok, a.shape[-1])
        xs = outproj_block(xs, flat(y_s5), flat(y_att), flat(y_rw[:, :ss]), sm(1, 2), npost(1), w_out_b, ns_tok, 1)
        xs = ffn_block(xs, sm(2, 0), sm(2, 1), sm(2, 2), npre(2), npost(2), wi1, wo1, ns_tok, 1)
        outs_s.append((k.reshape(bs, ss, N_ATT_HEADS, HEAD_DIM), v.reshape(bs, ss, N_ATT_HEADS, HEAD_DIM),
                       seq(kiw)[:, :, :IDX_DIM],
                       h_s5[:, :S5_W].reshape(bs, S5_GROUPS, S5_STATE),
                       h_s5[:, S5_W:].reshape(bs, S5_GROUPS, S5_STATE), st_rw, cols[:, ss - 1]))

    stack = lambda outs, i: jnp.stack([o[i] for o in outs])
    return ((xp.reshape(bp, sp, d), xs.reshape(bs, ss, d))
            + tuple(stack(outs_p, i) for i in range(7)) + tuple(stack(outs_s, i) for i in range(7)))
```

```python
import functools
import math

import jax
import jax.numpy as jnp
from jax import lax
from jax.experimental import pallas as pl
from jax.experimental.pallas import tpu as pltpu

F32 = jnp.float32
BF16 = jnp.bfloat16
I32 = jnp.int32

D_MODEL = 1024
PAGE_SIZE = 128
D_S5 = 256
D_ATT = 512
D_RW = 256
S5_GROUP = 16
S5_GROUPS = D_S5 // S5_GROUP
S5_STATE = 64
S5_W = S5_GROUPS * S5_STATE
HEAD_DIM = 64
N_ATT_HEADS = D_ATT // HEAD_DIM
N_IDX_HEADS = 4
IDX_DIM = 64
TOPK_MAX = 256
Q_BLOCK = 128
ROPE_THETA = 10000.0
N_RW_HEADS = D_RW // HEAD_DIM
RW_W_RANK = 32
RW_A_RANK = 32
RW_G_RANK = 64
RW_GN_EPS = 64e-5
N_RW_COLS = 3 * D_RW + RW_W_RANK + RW_A_RANK + RW_G_RANK
D_FF = 2816
HALF_STEP = 0.5
RMS_EPS = 1e-6
N_SUB = 3
NEG_INF = -1e30
OFF_S5 = 0
OFF_Q = OFF_S5 + D_S5
OFF_K = OFF_Q + D_ATT
OFF_V = OFF_K + D_ATT
OFF_QI = OFF_V + D_ATT
OFF_KI = OFF_QI + N_IDX_HEADS * IDX_DIM
OFF_WI = OFF_KI + IDX_DIM
OFF_RW = OFF_WI + N_IDX_HEADS
N_IN = OFF_RW + N_RW_COLS

LANES = 128
VMEM_LIMIT = 56 * 1024 * 1024
INT_MIN = -(2 ** 31)

ZP_U, ZP_Q, ZP_K, ZP_V, ZP_QI, ZP_RW, ZP_KIW, ZP_END = 0, 256, 768, 1280, 1792, 2048, 2944, 3072

RW_CHUNK = 64
S5_CHUNK = 128
KEY_CHUNK = 512
ATT_CHUNK = 512


def _cparams(sem):
    return pltpu.CompilerParams(dimension_semantics=sem, vmem_limit_bytes=VMEM_LIMIT)


def _dot(a, b):
    return jnp.dot(a, b, preferred_element_type=F32)


def _dot_nt(a, b):
    return lax.dot_general(a, b, (((1,), (1,)), ((), ())), preferred_element_type=F32)


def _split(x):
    hi = x.astype(BF16)
    lo = (x - hi.astype(F32)).astype(BF16)
    return hi, lo


def _dot3(a, b):
    ah, al = _split(a)
    bh, bl = _split(b)
    return _dot(ah, bh) + (_dot(ah, bl) + _dot(al, bh))


def _dot3_nt(a, b):
    ah, al = _split(a)
    bh, bl = _split(b)
    return _dot_nt(ah, bh) + (_dot_nt(ah, bl) + _dot_nt(al, bh))


def _mm3(a, b):
    return _dot(a[0], b[0]) + (_dot(a[0], b[1]) + _dot(a[1], b[0]))


def _mm3_nt(a, b):
    return _dot_nt(a[0], b[0]) + (_dot_nt(a[0], b[1]) + _dot_nt(a[1], b[0]))


def _rms(x, g):
    return x * lax.rsqrt(jnp.mean(x * x, axis=-1, keepdims=True) + RMS_EPS) * g


def _sigmoid(x):
    return 1.0 / (1.0 + jnp.exp(-x))


def _ada_kernel(c_ref, w_ref, b_ref, o_ref):
    c = c_ref[...]
    h = (c * _sigmoid(c)).astype(BF16)
    o_ref[...] = _dot(h, w_ref[...].astype(BF16)) + b_ref[...]


def ada_mod(c_all, ada_w, ada_b, tn=1152):
    depth, d, n = ada_w.shape
    rows = c_all.shape[0]
    return pl.pallas_call(
        _ada_kernel,
        grid=(depth, n // tn),
        in_specs=[pl.BlockSpec((rows, d), lambda l, j: (0, 0)),
                  pl.BlockSpec((None, d, tn), lambda l, j: (l, 0, j)),
                  pl.BlockSpec((None, 1, tn), lambda l, j: (l, 0, j))],
        out_specs=pl.BlockSpec((None, rows, tn), lambda l, j: (l, 0, j)),
        out_shape=jax.ShapeDtypeStruct((depth, rows, n), F32),
        compiler_params=_cparams(("arbitrary", "arbitrary")),
        name="ada_mod",
    )(c_all, ada_w, ada_b.reshape(depth, 1, n))


def _mod_spec(mod, tiles_per_group):
    r = mod.shape[1]
    return pl.BlockSpec((None, r, mod.shape[2]), lambda i: (i // tiles_per_group, 0, 0))


def _ffn_kernel(x_ref, shift_ref, scale_ref, gate_ref, gpre_ref, gpost_ref, wi_ref, wo_ref, o_ref, *, res_w):
    x = x_ref[...]
    h = (_rms(x, gpre_ref[...]) * (1.0 + scale_ref[...]) + shift_ref[...]).astype(BF16)
    g = _dot(h, wi_ref[:, :D_FF])
    u = _dot(h, wi_ref[:, D_FF:])
    a = (g * _sigmoid(g) * u).astype(BF16)
    o = _dot(a, wo_ref[...])
    o_ref[...] = x + (res_w * gate_ref[...]) * _rms(o, gpost_ref[...])


def ffn_block(x, shift, scale, gate, g_pre, g_post, wi, wo, tm, tiles_per_group):
    n, d = x.shape
    row = pl.BlockSpec((tm, d), lambda i: (i, 0))
    const = lambda a: pl.BlockSpec(a.shape, lambda i: (0,) * a.ndim)
    return pl.pallas_call(
        functools.partial(_ffn_kernel, res_w=HALF_STEP),
        grid=(n // tm,),
        in_specs=[row, _mod_spec(shift, tiles_per_group), _mod_spec(scale, tiles_per_group),
                  _mod_spec(gate, tiles_per_group), const(g_pre), const(g_post), const(wi), const(wo)],
        out_specs=row,
        out_shape=jax.ShapeDtypeStruct((n, d), F32),
        compiler_params=_cparams(("arbitrary",)),
        name="ffn_block",
    )(x, shift, scale, gate, g_pre, g_post, wi, wo)


def _rot_block(blk, cos, sin, lane):
    partner = jnp.where((lane & 32) == 0, pltpu.roll(blk, LANES - 32, 1), pltpu.roll(blk, 32, 1))
    return blk * cos + partner * sin


def _inproj_kernel(x_ref, shift_ref, scale_ref, gpre_ref, w_ref, cos_ref, sin_ref,
                   u_ref, q_ref, k_ref, kb_ref, v_ref, vb_ref, qi3_ref, kiw_ref, ki3_ref, rw_ref,
                   vt_ref, ktf_ref, vtf_ref, kiwt_ref, *, q_scale):
    x = x_ref[...]
    h = (_rms(x, gpre_ref[...]) * (1.0 + scale_ref[...]) + shift_ref[...]).astype(BF16)
    z = _dot(h, w_ref[...])
    cos = cos_ref[...]
    sin = sin_ref[...]
    lane = lax.broadcasted_iota(I32, cos.shape, 1)
    lo_half = lane < 64
    rot = lambda off: _rot_block(z[:, off:off + LANES], cos, sin, lane)

    u_ref[...] = z[:, ZP_U:ZP_Q]
    for j in range(D_ATT // LANES):
        q_ref[:, j * LANES:(j + 1) * LANES] = (rot(ZP_Q + j * LANES) * q_scale).astype(BF16)
        kr = rot(ZP_K + j * LANES)
        k_ref[:, j * LANES:(j + 1) * LANES] = kr
        kb_ref[:, j * LANES:(j + 1) * LANES] = kr.astype(BF16)
        ktf_ref[j * LANES:(j + 1) * LANES, :] = kr.T
    v = z[:, ZP_V:ZP_QI]
    v_ref[...] = v
    vb_ref[...] = v.astype(BF16)
    vt = v.T
    vtf_ref[...] = vt
    vt_ref[...] = vt.astype(BF16)
    for j in range(N_IDX_HEADS * IDX_DIM // LANES):
        qr = rot(ZP_QI + j * LANES)
        hi = qr.astype(BF16).astype(F32)
        lo = qr - hi
        hi_sw = pltpu.roll(hi, 64, 1)
        lo_sw = pltpu.roll(lo, 64, 1)
        zero = jnp.zeros_like(hi)
        for half in range(2):
            base = (2 * j + half) * 2 * LANES
            a, b = (hi, lo) if half == 0 else (hi_sw, lo_sw)
            a_sw = hi_sw if half == 0 else hi
            qi3_ref[:, base:base + LANES] = jnp.where(lo_half, a, a_sw).astype(BF16)
            qi3_ref[:, base + LANES:base + 2 * LANES] = jnp.where(lo_half, b, zero).astype(BF16)
    raw = z[:, ZP_KIW:ZP_END]
    kiw = jnp.where(lo_half, _rot_block(raw, cos, sin, lane), raw)
    kiw_ref[...] = kiw
    kiwt_ref[...] = kiw.T
    hi = kiw.astype(BF16).astype(F32)
    lo = kiw - hi
    ki3_ref[:, :LANES] = jnp.where(lo_half, hi, pltpu.roll(lo, 64, 1)).astype(BF16)
    ki3_ref[:, LANES:] = jnp.where(lo_half, hi, jnp.zeros_like(hi)).astype(BF16)
    rw_ref[...] = z[:, ZP_RW:ZP_KIW]


def inproj_block(x, shift, scale, g_pre, w, cos_t, sin_t, tm, tiles_per_group, pos_tiles, q_scale):
    n, d = x.shape
    row = lambda width: pl.BlockSpec((tm, width), lambda i: (i, 0))
    const = lambda a: pl.BlockSpec(a.shape, lambda i: (0,) * a.ndim)
    tab = pl.BlockSpec((tm, LANES), lambda i: (i % pos_tiles, 0))
    widths = [(D_S5, F32), (D_ATT, BF16), (D_ATT, F32), (D_ATT, BF16), (D_ATT, F32), (D_ATT, BF16),
              (N_IDX_HEADS * 2 * LANES, BF16), (LANES, F32), (2 * LANES, BF16), (N_RW_COLS, F32)]
    t_widths = [(D_ATT, BF16), (D_ATT, F32), (D_ATT, F32), (LANES, F32)]
    return pl.pallas_call(
        functools.partial(_inproj_kernel, q_scale=q_scale),
        grid=(n // tm,),
        in_specs=[row(d), _mod_spec(shift, tiles_per_group), _mod_spec(scale, tiles_per_group),
                  const(g_pre), const(w), tab, tab],
        out_specs=[row(wd) for wd, _ in widths]
        + [pl.BlockSpec((None, wd, tm), lambda i: (i // pos_tiles, 0, i % pos_tiles)) for wd, _ in t_widths],
        out_shape=[jax.ShapeDtypeStruct((n, wd), dt) for wd, dt in widths]
        + [jax.ShapeDtypeStruct((n // (pos_tiles * tm), wd, pos_tiles * tm), dt) for wd, dt in t_widths],
        compiler_params=_cparams(("arbitrary",)),
        name="inproj_block",
    )(x, shift, scale, g_pre, w, cos_t, sin_t)


def _outproj_kernel(x_ref, ys5_ref, yatt_ref, yrw_ref, gate_ref, gpost_ref, w_ref, o_ref):
    o = (_dot(ys5_ref[...].astype(BF16), w_ref[:D_S5, :])
         + _dot(yatt_ref[...].astype(BF16), w_ref[D_S5:D_S5 + D_ATT, :])
         + _dot(yrw_ref[...].astype(BF16), w_ref[D_S5 + D_ATT:, :]))
    o_ref[...] = x_ref[...] + gate_ref[...] * _rms(o, gpost_ref[...])


def outproj_block(x, y_s5, y_att, y_rw, gate, g_post, w, tm, tiles_per_group):
    n, d = x.shape
    row = lambda width: pl.BlockSpec((tm, width), lambda i: (i, 0))
    const = lambda a: pl.BlockSpec(a.shape, lambda i: (0,) * a.ndim)
    return pl.pallas_call(
        _outproj_kernel,
        grid=(n // tm,),
        in_specs=[row(d), row(D_S5), row(D_ATT), row(D_RW), _mod_spec(gate, tiles_per_group),
                  const(g_post), const(w)],
        out_specs=row(d),
        out_shape=jax.ShapeDtypeStruct((n, d), F32),
        compiler_params=_cparams(("arbitrary",)),
        name="outproj_block",
    )(x, y_s5, y_att, y_rw, gate, g_post, w)


def _gelu_tanh(x):
    return 0.5 * x * (1.0 + jnp.tanh(math.sqrt(2.0 / math.pi) * (x + 0.044715 * (x * x * x))))


def _s5_disc_kernel(are_ref, aim_ref, ldt_ref, arec_ref, aimc_ref, ldtc_ref, bre_ref, bim_ref,
                    powre_ref, powim_ref, bbre_ref, bbim_ref, *, rows):
    def zoh(ar, ai, ldt):
        dt = jnp.exp(ldt)
        mag = jnp.exp(ar * dt)
        abr, abi = mag * jnp.cos(ai * dt), mag * jnp.sin(ai * dt)
        den = ar * ar + ai * ai
        nr, ni = abr - 1.0, abi
        return abr, abi, (nr * ar + ni * ai) / den, (ni * ar - nr * ai) / den

    abr, abi, _, _ = zoh(are_ref[...], aim_ref[...], ldt_ref[...])
    pr = jnp.broadcast_to(abr, (rows, abr.shape[1]))
    pi = jnp.broadcast_to(abi, (rows, abr.shape[1]))
    row = lax.broadcasted_iota(I32, pr.shape, 0)
    d = 1
    while d < rows:
        sr = pltpu.roll(pr, d, 0)
        si = pltpu.roll(pi, d, 0)
        m = row >= d
        pr, pi = jnp.where(m, pr * sr - pi * si, pr), jnp.where(m, pr * si + pi * sr, pi)
        d *= 2
    powre_ref[...] = pr
    powim_ref[...] = pi
    _, _, cr, ci = zoh(arec_ref[...], aimc_ref[...], ldtc_ref[...])
    br, bi = bre_ref[...], bim_ref[...]
    bbre_ref[...] = cr * br - ci * bi
    bbim_ref[...] = cr * bi + ci * br


def s5_discretise(a_re, a_im, log_dt, b_re, b_im, rows):
    g, p = a_re.shape
    w = g * p
    ldt = jnp.broadcast_to(log_dt[:, None], (g, p))
    args = (a_re.reshape(1, w), a_im.reshape(1, w), ldt.reshape(1, w),
            a_re.reshape(w, 1), a_im.reshape(w, 1), ldt.reshape(w, 1),
            b_re.reshape(w, S5_GROUP), b_im.reshape(w, S5_GROUP))
    return pl.pallas_call(
        functools.partial(_s5_disc_kernel, rows=rows),
        out_shape=[jax.ShapeDtypeStruct((rows, w), F32), jax.ShapeDtypeStruct((rows, w), F32),
                   jax.ShapeDtypeStruct((w, S5_GROUP), F32), jax.ShapeDtypeStruct((w, S5_GROUP), F32)],
        name="s5_discretise",
    )(*args)


def _s5_head(y, u, d_ref, gluw_ref, glub_ref):
    y = _gelu_tanh(y + d_ref[...] * u)
    return y * _sigmoid(_dot(y.astype(BF16), gluw_ref[...]) + glub_ref[...])


def _s5_kernel(u_ref, h0_ref, powre_ref, powim_ref, wb_ref, wc_ref, d_ref, gluw_ref, glub_ref,
               y_ref, ht_ref, cre, cim):
    t = pl.program_id(1)
    rows = u_ref.shape[0]

    @pl.when(t == 0)
    def _():
        cre[...] = h0_ref[:, :S5_W]
        cim[...] = h0_ref[:, S5_W:]

    u = u_ref[...]
    bu = _dot(u.astype(BF16), wb_ref[...])
    hr, hi = bu[:, :S5_W], bu[:, S5_W:]
    row = lax.broadcasted_iota(I32, hr.shape, 0)
    d = 1
    while d < rows:
        ar, ai = powre_ref[d - 1:d, :], powim_ref[d - 1:d, :]
        sr, si = pltpu.roll(hr, d, 0), pltpu.roll(hi, d, 0)
        m = row >= d
        hr, hi = (hr + jnp.where(m, ar * sr - ai * si, 0.0), hi + jnp.where(m, ar * si + ai * sr, 0.0))
        d *= 2
    pr, pi = powre_ref[...], powim_ref[...]
    c_r, c_i = cre[...], cim[...]
    hr, hi = hr + (pr * c_r - pi * c_i), hi + (pr * c_i + pi * c_r)
    cre[...] = hr[rows - 1:rows, :]
    cim[...] = hi[rows - 1:rows, :]
    ht_ref[:, :S5_W] = hr[rows - 1:rows, :]
    ht_ref[:, S5_W:] = hi[rows - 1:rows, :]
    y = _dot(hr.astype(BF16), wc_ref[:S5_W, :]) + _dot(hi.astype(BF16), wc_ref[S5_W:, :])
    y_ref[...] = _s5_head(y, u, d_ref, gluw_ref, glub_ref)


def s5_prompt(u, h0, powre, powim, wb, wc, d, gluw, glub):
    b, s, _ = u.shape
    rows = powre.shape[0]
    const = lambda a: pl.BlockSpec(a.shape, lambda i, j: (0,) * a.ndim)
    return pl.pallas_call(
        _s5_kernel,
        grid=(b, s // rows),
        in_specs=[pl.BlockSpec((None, rows, D_S5), lambda i, j: (i, j, 0)),
                  pl.BlockSpec((None, 1, 2 * S5_W), lambda i, j: (i, 0, 0)),
                  const(powre), const(powim), const(wb), const(wc), const(d), const(gluw), const(glub)],
        out_specs=[pl.BlockSpec((None, rows, D_S5), lambda i, j: (i, j, 0)),
                   pl.BlockSpec((None, 1, 2 * S5_W), lambda i, j: (i, 0, 0))],
        out_shape=[jax.ShapeDtypeStruct((b, s, D_S5), F32), jax.ShapeDtypeStruct((b, 1, 2 * S5_W), F32)],
        scratch_shapes=[pltpu.VMEM((1, S5_W), F32), pltpu.VMEM((1, S5_W), F32)],
        compiler_params=_cparams(("arbitrary", "arbitrary")),
        name="s5_prompt",
    )(u, h0, powre, powim, wb, wc, d, gluw, glub)


def _s5_step_kernel(u_ref, h0_ref, powre_ref, powim_ref, wb_ref, wc_ref, d_ref, gluw_ref, glub_ref,
                    y_ref, ht_ref):
    ar, ai = powre_ref[0:1, :], powim_ref[0:1, :]
    hr, hi = h0_ref[:, :S5_W], h0_ref[:, S5_W:]
    for t in range(u_ref.shape[0]):
        u = u_ref[t]
        bu = _dot(u.astype(BF16), wb_ref[...])
        hr, hi = ar * hr - ai * hi + bu[:, :S5_W], ar * hi + ai * hr + bu[:, S5_W:]
        y = _dot(hr.astype(BF16), wc_ref[:S5_W, :]) + _dot(hi.astype(BF16), wc_ref[S5_W:, :])
        y_ref[t] = _s5_head(y, u, d_ref, gluw_ref, glub_ref)
    ht_ref[:, :S5_W] = hr
    ht_ref[:, S5_W:] = hi


def s5_sample(u_tm, h0, powre, powim, wb, wc, d, gluw, glub):
    s, b, _ = u_tm.shape
    return pl.pallas_call(
        _s5_step_kernel,
        out_shape=[jax.ShapeDtypeStruct((s, b, D_S5), F32), jax.ShapeDtypeStruct((b, 2 * S5_W), F32)],
        compiler_params=pltpu.CompilerParams(vmem_limit_bytes=VMEM_LIMIT),
        name="s5_sample",
    )(u_tm, h0, powre, powim, wb, wc, d, gluw, glub)


def s5_matrices(bb_re, bb_im, c_re, c_im):
    g, p, h = S5_GROUPS, S5_STATE, S5_GROUP
    eye = jnp.eye(g, dtype=F32)
    bd_in = lambda bb: jnp.einsum('gph,gk->ghkp', bb.reshape(g, p, h), eye).reshape(g * h, g * p)
    bd_out = lambda c: jnp.einsum('ghp,gk->gpkh', c, eye).reshape(g * p, g * h)
    wb = jnp.concatenate([bd_in(bb_re), bd_in(bb_im)], axis=1).astype(BF16)
    wc = jnp.concatenate([bd_out(c_re), -bd_out(c_im)], axis=0).astype(BF16)
    return wb, wc


RW_PACK = 8 * D_RW


def _head_ones(n):
    r = lax.broadcasted_iota(I32, (n, n), 0) // HEAD_DIM
    c = lax.broadcasted_iota(I32, (n, n), 1) // HEAD_DIM
    return jnp.where(r == c, 1.0, 0.0).astype(BF16)


def _seg_sum(x, ones_bd):
    hi, lo = _split(x)
    return _dot(hi, ones_bd) + _dot(lo, ones_bd)


def _softplus(x):
    return jnp.maximum(x, 0.0) + jnp.log(1.0 + jnp.exp(-jnp.abs(x)))


def _rw_pre_kernel(cols_ref, shift0_ref, mu_ref, w0_ref, a0_ref, wlr_ref, kk_ref, ka_ref, rk_ref,
                   o_ref, carry, *, chunk, valid):
    @pl.when(pl.program_id(1) == 0)
    def _():
        carry[...] = shift0_ref[...]

    cf = cols_ref[...]
    tm = cf.shape[0]
    row = lax.broadcasted_iota(I32, (tm, 1), 0)
    prev = jnp.where(row == 0, carry[...], pltpu.roll(cf, 1, 0))
    carry[...] = cf[tm - 1:tm, :]
    xs = cf + (prev - cf) * mu_ref[...]
    r, k, v = xs[:, :D_RW], xs[:, D_RW:2 * D_RW], xs[:, 2 * D_RW:3 * D_RW]
    lr = xs[:, 3 * D_RW:]
    lane = lax.broadcasted_iota(I32, lr.shape, 1)
    t = jnp.where(lane < RW_W_RANK, jnp.tanh(lr), jnp.where(lane < RW_W_RANK + RW_A_RANK, lr, _sigmoid(lr)))
    proj = _dot3(t, wlr_ref[...])
    w = -_softplus(-(w0_ref[...] + proj[:, :D_RW])) - 0.5
    logw = -jnp.exp(w)
    a = _sigmoid(a0_ref[...] + proj[:, D_RW:2 * D_RW])
    g = proj[:, 2 * D_RW:]
    ones_bd = _head_ones(D_RW)
    kk = k * kk_ref[...]
    kk = kk / jnp.maximum(jnp.sqrt(_seg_sum(kk * kk, ones_bd)), 1e-12)
    km = k * (1.0 + (a - 1.0) * ka_ref[...])
    bonus = _seg_sum(r * km * rk_ref[...], ones_bd) * v
    pos = row % chunk
    if valid < chunk:
        live = pos < valid
        zero = jnp.zeros_like(r)
        logw, kk, km, v, r = (jnp.where(live, logw, zero), jnp.where(live, kk, zero), jnp.where(live, km, zero),
                              jnp.where(live, v, zero), jnp.where(live, r, zero))
    gc = logw
    d = 1
    while d < chunk:
        gc = gc + jnp.where(pos >= d, pltpu.roll(gc, d, 0), 0.0)
        d *= 2
    eg = jnp.exp(gc)
    eng = jnp.exp(-gc)
    o_ref[:, 0 * D_RW:1 * D_RW] = r * eg
    o_ref[:, 1 * D_RW:2 * D_RW] = kk * jnp.exp(gc - logw)
    o_ref[:, 2 * D_RW:3 * D_RW] = kk * a * eng
    o_ref[:, 3 * D_RW:4 * D_RW] = km * eng
    o_ref[:, 4 * D_RW:5 * D_RW] = v
    o_ref[:, 5 * D_RW:6 * D_RW] = eg
    o_ref[:, 6 * D_RW:7 * D_RW] = bonus
    o_ref[:, 7 * D_RW:8 * D_RW] = g


def rwkv_prepare(cols, shift0, mu, w0, a0, wlr, k_k, k_a, r_k, tm, chunk, valid):
    b, s, c = cols.shape
    const = lambda a: pl.BlockSpec(a.shape, lambda i, j: (0,) * a.ndim)
    return pl.pallas_call(
        functools.partial(_rw_pre_kernel, chunk=chunk, valid=valid),
        grid=(b, s // tm),
        in_specs=[pl.BlockSpec((None, tm, c), lambda i, j: (i, j, 0)),
                  pl.BlockSpec((None, 1, c), lambda i, j: (i, 0, 0)),
                  const(mu), const(w0), const(a0), const(wlr), const(k_k), const(k_a), const(r_k)],
        out_specs=pl.BlockSpec((None, tm, RW_PACK), lambda i, j: (i, j, 0)),
        out_shape=jax.ShapeDtypeStruct((b, s, RW_PACK), F32),
        scratch_shapes=[pltpu.VMEM((1, c), F32)],
        compiler_params=_cparams(("arbitrary", "arbitrary")),
        name="rwkv_prepare",
    )(cols, shift0, mu, w0, a0, wlr, k_k, k_a, r_k)


def _rw_chunk_kernel(x_ref, s0_ref, lnw_ref, lnb_ref, y_ref, st_ref, s_scr, y_scr, *, chunk):
    @pl.when(pl.program_id(1) == 0)
    def _():
        s_scr[...] = s0_ref[...]

    rows = x_ref.shape[0]
    hd = HEAD_DIM
    ri = lax.broadcasted_iota(I32, (chunk, chunk), 0)
    ci = lax.broadcasted_iota(I32, (chunk, chunk), 1)
    strict = ri > ci
    incl = ri >= ci
    eye = jnp.where(ri == ci, 1.0, 0.0)
    ek = lax.broadcasted_iota(I32, (hd, hd), 0) == lax.broadcasted_iota(I32, (hd, hd), 1)
    eye_k = jnp.where(ek, 1.0, 0.0)
    n_sq = chunk.bit_length() - 2
    n_chunks = rows // chunk
    probs = [(c, h) for c in range(n_chunks) for h in range(N_RW_HEADS)]
    each = lambda f: [f(i) for i in range(len(probs))]

    def col(i, j):
        c, h = probs[i]
        return x_ref[c * chunk:(c + 1) * chunk, j * D_RW + h * hd:j * D_RW + (h + 1) * hd]

    rt, kt, bt, km, v = (each(lambda i: col(i, j)) for j in range(5))
    lhs = each(lambda i: _split(jnp.concatenate([kt[i], rt[i]], axis=0)))
    bts, kms, vs = each(lambda i: _split(bt[i])), each(lambda i: _split(km[i])), each(lambda i: _split(v[i]))
    gb = each(lambda i: _mm3_nt(lhs[i], bts[i]))
    gk = each(lambda i: _mm3_nt(lhs[i], kms[i]))
    a_bb = each(lambda i: jnp.where(strict, gb[i][:chunk], 0.0))
    a_rb = each(lambda i: _split(jnp.where(incl, gb[i][chunk:], 0.0)))
    a_kr = each(lambda i: _split(jnp.concatenate([jnp.where(strict, gk[i][:chunk], 0.0),
                                                  jnp.where(incl, gk[i][chunk:], 0.0)], axis=0)))
    av = each(lambda i: _mm3(a_kr[i], vs[i]))
    minv = each(lambda i: eye - a_bb[i])
    p = a_bb
    for _ in range(n_sq):
        ps = each(lambda i: _split(p[i]))
        p = each(lambda i: _mm3(ps[i], ps[i]))
        minv = each(lambda i: minv[i] + _mm3(_split(minv[i]), _split(p[i])))
    minvs = each(lambda i: _split(minv[i]))
    khat = each(lambda i: _mm3(minvs[i], _split(kt[i])))
    p1 = each(lambda i: _mm3(minvs[i], _split(av[i][:chunk])))
    rhat = each(lambda i: rt[i] - _mm3(a_rb[i], _split(khat[i])))
    y1 = each(lambda i: av[i][chunk:] - _mm3(a_rb[i], _split(p1[i])))
    tb = each(lambda i: _mm3(_split(jnp.concatenate([khat[i].T, p1[i].T], axis=0)), bts[i]))
    vk = each(lambda i: _mm3(_split(v[i].T), kms[i]))

    def eg_last(i):
        c, h = probs[i]
        return x_ref[(c + 1) * chunk - 1:(c + 1) * chunk, 5 * D_RW + h * hd:5 * D_RW + (h + 1) * hd]

    gmat = each(lambda i: (eye_k - tb[i][:hd]) * eg_last(i))
    umat = each(lambda i: (vk[i] - tb[i][hd:]) * eg_last(i))
    state = [s_scr[h] for h in range(N_RW_HEADS)]
    for i, (c, h) in enumerate(probs):
        y_scr[c * chunk:(c + 1) * chunk, h * hd:(h + 1) * hd] = _dot3_nt(rhat[i], state[h]) + y1[i]
        state[h] = _dot3(state[h], gmat[i]) + umat[i]
    for h in range(N_RW_HEADS):
        s_scr[h] = state[h]
    st_ref[...] = s_scr[...]
    y = y_scr[...]
    ones_bd = _head_ones(D_RW)
    mean = _seg_sum(y, ones_bd) * (1.0 / hd)
    yc = y - mean
    var = _seg_sum(yc * yc, ones_bd) * (1.0 / hd)
    yn = yc * lax.rsqrt(var + RW_GN_EPS) * lnw_ref[...] + lnb_ref[...]
    y_ref[...] = (yn + x_ref[:, 6 * D_RW:7 * D_RW]) * x_ref[:, 7 * D_RW:8 * D_RW]


def rwkv_chunked(packed, state0, ln_w, ln_b, rows, chunk):
    b, s, _ = packed.shape
    const = lambda a: pl.BlockSpec(a.shape, lambda i, j: (0,) * a.ndim)
    st_spec = pl.BlockSpec((None, N_RW_HEADS, HEAD_DIM, HEAD_DIM), lambda i, j: (i, 0, 0, 0))
    return pl.pallas_call(
        functools.partial(_rw_chunk_kernel, chunk=chunk),
        grid=(b, s // rows),
        in_specs=[pl.BlockSpec((None, rows, RW_PACK), lambda i, j: (i, j, 0)), st_spec, const(ln_w), const(ln_b)],
        out_specs=[pl.BlockSpec((None, rows, D_RW), lambda i, j: (i, j, 0)), st_spec],
        out_shape=[jax.ShapeDtypeStruct((b, s, D_RW), F32),
                   jax.ShapeDtypeStruct((b, N_RW_HEADS, HEAD_DIM, HEAD_DIM), F32)],
        scratch_shapes=[pltpu.VMEM((N_RW_HEADS, HEAD_DIM, HEAD_DIM), F32), pltpu.VMEM((rows, D_RW), F32)],
        compiler_params=_cparams(("arbitrary", "arbitrary")),
        name="rwkv_chunked",
    )(packed, state0, ln_w, ln_b)


def rwkv_lowrank_matrix(w2, a2, g2):
    z = lambda r: jnp.zeros((r, D_RW), F32)
    return jnp.concatenate([
        jnp.concatenate([w2, z(RW_W_RANK), z(RW_W_RANK)], axis=1),
        jnp.concatenate([z(RW_A_RANK), a2, z(RW_A_RANK)], axis=1),
        jnp.concatenate([z(RW_G_RANK), z(RW_G_RANK), g2], axis=1)], axis=0)


def _float_key(x):
    b = lax.bitcast_convert_type(x, I32)
    return jnp.where(b < 0, jnp.int32(INT_MIN) - b, b)


def _py_key(v):
    import numpy as np
    b = int(np.float32(v).view(np.int32))
    return -(b & 0x7FFFFFFF) if b < 0 else b


KEY_NEG_INF = _py_key(NEG_INF)


def _count_ge(sc_ref, nc, cand):
    parts = [jnp.where(sc_ref[c] >= cand, 1.0, 0.0) for c in range(nc)]
    while len(parts) > 1:
        parts = [parts[i] + parts[i + 1] for i in range(0, len(parts) - 1, 2)] + (parts[-1:] if len(parts) % 2 else [])
    return jnp.sum(parts[0], axis=1, keepdims=True)


def _select_threshold(sc_ref, nc, k):
    _, r, wc = sc_ref.shape

    def bit_body(i, carry):
        t, cge = carry
        cand = t + jnp.left_shift(jnp.int32(1), 31 - i)
        cnt = _count_ge(sc_ref, nc, cand)
        ok = cnt >= k
        return jnp.where(ok, cand, t), jnp.where(ok, cnt, cge)

    t0 = jnp.full((r, 1), INT_MIN, I32)
    cge0 = jnp.zeros((r, 1), F32) + jnp.asarray(nc * wc, F32)
    return lax.fori_loop(0, 32, bit_body, (t0, cge0))


def _resolve_ties(sc_ref, nc, t, need):
    _, r, wc = sc_ref.shape
    upper = jnp.where(lax.broadcasted_iota(I32, (wc, wc), 0) < lax.broadcasted_iota(I32, (wc, wc), 1), 1.0, 0.0)

    def body(c, offs):
        x = sc_ref[c]
        e = x == t
        ef = jnp.where(e, 1.0, 0.0)
        rank = _dot(ef, upper) + offs
        sc_ref[c] = jnp.where(e, jnp.where(rank >= need, jnp.int32(INT_MIN), x), x)
        return offs + jnp.sum(ef, axis=1, keepdims=True)

    lax.fori_loop(0, nc, body, jnp.zeros((r, 1), F32))


def _topk_mask_prepare(sc_ref, nc, k, valid_rows):
    t, cge = _select_threshold(sc_ref, nc, k)
    r = t.shape[0]
    live = lax.broadcasted_iota(I32, (r, 1), 0) < valid_rows
    tied = live & (cge > k) & (t != KEY_NEG_INF)

    @pl.when(jnp.max(jnp.where(tied, 1.0, 0.0)) > 0.0)
    def _():
        cgt = _count_ge(sc_ref, nc, t + 1)
        _resolve_ties(sc_ref, nc, t, k - cgt)

    return t


def _indexer_total(s4, wcols, rows):
    tot = jnp.maximum(s4[:rows], 0.0) * wcols[0]
    for h in range(1, N_IDX_HEADS):
        tot = tot + jnp.maximum(s4[h * rows:(h + 1) * rows], 0.0) * wcols[h]
    return tot


SUBLANES = 8


def _col_tree(x, op, rows=SUBLANES):
    parts = [x[i:i + rows] for i in range(0, x.shape[0], rows)]
    while len(parts) > 1:
        parts = [op(parts[i], parts[i + 1]) for i in range(0, len(parts) - 1, 2)] + (parts[-1:] if len(parts) % 2 else [])
    return parts[0]


def _col_sum(x):
    return _col_tree(x, jnp.add)


def _col_max(x):
    return _col_tree(x, jnp.maximum)


def _count_ge_t(sc_ref, nc, cand):
    _, wc, nq = sc_ref.shape
    hits = lambda c: _col_sum(jnp.where(sc_ref[c] >= cand, 1.0, 0.0))

    def pair_body(i, acc):
        return acc + (hits(2 * i) + hits(2 * i + 1))

    acc = lax.fori_loop(0, nc // 2, pair_body, jnp.zeros((SUBLANES, nq), F32))
    acc = lax.cond(nc % 2 == 1, lambda a: a + hits(nc - 1), lambda a: a, acc)
    return jnp.sum(acc, axis=0, keepdims=True)


def _topk_mask_prepare_t(sc_ref, nc, k):
    _, wc, nq = sc_ref.shape

    def bit_body(i, carry):
        t, cge = carry
        cand = t + jnp.left_shift(jnp.int32(1), 31 - i)
        cnt = _count_ge_t(sc_ref, nc, cand)
        ok = cnt >= k
        return jnp.where(ok, cand, t), jnp.where(ok, cnt, cge)

    t0 = jnp.full((1, nq), INT_MIN, I32)
    cge0 = jnp.zeros((1, nq), F32) + jnp.asarray(nc * wc, F32)
    t, cge = lax.fori_loop(0, 32, bit_body, (t0, cge0))
    tied = (cge > k) & (t != KEY_NEG_INF)

    @pl.when(jnp.max(jnp.where(tied, 1.0, 0.0)) > 0.0)
    def _():
        need = k - _count_ge_t(sc_ref, nc, t + 1)
        ri = lax.broadcasted_iota(I32, (wc, wc), 0)
        ci = lax.broadcasted_iota(I32, (wc, wc), 1)
        lower = jnp.where(ci < ri, 1.0, 0.0)

        def body(c, offs):
            x = sc_ref[c]
            e = x == t
            ef = jnp.where(e, 1.0, 0.0)
            rank = _dot(lower, ef) + offs
            sc_ref[c] = jnp.where(e, jnp.where(rank >= need, jnp.int32(INT_MIN), x), x)
            return offs + jnp.sum(ef, axis=0, keepdims=True)

        lax.fori_loop(0, nc, body, jnp.zeros((1, nq), F32))

    return t


def _dsa_prompt_kernel(q_ref, kb_ref, vt_ref, qi3_ref, ki3_ref, kiw_ref, o_ref, sc, s_even, s_odd, *, k_sel):
    j = pl.program_id(1)
    qb, wc = Q_BLOCK, KEY_CHUNK
    nck = (j * qb + qb + wc - 1) // wc
    qpos = j * qb + lax.broadcasted_iota(I32, (1, qb), 1)
    keybase = lax.broadcasted_iota(I32, (wc, 1), 0)

    lhs = jnp.concatenate([qi3_ref[:, 2 * LANES * h:2 * LANES * (h + 1)] for h in range(N_IDX_HEADS)], axis=0)
    wrows = [kiw_ref[IDX_DIM + h:IDX_DIM + h + 1, :] for h in range(N_IDX_HEADS)]

    def score_chunk(c, diagonal):
        kc = ki3_ref[pl.ds(pl.multiple_of(c * wc, wc), wc), :]
        s4 = _dot_nt(kc, lhs)
        tot = jnp.maximum(s4[:, :qb], 0.0) * wrows[0]
        for h in range(1, N_IDX_HEADS):
            tot = tot + jnp.maximum(s4[:, h * qb:(h + 1) * qb], 0.0) * wrows[h]
        if diagonal:
            tot = jnp.where(c * wc + keybase <= qpos, tot, NEG_INF)
        sc[c] = _float_key(tot)

    def score_body(c, carry):
        score_chunk(c, False)
        return carry

    lax.fori_loop(0, nck - 1, score_body, 0)
    score_chunk(nck - 1, True)
    t = _topk_mask_prepare_t(sc, nck, k_sel)
    t_sel = jnp.maximum(t, KEY_NEG_INF + 1)

    lane = lax.broadcasted_iota(I32, (qb, LANES), 1)
    n_pairs = N_ATT_HEADS // 2
    qpair = []
    for pr in range(n_pairs):
        blk = q_ref[:, pr * LANES:(pr + 1) * LANES]
        zero = jnp.zeros_like(blk)
        qpair.append(jnp.concatenate([jnp.where(lane < HEAD_DIM, blk, zero), jnp.where(lane < HEAD_DIM, zero, blk)],
                                     axis=0))

    ac = ATT_CHUNK
    per = wc // ac
    n_att = (j * qb + qb + ac - 1) // ac

    def scores_to(c, buf):
        off = pl.multiple_of(c * ac, ac)
        keys = sc[c // per, pl.ds(pl.multiple_of((c % per) * ac, ac), ac), :]
        bias = jnp.where(keys >= t_sel, 0.0, NEG_INF)
        bias2 = jnp.concatenate([bias, bias], axis=1)
        for pr in range(n_pairs):
            buf[pr] = _dot_nt(kb_ref[pl.ds(off, ac), pr * LANES:(pr + 1) * LANES], qpair[pr]) + bias2

    def consume(c, buf, carry):
        ms, ls, accs = carry
        off = pl.multiple_of(c * ac, ac)
        ss = [buf[pr] for pr in range(n_pairs)]
        new_m = [jnp.maximum(ms[pr], jnp.max(_col_max(ss[pr]), axis=0, keepdims=True)) for pr in range(n_pairs)]
        alphas = [jnp.exp2(ms[pr] - new_m[pr]) for pr in range(n_pairs)]
        ps = [jnp.exp2(ss[pr] - new_m[pr]) for pr in range(n_pairs)]
        new_l = [alphas[pr] * ls[pr] + jnp.sum(_col_sum(ps[pr]), axis=0, keepdims=True) for pr in range(n_pairs)]
        new_acc = [alphas[pr] * accs[pr]
                   + _dot(vt_ref[pr * LANES:(pr + 1) * LANES, pl.ds(off, ac)], ps[pr].astype(BF16))
                   for pr in range(n_pairs)]
        return tuple(new_m), tuple(new_l), tuple(new_acc)

    def att_body(i, carry):
        scores_to(2 * i + 1, s_odd)
        carry = consume(2 * i, s_even, carry)
        scores_to(jnp.minimum(2 * i + 2, n_att - 1), s_even)
        return consume(2 * i + 1, s_odd, carry)

    init = (tuple(jnp.full((1, 2 * qb), NEG_INF, F32) for _ in range(n_pairs)),
            tuple(jnp.zeros((1, 2 * qb), F32) for _ in range(n_pairs)),
            tuple(jnp.zeros((LANES, 2 * qb), F32) for _ in range(n_pairs)))
    scores_to(0, s_even)
    carry = lax.fori_loop(0, n_att // 2, att_body, init)
    _, ls, accs = lax.cond(n_att % 2 == 1, lambda cr: consume(n_att - 1, s_even, cr), lambda cr: cr, carry)
    sub = lax.broadcasted_iota(I32, (LANES, qb), 0)
    for pr in range(n_pairs):
        o2 = accs[pr] / ls[pr]
        o_ref[:, pr * LANES:(pr + 1) * LANES] = jnp.where(sub < HEAD_DIM, o2[:, :qb], o2[:, qb:]).T


def dsa_prompt(q, kb, vt, qi3, ki3, kiw_t):
    b, s, _ = q.shape
    k_sel = min(TOPK_MAX, s // 4)
    blk = lambda w: pl.BlockSpec((None, Q_BLOCK, w), lambda i, j: (i, j, 0))
    full = lambda w: pl.BlockSpec((None, s, w), lambda i, j: (i, 0, 0))
    return pl.pallas_call(
        functools.partial(_dsa_prompt_kernel, k_sel=k_sel),
        grid=(b, s // Q_BLOCK),
        in_specs=[blk(D_ATT), full(D_ATT), pl.BlockSpec((None, D_ATT, s), lambda i, j: (i, 0, 0)),
                  blk(N_IDX_HEADS * 2 * LANES), full(2 * LANES),
                  pl.BlockSpec((None, LANES, Q_BLOCK), lambda i, j: (i, 0, j))],
        out_specs=blk(D_ATT),
        out_shape=jax.ShapeDtypeStruct((b, s, D_ATT), F32),
        scratch_shapes=[pltpu.VMEM((s // KEY_CHUNK, KEY_CHUNK, Q_BLOCK), I32),
                        pltpu.VMEM((N_ATT_HEADS // 2, ATT_CHUNK, 2 * Q_BLOCK), F32),
                        pltpu.VMEM((N_ATT_HEADS // 2, ATT_CHUNK, 2 * Q_BLOCK), F32)],
        compiler_params=_cparams(("arbitrary", "arbitrary")),
        name="dsa_prompt",
    )(q, kb, vt, qi3, ki3, kiw_t)


SAMPLE_ROWS = 8
PAGE_GROUP = 16
IDX_PAGE_GROUP = 32


def _dsa_sample_index_kernel(pt_ref, *refs, n_valid, k_sel):
    pages = refs[:IDX_PAGE_GROUP]
    qi3_ref, kiw_ref, ki3n_ref, sc_ref, thr_ref = refs[IDX_PAGE_GROUP:]
    g = pl.program_id(1)
    r = SAMPLE_ROWS
    n_chunks = sc_ref.shape[0]
    q3 = qi3_ref[...].astype(F32)
    qhi = jnp.concatenate([q3[:, 2 * LANES * h:2 * LANES * h + IDX_DIM] for h in range(N_IDX_HEADS)],
                          axis=0).astype(BF16)
    qlo = jnp.concatenate([q3[:, 2 * LANES * h + LANES:2 * LANES * h + LANES + IDX_DIM]
                           for h in range(N_IDX_HEADS)], axis=0).astype(BF16)
    wcols = [kiw_ref[:, IDX_DIM + h:IDX_DIM + h + 1] for h in range(N_IDX_HEADS)]

    def scores(kt):
        khi, klo = _split(kt)
        return _dot(qhi, khi) + (_dot(qhi, klo) + _dot(qlo, khi))

    for i in range(IDX_PAGE_GROUP):
        sc_ref[g * IDX_PAGE_GROUP + i] = _float_key(_indexer_total(scores(pages[i][...]), wcols, r))

    @pl.when(g == pl.num_programs(1) - 1)
    def _():
        tot = _indexer_total(scores(ki3n_ref[...]), wcols, r)
        row = lax.broadcasted_iota(I32, tot.shape, 0)
        col = lax.broadcasted_iota(I32, tot.shape, 1)
        tot = jnp.where((col <= row) & (col < n_valid), tot, NEG_INF)
        sc_ref[n_chunks - 1] = _float_key(tot)
        t = _topk_mask_prepare(sc_ref, n_chunks, k_sel, n_valid)
        thr_ref[...] = jnp.broadcast_to(t, thr_ref.shape)


def dsa_sample_index(page_table, cache_kidx_t, layer, qi3, kiw, ki_new_t, n_valid):
    b, n_pages = page_table.shape
    n_chunks = n_pages + 1
    k_sel = min(TOPK_MAX, (n_pages * PAGE_SIZE + n_valid) // 4)
    page_spec = lambda i: pl.BlockSpec((None, None, IDX_DIM, PAGE_SIZE),
                                       lambda bi, g, pt: (layer, pt[bi, g * IDX_PAGE_GROUP + i], 0, 0))
    per_b = lambda shape: pl.BlockSpec((None,) + shape, lambda bi, g, pt: (bi,) + (0,) * len(shape))
    r = SAMPLE_ROWS
    return pl.pallas_call(
        functools.partial(_dsa_sample_index_kernel, n_valid=n_valid, k_sel=k_sel),
        grid_spec=pltpu.PrefetchScalarGridSpec(
            num_scalar_prefetch=1,
            grid=(b, n_pages // IDX_PAGE_GROUP),
            in_specs=[page_spec(i) for i in range(IDX_PAGE_GROUP)]
            + [per_b((r, N_IDX_HEADS * 2 * LANES)), per_b((r, LANES)), per_b((IDX_DIM, PAGE_SIZE))],
            out_specs=[per_b((n_chunks, r, PAGE_SIZE)), per_b((r, PAGE_SIZE))],
        ),
        out_shape=[jax.ShapeDtypeStruct((b, n_chunks, r, PAGE_SIZE), I32),
                   jax.ShapeDtypeStruct((b, r, PAGE_SIZE), I32)],
        compiler_params=_cparams(("arbitrary", "arbitrary")),
        name="dsa_sample_index",
    )(page_table, *([cache_kidx_t] * IDX_PAGE_GROUP), qi3, kiw, ki_new_t)


def _dsa_sample_attend_kernel(pt_ref, *refs, n_valid):
    kpages = refs[:PAGE_GROUP]
    vpages = refs[PAGE_GROUP:2 * PAGE_GROUP]
    qbd_ref, sel_ref, seln_ref, thr_ref, kn_ref, vn_ref, o_ref, m_scr, l_scr, acc_scr = refs[2 * PAGE_GROUP:]
    g = pl.program_id(1)
    r = SAMPLE_ROWS

    @pl.when(g == 0)
    def _():
        m_scr[...] = jnp.full(m_scr.shape, NEG_INF, F32)
        l_scr[...] = jnp.zeros(l_scr.shape, F32)
        acc_scr[...] = jnp.zeros(acc_scr.shape, F32)

    qbd = qbd_ref[...]
    thr = thr_ref[...]
    per_head = lambda x: jnp.concatenate([x] * N_ATT_HEADS, axis=0)
    thr_all = per_head(thr)

    def update(sels, extra, kts, vts):
        n = len(kts)
        scores = jnp.concatenate([_dot(qbd, kt) for kt in kts], axis=1)
        mask = jnp.concatenate([per_head(sel) for sel in sels], axis=1) >= jnp.concatenate([thr_all] * n, axis=1)
        if extra is not None:
            mask = mask & (per_head(extra) > 0)
        s = jnp.where(mask, scores, NEG_INF)
        m_prev = m_scr[...]
        m_new = jnp.maximum(m_prev, jnp.max(s, axis=1, keepdims=True))
        alpha = jnp.exp(m_prev - m_new)
        p = jnp.exp(s - m_new)
        l_scr[...] = alpha * l_scr[...] + jnp.sum(p, axis=1, keepdims=True)
        p = p.astype(BF16)
        pv = _dot_nt(p[:, :PAGE_SIZE], vts[0])
        for i in range(1, n):
            pv = pv + _dot_nt(p[:, i * PAGE_SIZE:(i + 1) * PAGE_SIZE], vts[i])
        acc_scr[...] = alpha * acc_scr[...] + pv
        m_scr[...] = m_new

    update([sel_ref[i] for i in range(PAGE_GROUP)], None,
           [kpages[i][...].astype(BF16) for i in range(PAGE_GROUP)],
           [vpages[i][...].astype(BF16) for i in range(PAGE_GROUP)])

    @pl.when(g == pl.num_programs(1) - 1)
    def _():
        row = lax.broadcasted_iota(I32, thr.shape, 0)
        col = lax.broadcasted_iota(I32, thr.shape, 1)
        causal = jnp.where((col <= row) & (col < n_valid), 1, 0)
        update([seln_ref[...]], causal, [kn_ref[...]], [vn_ref[...]])
        out = acc_scr[...] / l_scr[...]
        lane = lax.broadcasted_iota(I32, (r, D_ATT), 1) // HEAD_DIM
        tot = jnp.where(lane == 0, out[:r], 0.0)
        for h in range(1, N_ATT_HEADS):
            tot = tot + jnp.where(lane == h, out[h * r:(h + 1) * r], 0.0)
        o_ref[...] = tot


def dsa_sample_attend(page_table, cache_kt, cache_vt, layer, q_bd, sel, thr, k_new_t, v_new_t, n_valid):
    b, n_pages = page_table.shape
    r = SAMPLE_ROWS
    page_spec = lambda i: pl.BlockSpec((None, None, D_ATT, PAGE_SIZE),
                                       lambda bi, g, pt: (layer, pt[bi, g * PAGE_GROUP + i], 0, 0))
    per_b = lambda shape: pl.BlockSpec((None,) + shape, lambda bi, g, pt: (bi,) + (0,) * len(shape))
    rows = N_ATT_HEADS * r
    return pl.pallas_call(
        functools.partial(_dsa_sample_attend_kernel, n_valid=n_valid),
        grid_spec=pltpu.PrefetchScalarGridSpec(
            num_scalar_prefetch=1,
            grid=(b, n_pages // PAGE_GROUP),
            in_specs=[page_spec(i) for i in range(PAGE_GROUP)] + [page_spec(i) for i in range(PAGE_GROUP)]
            + [per_b((rows, D_ATT)),
               pl.BlockSpec((None, PAGE_GROUP, r, PAGE_SIZE), lambda bi, g, pt: (bi, g, 0, 0)),
               pl.BlockSpec((None, None, r, PAGE_SIZE), lambda bi, g, pt: (bi, n_pages, 0, 0)),
               per_b((r, PAGE_SIZE)), per_b((D_ATT, PAGE_SIZE)), per_b((D_ATT, PAGE_SIZE))],
            out_specs=per_b((r, D_ATT)),
            scratch_shapes=[pltpu.VMEM((rows, 1), F32), pltpu.VMEM((rows, 1), F32), pltpu.VMEM((rows, D_ATT), F32)],
        ),
        out_shape=jax.ShapeDtypeStruct((b, r, D_ATT), F32),
        compiler_params=_cparams(("arbitrary", "arbitrary")),
        name="dsa_sample_attend",
    )(page_table, *([cache_kt] * PAGE_GROUP), *([cache_vt] * PAGE_GROUP), q_bd, sel, sel, thr, k_new_t, v_new_t)


def pack_w_in(w_in):
    d = w_in.shape[0]
    pad = jnp.zeros((d, ZP_END - ZP_KIW - (OFF_RW - OFF_KI)), w_in.dtype)
    return jnp.concatenate([w_in[:, OFF_S5:OFF_KI], w_in[:, OFF_RW:], w_in[:, OFF_KI:OFF_RW], pad], axis=1).astype(BF16)


def rope_tables(pos):
    half = HEAD_DIM // 2
    inv = ROPE_THETA ** (-jnp.arange(half, dtype=F32) / half)
    ang = pos.astype(F32)[:, None] * inv[None, :]
    cos, sin = jnp.cos(ang), jnp.sin(ang)
    return jnp.tile(jnp.concatenate([cos, cos], 1), (1, 2)), jnp.tile(jnp.concatenate([-sin, sin], 1), (1, 2))


def block_diag_queries(q):
    b, r, d = q.shape
    head_of_lane = jnp.arange(d) // HEAD_DIM
    keep = head_of_lane[None, :] == jnp.arange(N_ATT_HEADS)[:, None]
    return jnp.where(keep[None, :, None, :], q[:, None], jnp.zeros((), q.dtype)).reshape(b, N_ATT_HEADS * r, d)


def _pad_rows(a, rows):
    return jnp.pad(a, ((0, 0), (0, rows - a.shape[1])) + ((0, 0),) * (a.ndim - 2))


Q_SCALE = HEAD_DIM ** -0.5
Q_SCALE_EXP2 = Q_SCALE * math.log2(math.e)
PROMPT_TILE = 256
RW_PRE_TILE = 512
RW_STEP_ROWS = 256


def kernel(x_prompt, x_sample, c_prompt, c_sample, cache_k, cache_v, cache_kidx, state_s5_re, state_s5_im, state_rwkv, state_rwkv_shift, page_table, ada_w, ada_b, norm_pre, norm_post, ffn_wi, ffn_wo, w_in, w_out, s5_a_re, s5_a_im, s5_log_dt, s5_b_re, s5_b_im, s5_c_re, s5_c_im, s5_d, s5_glu_w, s5_glu_b, rw_mu, rw_w0, rw_w2, rw_a0, rw_a2, rw_g2, rw_k_k, rw_k_a, rw_r_k, rw_ln_w, rw_ln_b):
    bp, sp, d = x_prompt.shape
    bs, ss, _ = x_sample.shape
    depth = ada_w.shape[0]
    past = page_table.shape[1] * PAGE_SIZE
    np_tok, ns_tok = bp * sp, bs * ss
    tpb = sp // PROMPT_TILE
    row1 = lambda a: a.reshape(1, -1)

    c_all = _pad_rows(jnp.concatenate([c_prompt, c_sample], axis=0)[None], -(-(bp + bs) // 8) * 8)[0]
    mod = ada_mod(c_all, ada_w, ada_b).reshape(depth, c_all.shape[0], N_SUB, 3, d)

    cos_p, sin_p = rope_tables(jnp.arange(sp))
    cos_s, sin_s = (jnp.tile(t, (bs, 1)) for t in rope_tables(past + jnp.arange(ss)))
    n_pool = cache_k.shape[1]
    cache_kt = jnp.transpose(cache_k, (0, 1, 3, 4, 2)).reshape(depth, n_pool, D_ATT, PAGE_SIZE)
    cache_vt = jnp.transpose(cache_v, (0, 1, 3, 4, 2)).reshape(depth, n_pool, D_ATT, PAGE_SIZE)
    cache_kidx_t = jnp.swapaxes(cache_kidx, 2, 3)

    xp = x_prompt.reshape(np_tok, d)
    xs = x_sample.reshape(ns_tok, d)
    outs_p, outs_s = [], []
    for l in range(depth):
        mod_p = mod[l, :bp]
        mod_s = jnp.repeat(mod[l, bp:bp + bs], ss, axis=0)
        pm = lambda i, j: mod_p[:, i, j][:, None, :]
        sm = lambda i, j: mod_s[:, i, j][None]
        npre = lambda i: row1(norm_pre[l, i])
        npost = lambda i: row1(norm_post[l, i])
        wi0, wo0 = ffn_wi[l, 0].astype(BF16), ffn_wo[l, 0].astype(BF16)
        wi1, wo1 = ffn_wi[l, 1].astype(BF16), ffn_wo[l, 1].astype(BF16)
        w_in_p = pack_w_in(w_in[l])
        w_out_b = w_out[l].astype(BF16)
        powre, powim, bb_re, bb_im = s5_discretise(s5_a_re[l], s5_a_im[l], s5_log_dt[l], s5_b_re[l], s5_b_im[l],
                                                   rows=S5_CHUNK)
        wb, wc = s5_matrices(bb_re, bb_im, s5_c_re[l], s5_c_im[l])
        s5_tail = (powre, powim, wb, wc, row1(s5_d[l]), s5_glu_w[l].astype(BF16), row1(s5_glu_b[l]))
        wlr = rwkv_lowrank_matrix(rw_w2[l], rw_a2[l], rw_g2[l])
        rw_pre = (row1(rw_mu[l]), row1(rw_w0[l]), row1(rw_a0[l]), wlr, row1(rw_k_k[l]), row1(rw_k_a[l]),
                  row1(rw_r_k[l]))
        ln = (row1(rw_ln_w[l]), row1(rw_ln_b[l]))

        xp = ffn_block(xp, pm(0, 0), pm(0, 1), pm(0, 2), npre(0), npost(0), wi0, wo0, PROMPT_TILE, tpb)
        (u, q, _, kb, _, _, qi3, _, ki3, rw, vt, kt_f, vt_f, kiw_t) = inproj_block(
            xp, pm(1, 0), pm(1, 1), npre(1), w_in_p, cos_p, sin_p, PROMPT_TILE, tpb, tpb, Q_SCALE_EXP2)
        seq = lambda a: a.reshape(bp, sp, a.shape[-1])
        y_s5, h_s5 = s5_prompt(seq(u), jnp.zeros((bp, 1, 2 * S5_W), F32), *s5_tail)
        y_att = dsa_prompt(seq(q), seq(kb), vt, seq(qi3), seq(ki3), kiw_t)
        cols = seq(rw)
        packed = rwkv_prepare(cols, jnp.zeros((bp, 1, N_RW_COLS), F32), *rw_pre, RW_PRE_TILE, RW_CHUNK, RW_CHUNK)
        y_rw, st_rw = rwkv_chunked(packed, jnp.zeros((bp, N_RW_HEADS, HEAD_DIM, HEAD_DIM), F32), *ln,
                                   RW_STEP_ROWS, RW_CHUNK)
        flat = lambda a: a.reshape(np_tok, a.shape[-1])
        xp = outproj_block(xp, flat(y_s5), flat(y_att), flat(y_rw), pm(1, 2), npost(1), w_out_b, PROMPT_TILE, tpb)
        xp = ffn_block(xp, pm(2, 0), pm(2, 1), pm(2, 2), npre(2), npost(2), wi1, wo1, PROMPT_TILE, tpb)
        heads_last = lambda a: jnp.transpose(a.reshape(bp, N_ATT_HEADS, HEAD_DIM, sp), (0, 3, 1, 2))
        outs_p.append((heads_last(kt_f), heads_last(vt_f), jnp.swapaxes(kiw_t[:, :IDX_DIM, :], 1, 2),
                       h_s5[:, 0, :S5_W].reshape(bp, S5_GROUPS, S5_STATE),
                       h_s5[:, 0, S5_W:].reshape(bp, S5_GROUPS, S5_STATE), st_rw, cols[:, sp - 1]))

        xs = ffn_block(xs, sm(0, 0), sm(0, 1), sm(0, 2), npre(0), npost(0), wi0, wo0, ns_tok, 1)
        u, q, k, kb, v, vb, qi3, kiw, ki3, rw = inproj_block(xs, sm(1, 0), sm(1, 1), npre(1), w_in_p, cos_s, sin_s,
                                                             ns_tok, 1, 1, Q_SCALE)[:10]
        seq = lambda a: a.reshape(bs, ss, a.shape[-1])
        h0 = jnp.concatenate([state_s5_re[l].reshape(bs, S5_W), state_s5_im[l].reshape(bs, S5_W)], axis=1)
        y_s5, h_s5 = s5_sample(jnp.swapaxes(seq(u), 0, 1), h0, *s5_tail)
        y_s5 = jnp.swapaxes(y_s5, 0, 1)
        tok_t = lambda a: jnp.swapaxes(_pad_rows(a, PAGE_SIZE), 1, 2)
        sel, thr = dsa_sample_index(page_table, cache_kidx_t, l, _pad_rows(seq(qi3), SAMPLE_ROWS),
                                    _pad_rows(seq(kiw), SAMPLE_ROWS), tok_t(seq(kiw)[:, :, :IDX_DIM]), ss)
        q_bd = block_diag_queries(_pad_rows(seq(q), SAMPLE_ROWS))
        y_att = dsa_sample_attend(page_table, cache_kt, cache_vt, l, q_bd, sel, thr, tok_t(seq(kb)), tok_t(seq(vb)),
                                  ss)[:, :ss]
        cols = seq(rw)
        packed = rwkv_prepare(_pad_rows(cols, RW_CHUNK), state_rwkv_shift[l].reshape(bs, 1, N_RW_COLS), *rw_pre,
                              RW_CHUNK, RW_CHUNK, ss)
        y_rw, st_rw = rwkv_chunked(packed, state_rwkv[l], *ln, RW_CHUNK, RW_CHUNK)
        flat = lambda a: a.reshape(ns_tok, a.shape[-1])
        xs = outproj_block(xs, flat(y_s5), flat(y_att), flat(y_rw[:, :ss]), sm(1, 2), npost(1), w_out_b, ns_tok, 1)
        xs = ffn_block(xs, sm(2, 0), sm(2, 1), sm(2, 2), npre(2), npost(2), wi1, wo1, ns_tok, 1)
        outs_s.append((k.reshape(bs, ss, N_ATT_HEADS, HEAD_DIM), v.reshape(bs, ss, N_ATT_HEADS, HEAD_DIM),
                       seq(kiw)[:, :, :IDX_DIM],
                       h_s5[:, :S5_W].reshape(bs, S5_GROUPS, S5_STATE),
                       h_s5[:, S5_W:].reshape(bs, S5_GROUPS, S5_STATE), st_rw, cols[:, ss - 1]))

    stack = lambda outs, i: jnp.stack([o[i] for o in outs])
    return ((xp.reshape(bp, sp, d), xs.reshape(bs, ss, d))
            + tuple(stack(outs_p, i) for i in range(7)) + tuple(stack(outs_s, i) for i in range(7)))
```

```python
import functools
import math

import jax
import jax.numpy as jnp
from jax import lax
from jax.experimental import pallas as pl
from jax.experimental.pallas import tpu as pltpu

F32 = jnp.float32
BF16 = jnp.bfloat16
I32 = jnp.int32

D_MODEL = 1024
PAGE_SIZE = 128
D_S5 = 256
D_ATT = 512
D_RW = 256
S5_GROUP = 16
S5_GROUPS = D_S5 // S5_GROUP
S5_STATE = 64
S5_W = S5_GROUPS * S5_STATE
HEAD_DIM = 64
N_ATT_HEADS = D_ATT // HEAD_DIM
N_IDX_HEADS = 4
IDX_DIM = 64
TOPK_MAX = 256
Q_BLOCK = 128
ROPE_THETA = 10000.0
N_RW_HEADS = D_RW // HEAD_DIM
RW_W_RANK = 32
RW_A_RANK = 32
RW_G_RANK = 64
RW_GN_EPS = 64e-5
N_RW_COLS = 3 * D_RW + RW_W_RANK + RW_A_RANK + RW_G_RANK
D_FF = 2816
HALF_STEP = 0.5
RMS_EPS = 1e-6
N_SUB = 3
NEG_INF = -1e30
OFF_S5 = 0
OFF_Q = OFF_S5 + D_S5
OFF_K = OFF_Q + D_ATT
OFF_V = OFF_K + D_ATT
OFF_QI = OFF_V + D_ATT
OFF_KI = OFF_QI + N_IDX_HEADS * IDX_DIM
OFF_WI = OFF_KI + IDX_DIM
OFF_RW = OFF_WI + N_IDX_HEADS
N_IN = OFF_RW + N_RW_COLS

LANES = 128
VMEM_LIMIT = 56 * 1024 * 1024
INT_MIN = -(2 ** 31)

ZP_U, ZP_Q, ZP_K, ZP_V, ZP_QI, ZP_RW, ZP_KIW, ZP_END = 0, 256, 768, 1280, 1792, 2048, 2944, 3072

RW_CHUNK = 64
S5_CHUNK = 256
KEY_CHUNK = 512
ATT_CHUNK = 512


def _cparams(sem):
    return pltpu.CompilerParams(dimension_semantics=sem, vmem_limit_bytes=VMEM_LIMIT)


def _dot(a, b):
    return jnp.dot(a, b, preferred_element_type=F32)


def _dot_nt(a, b):
    return lax.dot_general(a, b, (((1,), (1,)), ((), ())), preferred_element_type=F32)


def _split(x):
    hi = x.astype(BF16)
    lo = (x - hi.astype(F32)).astype(BF16)
    return hi, lo


def _dot3(a, b):
    ah, al = _split(a)
    bh, bl = _split(b)
    return _dot(ah, bh) + (_dot(ah, bl) + _dot(al, bh))


def _dot3_nt(a, b):
    ah, al = _split(a)
    bh, bl = _split(b)
    return _dot_nt(ah, bh) + (_dot_nt(ah, bl) + _dot_nt(al, bh))


def _mm3(a, b):
    return _dot(a[0], b[0]) + (_dot(a[0], b[1]) + _dot(a[1], b[0]))


def _mm3_nt(a, b):
    return _dot_nt(a[0], b[0]) + (_dot_nt(a[0], b[1]) + _dot_nt(a[1], b[0]))


def _rms(x, g):
    return x * lax.rsqrt(jnp.mean(x * x, axis=-1, keepdims=True) + RMS_EPS) * g


def _sigmoid(x):
    return 1.0 / (1.0 + jnp.exp(-x))


def _ada_kernel(c_ref, w_ref, b_ref, o_ref):
    c = c_ref[...]
    h = (c * _sigmoid(c)).astype(BF16)
    o_ref[...] = _dot(h, w_ref[...].astype(BF16)) + b_ref[...]


def ada_mod(c_all, ada_w, ada_b, tn=1152):
    depth, d, n = ada_w.shape
    rows = c_all.shape[0]
    return pl.pallas_call(
        _ada_kernel,
        grid=(depth, n // tn),
        in_specs=[pl.BlockSpec((rows, d), lambda l, j: (0, 0)),
                  pl.BlockSpec((None, d, tn), lambda l, j: (l, 0, j)),
                  pl.BlockSpec((None, 1, tn), lambda l, j: (l, 0, j))],
        out_specs=pl.BlockSpec((None, rows, tn), lambda l, j: (l, 0, j)),
        out_shape=jax.ShapeDtypeStruct((depth, rows, n), F32),
        compiler_params=_cparams(("arbitrary", "arbitrary")),
        name="ada_mod",
    )(c_all, ada_w, ada_b.reshape(depth, 1, n))


def _mod_spec(mod, tiles_per_group):
    r = mod.shape[1]
    return pl.BlockSpec((None, r, mod.shape[2]), lambda i: (i // tiles_per_group, 0, 0))


def _ffn_kernel(x_ref, shift_ref, scale_ref, gate_ref, gpre_ref, gpost_ref, wi_ref, wo_ref, o_ref, *, res_w):
    x = x_ref[...]
    h = (_rms(x, gpre_ref[...]) * (1.0 + scale_ref[...]) + shift_ref[...]).astype(BF16)
    g = _dot(h, wi_ref[:, :D_FF])
    u = _dot(h, wi_ref[:, D_FF:])
    a = (g * _sigmoid(g) * u).astype(BF16)
    o = _dot(a, wo_ref[...])
    o_ref[...] = x + (res_w * gate_ref[...]) * _rms(o, gpost_ref[...])


def ffn_block(x, shift, scale, gate, g_pre, g_post, wi, wo, tm, tiles_per_group):
    n, d = x.shape
    row = pl.BlockSpec((tm, d), lambda i: (i, 0))
    const = lambda a: pl.BlockSpec(a.shape, lambda i: (0,) * a.ndim)
    return pl.pallas_call(
        functools.partial(_ffn_kernel, res_w=HALF_STEP),
        grid=(n // tm,),
        in_specs=[row, _mod_spec(shift, tiles_per_group), _mod_spec(scale, tiles_per_group),
                  _mod_spec(gate, tiles_per_group), const(g_pre), const(g_post), const(wi), const(wo)],
        out_specs=row,
        out_shape=jax.ShapeDtypeStruct((n, d), F32),
        compiler_params=_cparams(("arbitrary",)),
        name="ffn_block",
    )(x, shift, scale, gate, g_pre, g_post, wi, wo)


def _rot_block(blk, cos, sin, lane):
    partner = jnp.where((lane & 32) == 0, pltpu.roll(blk, LANES - 32, 1), pltpu.roll(blk, 32, 1))
    return blk * cos + partner * sin


def _inproj_kernel(x_ref, shift_ref, scale_ref, gpre_ref, w_ref, cos_ref, sin_ref,
                   u_ref, q_ref, k_ref, kb_ref, v_ref, vb_ref, qi3_ref, kiw_ref, ki3_ref, rw_ref,
                   vt_ref, ktf_ref, vtf_ref, kiwt_ref, *, q_scale):
    x = x_ref[...]
    h = (_rms(x, gpre_ref[...]) * (1.0 + scale_ref[...]) + shift_ref[...]).astype(BF16)
    z = _dot(h, w_ref[...])
    cos = cos_ref[...]
    sin = sin_ref[...]
    lane = lax.broadcasted_iota(I32, cos.shape, 1)
    lo_half = lane < 64
    rot = lambda off: _rot_block(z[:, off:off + LANES], cos, sin, lane)

    u_ref[...] = z[:, ZP_U:ZP_Q]
    for j in range(D_ATT // LANES):
        q_ref[:, j * LANES:(j + 1) * LANES] = (rot(ZP_Q + j * LANES) * q_scale).astype(BF16)
        kr = rot(ZP_K + j * LANES)
        k_ref[:, j * LANES:(j + 1) * LANES] = kr
        kb_ref[:, j * LANES:(j + 1) * LANES] = kr.astype(BF16)
        ktf_ref[j * LANES:(j + 1) * LANES, :] = kr.T
    v = z[:, ZP_V:ZP_QI]
    v_ref[...] = v
    vb_ref[...] = v.astype(BF16)
    vt = v.T
    vtf_ref[...] = vt
    vt_ref[...] = vt.astype(BF16)
    for j in range(N_IDX_HEADS * IDX_DIM // LANES):
        qr = rot(ZP_QI + j * LANES)
        hi = qr.astype(BF16).astype(F32)
        lo = qr - hi
        hi_sw = pltpu.roll(hi, 64, 1)
        lo_sw = pltpu.roll(lo, 64, 1)
        zero = jnp.zeros_like(hi)
        for half in range(2):
            base = (2 * j + half) * 2 * LANES
            a, b = (hi, lo) if half == 0 else (hi_sw, lo_sw)
            a_sw = hi_sw if half == 0 else hi
            qi3_ref[:, base:base + LANES] = jnp.where(lo_half, a, a_sw).astype(BF16)
            qi3_ref[:, base + LANES:base + 2 * LANES] = jnp.where(lo_half, b, zero).astype(BF16)
    raw = z[:, ZP_KIW:ZP_END]
    kiw = jnp.where(lo_half, _rot_block(raw, cos, sin, lane), raw)
    kiw_ref[...] = kiw
    kiwt_ref[...] = kiw.T
    hi = kiw.astype(BF16).astype(F32)
    lo = kiw - hi
    ki3_ref[:, :LANES] = jnp.where(lo_half, hi, pltpu.roll(lo, 64, 1)).astype(BF16)
    ki3_ref[:, LANES:] = jnp.where(lo_half, hi, jnp.zeros_like(hi)).astype(BF16)
    rw_ref[...] = z[:, ZP_RW:ZP_KIW]


def inproj_block(x, shift, scale, g_pre, w, cos_t, sin_t, tm, tiles_per_group, pos_tiles, q_scale):
    n, d = x.shape
    row = lambda width: pl.BlockSpec((tm, width), lambda i: (i, 0))
    const = lambda a: pl.BlockSpec(a.shape, lambda i: (0,) * a.ndim)
    tab = pl.BlockSpec((tm, LANES), lambda i: (i % pos_tiles, 0))
    widths = [(D_S5, F32), (D_ATT, BF16), (D_ATT, F32), (D_ATT, BF16), (D_ATT, F32), (D_ATT, BF16),
              (N_IDX_HEADS * 2 * LANES, BF16), (LANES, F32), (2 * LANES, BF16), (N_RW_COLS, F32)]
    t_widths = [(D_ATT, BF16), (D_ATT, F32), (D_ATT, F32), (LANES, F32)]
    return pl.pallas_call(
        functools.partial(_inproj_kernel, q_scale=q_scale),
        grid=(n // tm,),
        in_specs=[row(d), _mod_spec(shift, tiles_per_group), _mod_spec(scale, tiles_per_group),
                  const(g_pre), const(w), tab, tab],
        out_specs=[row(wd) for wd, _ in widths]
        + [pl.BlockSpec((None, wd, tm), lambda i: (i // pos_tiles, 0, i % pos_tiles)) for wd, _ in t_widths],
        out_shape=[jax.ShapeDtypeStruct((n, wd), dt) for wd, dt in widths]
        + [jax.ShapeDtypeStruct((n // (pos_tiles * tm), wd, pos_tiles * tm), dt) for wd, dt in t_widths],
        compiler_params=_cparams(("arbitrary",)),
        name="inproj_block",
    )(x, shift, scale, g_pre, w, cos_t, sin_t)


def _outproj_kernel(x_ref, ys5_ref, yatt_ref, yrw_ref, gate_ref, gpost_ref, w_ref, o_ref):
    o = (_dot(ys5_ref[...].astype(BF16), w_ref[:D_S5, :])
         + _dot(yatt_ref[...].astype(BF16), w_ref[D_S5:D_S5 + D_ATT, :])
         + _dot(yrw_ref[...].astype(BF16), w_ref[D_S5 + D_ATT:, :]))
    o_ref[...] = x_ref[...] + gate_ref[...] * _rms(o, gpost_ref[...])


def outproj_block(x, y_s5, y_att, y_rw, gate, g_post, w, tm, tiles_per_group):
    n, d = x.shape
    row = lambda width: pl.BlockSpec((tm, width), lambda i: (i, 0))
    const = lambda a: pl.BlockSpec(a.shape, lambda i: (0,) * a.ndim)
    return pl.pallas_call(
        _outproj_kernel,
        grid=(n // tm,),
        in_specs=[row(d), row(D_S5), row(D_ATT), row(D_RW), _mod_spec(gate, tiles_per_group),
                  const(g_post), const(w)],
        out_specs=row(d),
        out_shape=jax.ShapeDtypeStruct((n, d), F32),
        compiler_params=_cparams(("arbitrary",)),
        name="outproj_block",
    )(x, y_s5, y_att, y_rw, gate, g_post, w)


def _gelu_tanh(x):
    return 0.5 * x * (1.0 + jnp.tanh(math.sqrt(2.0 / math.pi) * (x + 0.044715 * (x * x * x))))


def _s5_disc_kernel(are_ref, aim_ref, ldt_ref, arec_ref, aimc_ref, ldtc_ref, bre_ref, bim_ref,
                    powre_ref, powim_ref, bbre_ref, bbim_ref, *, rows):
    def zoh(ar, ai, ldt):
        dt = jnp.exp(ldt)
        mag = jnp.exp(ar * dt)
        abr, abi = mag * jnp.cos(ai * dt), mag * jnp.sin(ai * dt)
        den = ar * ar + ai * ai
        nr, ni = abr - 1.0, abi
        return abr, abi, (nr * ar + ni * ai) / den, (ni * ar - nr * ai) / den

    abr, abi, _, _ = zoh(are_ref[...], aim_ref[...], ldt_ref[...])
    pr = jnp.broadcast_to(abr, (rows, abr.shape[1]))
    pi = jnp.broadcast_to(abi, (rows, abr.shape[1]))
    row = lax.broadcasted_iota(I32, pr.shape, 0)
    d = 1
    while d < rows:
        sr = pltpu.roll(pr, d, 0)
        si = pltpu.roll(pi, d, 0)
        m = row >= d
        pr, pi = jnp.where(m, pr * sr - pi * si, pr), jnp.where(m, pr * si + pi * sr, pi)
        d *= 2
    powre_ref[...] = pr
    powim_ref[...] = pi
    _, _, cr, ci = zoh(arec_ref[...], aimc_ref[...], ldtc_ref[...])
    br, bi = bre_ref[...], bim_ref[...]
    bbre_ref[...] = cr * br - ci * bi
    bbim_ref[...] = cr * bi + ci * br


def s5_discretise(a_re, a_im, log_dt, b_re, b_im, rows):
    g, p = a_re.shape
    w = g * p
    ldt = jnp.broadcast_to(log_dt[:, None], (g, p))
    args = (a_re.reshape(1, w), a_im.reshape(1, w), ldt.reshape(1, w),
            a_re.reshape(w, 1), a_im.reshape(w, 1), ldt.reshape(w, 1),
            b_re.reshape(w, S5_GROUP), b_im.reshape(w, S5_GROUP))
    return pl.pallas_call(
        functools.partial(_s5_disc_kernel, rows=rows),
        out_shape=[jax.ShapeDtypeStruct((rows, w), F32), jax.ShapeDtypeStruct((rows, w), F32),
                   jax.ShapeDtypeStruct((w, S5_GROUP), F32), jax.ShapeDtypeStruct((w, S5_GROUP), F32)],
        name="s5_discretise",
    )(*args)


def _s5_head(y, u, d_ref, gluw_ref, glub_ref):
    y = _gelu_tanh(y + d_ref[...] * u)
    return y * _sigmoid(_dot(y.astype(BF16), gluw_ref[...]) + glub_ref[...])


def _s5_kernel(u_ref, h0_ref, powre_ref, powim_ref, wb_ref, wc_ref, d_ref, gluw_ref, glub_ref,
               y_ref, ht_ref, cre, cim):
    t = pl.program_id(1)
    rows = u_ref.shape[0]

    @pl.when(t == 0)
    def _():
        cre[...] = h0_ref[:, :S5_W]
        cim[...] = h0_ref[:, S5_W:]

    u = u_ref[...]
    bu = _dot(u.astype(BF16), wb_ref[...])
    hr, hi = bu[:, :S5_W], bu[:, S5_W:]
    row = lax.broadcasted_iota(I32, hr.shape, 0)
    d = 1
    while d < rows:
        ar, ai = powre_ref[d - 1:d, :], powim_ref[d - 1:d, :]
        sr, si = pltpu.roll(hr, d, 0), pltpu.roll(hi, d, 0)
        m = row >= d
        hr, hi = (hr + jnp.where(m, ar * sr - ai * si, 0.0), hi + jnp.where(m, ar * si + ai * sr, 0.0))
        d *= 2
    pr, pi = powre_ref[...], powim_ref[...]
    c_r, c_i = cre[...], cim[...]
    hr, hi = hr + (pr * c_r - pi * c_i), hi + (pr * c_i + pi * c_r)
    cre[...] = hr[rows - 1:rows, :]
    cim[...] = hi[rows - 1:rows, :]
    ht_ref[:, :S5_W] = hr[rows - 1:rows, :]
    ht_ref[:, S5_W:] = hi[rows - 1:rows, :]
    y = _dot(hr.astype(BF16), wc_ref[:S5_W, :]) + _dot(hi.astype(BF16), wc_ref[S5_W:, :])
    y_ref[...] = _s5_head(y, u, d_ref, gluw_ref, glub_ref)


def s5_prompt(u, h0, powre, powim, wb, wc, d, gluw, glub):
    b, s, _ = u.shape
    rows = powre.shape[0]
    const = lambda a: pl.BlockSpec(a.shape, lambda i, j: (0,) * a.ndim)
    return pl.pallas_call(
        _s5_kernel,
        grid=(b, s // rows),
        in_specs=[pl.BlockSpec((None, rows, D_S5), lambda i, j: (i, j, 0)),
                  pl.BlockSpec((None, 1, 2 * S5_W), lambda i, j: (i, 0, 0)),
                  const(powre), const(powim), const(wb), const(wc), const(d), const(gluw), const(glub)],
        out_specs=[pl.BlockSpec((None, rows, D_S5), lambda i, j: (i, j, 0)),
                   pl.BlockSpec((None, 1, 2 * S5_W), lambda i, j: (i, 0, 0))],
        out_shape=[jax.ShapeDtypeStruct((b, s, D_S5), F32), jax.ShapeDtypeStruct((b, 1, 2 * S5_W), F32)],
        scratch_shapes=[pltpu.VMEM((1, S5_W), F32), pltpu.VMEM((1, S5_W), F32)],
        compiler_params=_cparams(("arbitrary", "arbitrary")),
        name="s5_prompt",
    )(u, h0, powre, powim, wb, wc, d, gluw, glub)


def _s5_step_kernel(u_ref, h0_ref, powre_ref, powim_ref, wb_ref, wc_ref, d_ref, gluw_ref, glub_ref,
                    y_ref, ht_ref):
    ar, ai = powre_ref[0:1, :], powim_ref[0:1, :]
    hr, hi = h0_ref[:, :S5_W], h0_ref[:, S5_W:]
    for t in range(u_ref.shape[0]):
        u = u_ref[t]
        bu = _dot(u.astype(BF16), wb_ref[...])
        hr, hi = ar * hr - ai * hi + bu[:, :S5_W], ar * hi + ai * hr + bu[:, S5_W:]
        y = _dot(hr.astype(BF16), wc_ref[:S5_W, :]) + _dot(hi.astype(BF16), wc_ref[S5_W:, :])
        y_ref[t] = _s5_head(y, u, d_ref, gluw_ref, glub_ref)
    ht_ref[:, :S5_W] = hr
    ht_ref[:, S5_W:] = hi


def s5_sample(u_tm, h0, powre, powim, wb, wc, d, gluw, glub):
    s, b, _ = u_tm.shape
    return pl.pallas_call(
        _s5_step_kernel,
        out_shape=[jax.ShapeDtypeStruct((s, b, D_S5), F32), jax.ShapeDtypeStruct((b, 2 * S5_W), F32)],
        compiler_params=pltpu.CompilerParams(vmem_limit_bytes=VMEM_LIMIT),
        name="s5_sample",
    )(u_tm, h0, powre, powim, wb, wc, d, gluw, glub)


def s5_matrices(bb_re, bb_im, c_re, c_im):
    g, p, h = S5_GROUPS, S5_STATE, S5_GROUP
    eye = jnp.eye(g, dtype=F32)
    bd_in = lambda bb: jnp.einsum('gph,gk->ghkp', bb.reshape(g, p, h), eye).reshape(g * h, g * p)
    bd_out = lambda c: jnp.einsum('ghp,gk->gpkh', c, eye).reshape(g * p, g * h)
    wb = jnp.concatenate([bd_in(bb_re), bd_in(bb_im)], axis=1).astype(BF16)
    wc = jnp.concatenate([bd_out(c_re), -bd_out(c_im)], axis=0).astype(BF16)
    return wb, wc


RW_PACK = 8 * D_RW


def _head_ones(n):
    r = lax.broadcasted_iota(I32, (n, n), 0) // HEAD_DIM
    c = lax.broadcasted_iota(I32, (n, n), 1) // HEAD_DIM
    return jnp.where(r == c, 1.0, 0.0).astype(BF16)


def _seg_sum(x, ones_bd):
    hi, lo = _split(x)
    return _dot(hi, ones_bd) + _dot(lo, ones_bd)


def _softplus(x):
    return jnp.maximum(x, 0.0) + jnp.log(1.0 + jnp.exp(-jnp.abs(x)))


def _rw_pre_kernel(cols_ref, shift0_ref, mu_ref, w0_ref, a0_ref, wlr_ref, kk_ref, ka_ref, rk_ref,
                   o_ref, carry, *, chunk, valid):
    @pl.when(pl.program_id(1) == 0)
    def _():
        carry[...] = shift0_ref[...]

    cf = cols_ref[...]
    tm = cf.shape[0]
    row = lax.broadcasted_iota(I32, (tm, 1), 0)
    prev = jnp.where(row == 0, carry[...], pltpu.roll(cf, 1, 0))
    carry[...] = cf[tm - 1:tm, :]
    xs = cf + (prev - cf) * mu_ref[...]
    r, k, v = xs[:, :D_RW], xs[:, D_RW:2 * D_RW], xs[:, 2 * D_RW:3 * D_RW]
    lr = xs[:, 3 * D_RW:]
    lane = lax.broadcasted_iota(I32, lr.shape, 1)
    t = jnp.where(lane < RW_W_RANK, jnp.tanh(lr), jnp.where(lane < RW_W_RANK + RW_A_RANK, lr, _sigmoid(lr)))
    proj = _dot3(t, wlr_ref[...])
    w = -_softplus(-(w0_ref[...] + proj[:, :D_RW])) - 0.5
    logw = -jnp.exp(w)
    a = _sigmoid(a0_ref[...] + proj[:, D_RW:2 * D_RW])
    g = proj[:, 2 * D_RW:]
    ones_bd = _head_ones(D_RW)
    kk = k * kk_ref[...]
    kk = kk / jnp.maximum(jnp.sqrt(_seg_sum(kk * kk, ones_bd)), 1e-12)
    km = k * (1.0 + (a - 1.0) * ka_ref[...])
    bonus = _seg_sum(r * km * rk_ref[...], ones_bd) * v
    pos = row % chunk
    if valid < chunk:
        live = pos < valid
        zero = jnp.zeros_like(r)
        logw, kk, km, v, r = (jnp.where(live, logw, zero), jnp.where(live, kk, zero), jnp.where(live, km, zero),
                              jnp.where(live, v, zero), jnp.where(live, r, zero))
    gc = logw
    d = 1
    while d < chunk:
        gc = gc + jnp.where(pos >= d, pltpu.roll(gc, d, 0), 0.0)
        d *= 2
    eg = jnp.exp(gc)
    eng = jnp.exp(-gc)
    o_ref[:, 0 * D_RW:1 * D_RW] = r * eg
    o_ref[:, 1 * D_RW:2 * D_RW] = kk * jnp.exp(gc - logw)
    o_ref[:, 2 * D_RW:3 * D_RW] = kk * a * eng
    o_ref[:, 3 * D_RW:4 * D_RW] = km * eng
    o_ref[:, 4 * D_RW:5 * D_RW] = v
    o_ref[:, 5 * D_RW:6 * D_RW] = eg
    o_ref[:, 6 * D_RW:7 * D_RW] = bonus
    o_ref[:, 7 * D_RW:8 * D_RW] = g


def rwkv_prepare(cols, shift0, mu, w0, a0, wlr, k_k, k_a, r_k, tm, chunk, valid):
    b, s, c = cols.shape
    const = lambda a: pl.BlockSpec(a.shape, lambda i, j: (0,) * a.ndim)
    return pl.pallas_call(
        functools.partial(_rw_pre_kernel, chunk=chunk, valid=valid),
        grid=(b, s // tm),
        in_specs=[pl.BlockSpec((None, tm, c), lambda i, j: (i, j, 0)),
                  pl.BlockSpec((None, 1, c), lambda i, j: (i, 0, 0)),
                  const(mu), const(w0), const(a0), const(wlr), const(k_k), const(k_a), const(r_k)],
        out_specs=pl.BlockSpec((None, tm, RW_PACK), lambda i, j: (i, j, 0)),
        out_shape=jax.ShapeDtypeStruct((b, s, RW_PACK), F32),
        scratch_shapes=[pltpu.VMEM((1, c), F32)],
        compiler_params=_cparams(("arbitrary", "arbitrary")),
        name="rwkv_prepare",
    )(cols, shift0, mu, w0, a0, wlr, k_k, k_a, r_k)


def _rw_chunk_kernel(x_ref, s0_ref, lnw_ref, lnb_ref, y_ref, st_ref, s_scr, y_scr, *, chunk):
    @pl.when(pl.program_id(1) == 0)
    def _():
        s_scr[...] = s0_ref[...]

    rows = x_ref.shape[0]
    hd = HEAD_DIM
    ri = lax.broadcasted_iota(I32, (chunk, chunk), 0)
    ci = lax.broadcasted_iota(I32, (chunk, chunk), 1)
    strict = ri > ci
    incl = ri >= ci
    eye = jnp.where(ri == ci, 1.0, 0.0)
    ek = lax.broadcasted_iota(I32, (hd, hd), 0) == lax.broadcasted_iota(I32, (hd, hd), 1)
    eye_k = jnp.where(ek, 1.0, 0.0)
    n_sq = chunk.bit_length() - 2
    n_chunks = rows // chunk
    probs = [(c, h) for c in range(n_chunks) for h in range(N_RW_HEADS)]
    each = lambda f: [f(i) for i in range(len(probs))]

    def col(i, j):
        c, h = probs[i]
        return x_ref[c * chunk:(c + 1) * chunk, j * D_RW + h * hd:j * D_RW + (h + 1) * hd]

    rt, kt, bt, km, v = (each(lambda i: col(i, j)) for j in range(5))
    lhs = each(lambda i: _split(jnp.concatenate([kt[i], rt[i]], axis=0)))
    bts, kms, vs = each(lambda i: _split(bt[i])), each(lambda i: _split(km[i])), each(lambda i: _split(v[i]))
    gb = each(lambda i: _mm3_nt(lhs[i], bts[i]))
    gk = each(lambda i: _mm3_nt(lhs[i], kms[i]))
    a_bb = each(lambda i: jnp.where(strict, gb[i][:chunk], 0.0))
    a_rb = each(lambda i: _split(jnp.where(incl, gb[i][chunk:], 0.0)))
    a_kr = each(lambda i: _split(jnp.concatenate([jnp.where(strict, gk[i][:chunk], 0.0),
                                                  jnp.where(incl, gk[i][chunk:], 0.0)], axis=0)))
    av = each(lambda i: _mm3(a_kr[i], vs[i]))
    minv = each(lambda i: eye - a_bb[i])
    p = a_bb
    for _ in range(n_sq):
        ps = each(lambda i: _split(p[i]))
        p = each(lambda i: _mm3(ps[i], ps[i]))
        minv = each(lambda i: minv[i] + _mm3(_split(minv[i]), _split(p[i])))
    minvs = each(lambda i: _split(minv[i]))
    khat = each(lambda i: _mm3(minvs[i], _split(kt[i])))
    p1 = each(lambda i: _mm3(minvs[i], _split(av[i][:chunk])))
    rhat = each(lambda i: rt[i] - _mm3(a_rb[i], _split(khat[i])))
    y1 = each(lambda i: av[i][chunk:] - _mm3(a_rb[i], _split(p1[i])))
    tb = each(lambda i: _mm3(_split(jnp.concatenate([khat[i].T, p1[i].T], axis=0)), bts[i]))
    vk = each(lambda i: _mm3(_split(v[i].T), kms[i]))

    def eg_last(i):
        c, h = probs[i]
        return x_ref[(c + 1) * chunk - 1:(c + 1) * chunk, 5 * D_RW + h * hd:5 * D_RW + (h + 1) * hd]

    gmat = each(lambda i: (eye_k - tb[i][:hd]) * eg_last(i))
    umat = each(lambda i: (vk[i] - tb[i][hd:]) * eg_last(i))
    state = [s_scr[h] for h in range(N_RW_HEADS)]
    for i, (c, h) in enumerate(probs):
        y_scr[c * chunk:(c + 1) * chunk, h * hd:(h + 1) * hd] = _dot3_nt(rhat[i], state[h]) + y1[i]
        state[h] = _dot3(state[h], gmat[i]) + umat[i]
    for h in range(N_RW_HEADS):
        s_scr[h] = state[h]
    st_ref[...] = s_scr[...]
    y = y_scr[...]
    ones_bd = _head_ones(D_RW)
    mean = _seg_sum(y, ones_bd) * (1.0 / hd)
    yc = y - mean
    var = _seg_sum(yc * yc, ones_bd) * (1.0 / hd)
    yn = yc * lax.rsqrt(var + RW_GN_EPS) * lnw_ref[...] + lnb_ref[...]
    y_ref[...] = (yn + x_ref[:, 6 * D_RW:7 * D_RW]) * x_ref[:, 7 * D_RW:8 * D_RW]


def rwkv_chunked(packed, state0, ln_w, ln_b, rows, chunk):
    b, s, _ = packed.shape
    const = lambda a: pl.BlockSpec(a.shape, lambda i, j: (0,) * a.ndim)
    st_spec = pl.BlockSpec((None, N_RW_HEADS, HEAD_DIM, HEAD_DIM), lambda i, j: (i, 0, 0, 0))
    return pl.pallas_call(
        functools.partial(_rw_chunk_kernel, chunk=chunk),
        grid=(b, s // rows),
        in_specs=[pl.BlockSpec((None, rows, RW_PACK), lambda i, j: (i, j, 0)), st_spec, const(ln_w), const(ln_b)],
        out_specs=[pl.BlockSpec((None, rows, D_RW), lambda i, j: (i, j, 0)), st_spec],
        out_shape=[jax.ShapeDtypeStruct((b, s, D_RW), F32),
                   jax.ShapeDtypeStruct((b, N_RW_HEADS, HEAD_DIM, HEAD_DIM), F32)],
        scratch_shapes=[pltpu.VMEM((N_RW_HEADS, HEAD_DIM, HEAD_DIM), F32), pltpu.VMEM((rows, D_RW), F32)],
        compiler_params=_cparams(("arbitrary", "arbitrary")),
        name="rwkv_chunked",
    )(packed, state0, ln_w, ln_b)


def rwkv_lowrank_matrix(w2, a2, g2):
    z = lambda r: jnp.zeros((r, D_RW), F32)
    return jnp.concatenate([
        jnp.concatenate([w2, z(RW_W_RANK), z(RW_W_RANK)], axis=1),
        jnp.concatenate([z(RW_A_RANK), a2, z(RW_A_RANK)], axis=1),
        jnp.concatenate([z(RW_G_RANK), z(RW_G_RANK), g2], axis=1)], axis=0)


def _float_key(x):
    b = lax.bitcast_convert_type(x, I32)
    return jnp.where(b < 0, jnp.int32(INT_MIN) - b, b)


def _py_key(v):
    import numpy as np
    b = int(np.float32(v).view(np.int32))
    return -(b & 0x7FFFFFFF) if b < 0 else b


KEY_NEG_INF = _py_key(NEG_INF)


def _count_ge(sc_ref, nc, cand):
    parts = [jnp.where(sc_ref[c] >= cand, 1.0, 0.0) for c in range(nc)]
    while len(parts) > 1:
        parts = [parts[i] + parts[i + 1] for i in range(0, len(parts) - 1, 2)] + (parts[-1:] if len(parts) % 2 else [])
    return jnp.sum(parts[0], axis=1, keepdims=True)


def _select_threshold(sc_ref, nc, k):
    _, r, wc = sc_ref.shape

    def bit_body(i, carry):
        t, cge = carry
        cand = t + jnp.left_shift(jnp.int32(1), 31 - i)
        cnt = _count_ge(sc_ref, nc, cand)
        ok = cnt >= k
        return jnp.where(ok, cand, t), jnp.where(ok, cnt, cge)

    t0 = jnp.full((r, 1), INT_MIN, I32)
    cge0 = jnp.zeros((r, 1), F32) + jnp.asarray(nc * wc, F32)
    return lax.fori_loop(0, 32, bit_body, (t0, cge0))


def _resolve_ties(sc_ref, nc, t, need):
    _, r, wc = sc_ref.shape
    upper = jnp.where(lax.broadcasted_iota(I32, (wc, wc), 0) < lax.broadcasted_iota(I32, (wc, wc), 1), 1.0, 0.0)

    def body(c, offs):
        x = sc_ref[c]
        e = x == t
        ef = jnp.where(e, 1.0, 0.0)
        rank = _dot(ef, upper) + offs
        sc_ref[c] = jnp.where(e, jnp.where(rank >= need, jnp.int32(INT_MIN), x), x)
        return offs + jnp.sum(ef, axis=1, keepdims=True)

    lax.fori_loop(0, nc, body, jnp.zeros((r, 1), F32))


def _topk_mask_prepare(sc_ref, nc, k, valid_rows):
    t, cge = _select_threshold(sc_ref, nc, k)
    r = t.shape[0]
    live = lax.broadcasted_iota(I32, (r, 1), 0) < valid_rows
    tied = live & (cge > k) & (t != KEY_NEG_INF)

    @pl.when(jnp.max(jnp.where(tied, 1.0, 0.0)) > 0.0)
    def _():
        cgt = _count_ge(sc_ref, nc, t + 1)
        _resolve_ties(sc_ref, nc, t, k - cgt)

    return t


def _indexer_total(s4, wcols, rows):
    tot = jnp.maximum(s4[:rows], 0.0) * wcols[0]
    for h in range(1, N_IDX_HEADS):
        tot = tot + jnp.maximum(s4[h * rows:(h + 1) * rows], 0.0) * wcols[h]
    return tot


SUBLANES = 8


def _col_tree(x, op, rows=SUBLANES):
    parts = [x[i:i + rows] for i in range(0, x.shape[0], rows)]
    while len(parts) > 1:
        parts = [op(parts[i], parts[i + 1]) for i in range(0, len(parts) - 1, 2)] + (parts[-1:] if len(parts) % 2 else [])
    return parts[0]


def _col_sum(x):
    return _col_tree(x, jnp.add)


def _col_max(x):
    return _col_tree(x, jnp.maximum)


def _count_ge_t(sc_ref, nc, cand):
    _, wc, nq = sc_ref.shape
    hits = lambda c: _col_sum(jnp.where(sc_ref[c] >= cand, 1.0, 0.0))

    def pair_body(i, acc):
        return acc + (hits(2 * i) + hits(2 * i + 1))

    acc = lax.fori_loop(0, nc // 2, pair_body, jnp.zeros((SUBLANES, nq), F32))
    acc = lax.cond(nc % 2 == 1, lambda a: a + hits(nc - 1), lambda a: a, acc)
    return jnp.sum(acc, axis=0, keepdims=True)


def _topk_mask_prepare_t(sc_ref, nc, k):
    _, wc, nq = sc_ref.shape

    def bit_body(i, carry):
        t, cge = carry
        cand = t + jnp.left_shift(jnp.int32(1), 31 - i)
        cnt = _count_ge_t(sc_ref, nc, cand)
        ok = cnt >= k
        return jnp.where(ok, cand, t), jnp.where(ok, cnt, cge)

    t0 = jnp.full((1, nq), INT_MIN, I32)
    cge0 = jnp.zeros((1, nq), F32) + jnp.asarray(nc * wc, F32)
    t, cge = lax.fori_loop(0, 32, bit_body, (t0, cge0))
    tied = (cge > k) & (t != KEY_NEG_INF)

    @pl.when(jnp.max(jnp.where(tied, 1.0, 0.0)) > 0.0)
    def _():
        need = k - _count_ge_t(sc_ref, nc, t + 1)
        ri = lax.broadcasted_iota(I32, (wc, wc), 0)
        ci = lax.broadcasted_iota(I32, (wc, wc), 1)
        lower = jnp.where(ci < ri, 1.0, 0.0)

        def body(c, offs):
            x = sc_ref[c]
            e = x == t
            ef = jnp.where(e, 1.0, 0.0)
            rank = _dot(lower, ef) + offs
            sc_ref[c] = jnp.where(e, jnp.where(rank >= need, jnp.int32(INT_MIN), x), x)
            return offs + jnp.sum(ef, axis=0, keepdims=True)

        lax.fori_loop(0, nc, body, jnp.zeros((1, nq), F32))

    return t


def _dsa_prompt_kernel(q_ref, kb_ref, vt_ref, qi3_ref, ki3_ref, kiw_ref, o_ref, sc, s_even, s_odd, *, k_sel):
    j = pl.program_id(1)
    qb, wc = Q_BLOCK, KEY_CHUNK
    nck = (j * qb + qb + wc - 1) // wc
    qpos = j * qb + lax.broadcasted_iota(I32, (1, qb), 1)
    keybase = lax.broadcasted_iota(I32, (wc, 1), 0)

    lhs = jnp.concatenate([qi3_ref[:, 2 * LANES * h:2 * LANES * (h + 1)] for h in range(N_IDX_HEADS)], axis=0)
    wrows = [kiw_ref[IDX_DIM + h:IDX_DIM + h + 1, :] for h in range(N_IDX_HEADS)]

    def head_scores(c):
        kc = ki3_ref[pl.ds(pl.multiple_of(c * wc, wc), wc), :]
        return _dot_nt(kc, lhs)

    def store_keys(c, s4, diagonal):
        tot = jnp.maximum(s4[:, :qb], 0.0) * wrows[0]
        for h in range(1, N_IDX_HEADS):
            tot = tot + jnp.maximum(s4[:, h * qb:(h + 1) * qb], 0.0) * wrows[h]
        if diagonal:
            tot = jnp.where(c * wc + keybase <= qpos, tot, NEG_INF)
        sc[c] = _float_key(tot)

    def score_pair(i, carry):
        s_a, s_b = head_scores(2 * i), head_scores(2 * i + 1)
        store_keys(2 * i, s_a, False)
        store_keys(2 * i + 1, s_b, False)
        return carry

    lax.fori_loop(0, (nck - 1) // 2, score_pair, 0)

    @pl.when((nck - 1) % 2 == 1)
    def _():
        store_keys(nck - 2, head_scores(nck - 2), False)

    store_keys(nck - 1, head_scores(nck - 1), True)
    t = _topk_mask_prepare_t(sc, nck, k_sel)
    t_sel = jnp.maximum(t, KEY_NEG_INF + 1)

    lane = lax.broadcasted_iota(I32, (qb, LANES), 1)
    n_pairs = N_ATT_HEADS // 2
    qpair = []
    for pr in range(n_pairs):
        blk = q_ref[:, pr * LANES:(pr + 1) * LANES]
        zero = jnp.zeros_like(blk)
        qpair.append(jnp.concatenate([jnp.where(lane < HEAD_DIM, blk, zero), jnp.where(lane < HEAD_DIM, zero, blk)],
                                     axis=0))

    ac = ATT_CHUNK
    per = wc // ac
    n_att = (j * qb + qb + ac - 1) // ac

    def scores_to(c, buf):
        off = pl.multiple_of(c * ac, ac)
        keys = sc[c // per, pl.ds(pl.multiple_of((c % per) * ac, ac), ac), :]
        bias = jnp.where(keys >= t_sel, 0.0, NEG_INF)
        bias2 = jnp.concatenate([bias, bias], axis=1)
        for pr in range(n_pairs):
            buf[pr] = _dot_nt(kb_ref[pl.ds(off, ac), pr * LANES:(pr + 1) * LANES], qpair[pr]) + bias2

    def consume(c, buf, carry):
        ms, ls, accs = carry
        off = pl.multiple_of(c * ac, ac)
        ss = [buf[pr] for pr in range(n_pairs)]
        new_m = [jnp.maximum(ms[pr], jnp.max(_col_max(ss[pr]), axis=0, keepdims=True)) for pr in range(n_pairs)]
        alphas = [jnp.exp2(ms[pr] - new_m[pr]) for pr in range(n_pairs)]
        ps = [jnp.exp2(ss[pr] - new_m[pr]) for pr in range(n_pairs)]
        new_l = [alphas[pr] * ls[pr] + jnp.sum(_col_sum(ps[pr]), axis=0, keepdims=True) for pr in range(n_pairs)]
        new_acc = [alphas[pr] * accs[pr]
                   + _dot(vt_ref[pr * LANES:(pr + 1) * LANES, pl.ds(off, ac)], ps[pr].astype(BF16))
                   for pr in range(n_pairs)]
        return tuple(new_m), tuple(new_l), tuple(new_acc)

    def att_body(i, carry):
        scores_to(2 * i + 1, s_odd)
        carry = consume(2 * i, s_even, carry)
        scores_to(jnp.minimum(2 * i + 2, n_att - 1), s_even)
        return consume(2 * i + 1, s_odd, carry)

    init = (tuple(jnp.full((1, 2 * qb), NEG_INF, F32) for _ in range(n_pairs)),
            tuple(jnp.zeros((1, 2 * qb), F32) for _ in range(n_pairs)),
            tuple(jnp.zeros((LANES, 2 * qb), F32) for _ in range(n_pairs)))
    scores_to(0, s_even)
    carry = lax.fori_loop(0, n_att // 2, att_body, init)
    _, ls, accs = lax.cond(n_att % 2 == 1, lambda cr: consume(n_att - 1, s_even, cr), lambda cr: cr, carry)
    sub = lax.broadcasted_iota(I32, (LANES, qb), 0)
    for pr in range(n_pairs):
        o2 = accs[pr] / ls[pr]
        o_ref[:, pr * LANES:(pr + 1) * LANES] = jnp.where(sub < HEAD_DIM, o2[:, :qb], o2[:, qb:]).T


def dsa_prompt(q, kb, vt, qi3, ki3, kiw_t):
    b, s, _ = q.shape
    k_sel = min(TOPK_MAX, s // 4)
    blk = lambda w: pl.BlockSpec((None, Q_BLOCK, w), lambda i, j: (i, j, 0))
    full = lambda w: pl.BlockSpec((None, s, w), lambda i, j: (i, 0, 0))
    return pl.pallas_call(
        functools.partial(_dsa_prompt_kernel, k_sel=k_sel),
        grid=(b, s // Q_BLOCK),
        in_specs=[blk(D_ATT), full(D_ATT), pl.BlockSpec((None, D_ATT, s), lambda i, j: (i, 0, 0)),
                  blk(N_IDX_HEADS * 2 * LANES), full(2 * LANES),
                  pl.BlockSpec((None, LANES, Q_BLOCK), lambda i, j: (i, 0, j))],
        out_specs=blk(D_ATT),
        out_shape=jax.ShapeDtypeStruct((b, s, D_ATT), F32),
        scratch_shapes=[pltpu.VMEM((s // KEY_CHUNK, KEY_CHUNK, Q_BLOCK), I32),
                        pltpu.VMEM((N_ATT_HEADS // 2, ATT_CHUNK, 2 * Q_BLOCK), F32),
                        pltpu.VMEM((N_ATT_HEADS // 2, ATT_CHUNK, 2 * Q_BLOCK), F32)],
        compiler_params=_cparams(("arbitrary", "arbitrary")),
        name="dsa_prompt",
    )(q, kb, vt, qi3, ki3, kiw_t)


SAMPLE_ROWS = 8
PAGE_GROUP = 16
IDX_PAGE_GROUP = 32


def _dsa_sample_index_kernel(pt_ref, *refs, n_valid, k_sel):
    pages = refs[:IDX_PAGE_GROUP]
    qi3_ref, kiw_ref, ki3n_ref, sc_ref, thr_ref = refs[IDX_PAGE_GROUP:]
    g = pl.program_id(1)
    r = SAMPLE_ROWS
    n_chunks = sc_ref.shape[0]
    q3 = qi3_ref[...].astype(F32)
    qhi = jnp.concatenate([q3[:, 2 * LANES * h:2 * LANES * h + IDX_DIM] for h in range(N_IDX_HEADS)],
                          axis=0).astype(BF16)
    qlo = jnp.concatenate([q3[:, 2 * LANES * h + LANES:2 * LANES * h + LANES + IDX_DIM]
                           for h in range(N_IDX_HEADS)], axis=0).astype(BF16)
    wcols = [kiw_ref[:, IDX_DIM + h:IDX_DIM + h + 1] for h in range(N_IDX_HEADS)]

    def scores(kt):
        khi, klo = _split(kt)
        return _dot(qhi, khi) + (_dot(qhi, klo) + _dot(qlo, khi))

    for i in range(IDX_PAGE_GROUP):
        sc_ref[g * IDX_PAGE_GROUP + i] = _float_key(_indexer_total(scores(pages[i][...]), wcols, r))

    @pl.when(g == pl.num_programs(1) - 1)
    def _():
        tot = _indexer_total(scores(ki3n_ref[...]), wcols, r)
        row = lax.broadcasted_iota(I32, tot.shape, 0)
        col = lax.broadcasted_iota(I32, tot.shape, 1)
        tot = jnp.where((col <= row) & (col < n_valid), tot, NEG_INF)
        sc_ref[n_chunks - 1] = _float_key(tot)
        t = _topk_mask_prepare(sc_ref, n_chunks, k_sel, n_valid)
        thr_ref[...] = jnp.broadcast_to(t, thr_ref.shape)


def dsa_sample_index(page_table, cache_kidx_t, layer, qi3, kiw, ki_new_t, n_valid):
    b, n_pages = page_table.shape
    n_chunks = n_pages + 1
    k_sel = min(TOPK_MAX, (n_pages * PAGE_SIZE + n_valid) // 4)
    page_spec = lambda i: pl.BlockSpec((None, None, IDX_DIM, PAGE_SIZE),
                                       lambda bi, g, pt: (layer, pt[bi, g * IDX_PAGE_GROUP + i], 0, 0))
    per_b = lambda shape: pl.BlockSpec((None,) + shape, lambda bi, g, pt: (bi,) + (0,) * len(shape))
    r = SAMPLE_ROWS
    return pl.pallas_call(
        functools.partial(_dsa_sample_index_kernel, n_valid=n_valid, k_sel=k_sel),
        grid_spec=pltpu.PrefetchScalarGridSpec(
            num_scalar_prefetch=1,
            grid=(b, n_pages // IDX_PAGE_GROUP),
            in_specs=[page_spec(i) for i in range(IDX_PAGE_GROUP)]
            + [per_b((r, N_IDX_HEADS * 2 * LANES)), per_b((r, LANES)), per_b((IDX_DIM, PAGE_SIZE))],
            out_specs=[per_b((n_chunks, r, PAGE_SIZE)), per_b((r, PAGE_SIZE))],
        ),
        out_shape=[jax.ShapeDtypeStruct((b, n_chunks, r, PAGE_SIZE), I32),
                   jax.ShapeDtypeStruct((b, r, PAGE_SIZE), I32)],
        compiler_params=_cparams(("arbitrary", "arbitrary")),
        name="dsa_sample_index",
    )(page_table, *([cache_kidx_t] * IDX_PAGE_GROUP), qi3, kiw, ki_new_t)


def _dsa_sample_attend_kernel(pt_ref, *refs, n_valid):
    kpages = refs[:PAGE_GROUP]
    vpages = refs[PAGE_GROUP:2 * PAGE_GROUP]
    qbd_ref, sel_ref, seln_ref, thr_ref, kn_ref, vn_ref, o_ref, m_scr, l_scr, acc_scr = refs[2 * PAGE_GROUP:]
    g = pl.program_id(1)
    r = SAMPLE_ROWS

    @pl.when(g == 0)
    def _():
        m_scr[...] = jnp.full(m_scr.shape, NEG_INF, F32)
        l_scr[...] = jnp.zeros(l_scr.shape, F32)
        acc_scr[...] = jnp.zeros(acc_scr.shape, F32)

    qbd = qbd_ref[...]
    thr = thr_ref[...]
    per_head = lambda x: jnp.concatenate([x] * N_ATT_HEADS, axis=0)
    thr_all = per_head(thr)

    def update(sels, extra, kts, vts):
        n = len(kts)
        scores = jnp.concatenate([_dot(qbd, kt) for kt in kts], axis=1)
        mask = jnp.concatenate([per_head(sel) for sel in sels], axis=1) >= jnp.concatenate([thr_all] * n, axis=1)
        if extra is not None:
            mask = mask & (per_head(extra) > 0)
        s = jnp.where(mask, scores, NEG_INF)
        m_prev = m_scr[...]
        m_new = jnp.maximum(m_prev, jnp.max(s, axis=1, keepdims=True))
        alpha = jnp.exp(m_prev - m_new)
        p = jnp.exp(s - m_new)
        l_scr[...] = alpha * l_scr[...] + jnp.sum(p, axis=1, keepdims=True)
        p = p.astype(BF16)
        pv = _dot_nt(p[:, :PAGE_SIZE], vts[0])
        for i in range(1, n):
            pv = pv + _dot_nt(p[:, i * PAGE_SIZE:(i + 1) * PAGE_SIZE], vts[i])
        acc_scr[...] = alpha * acc_scr[...] + pv
        m_scr[...] = m_new

    update([sel_ref[i] for i in range(PAGE_GROUP)], None,
           [kpages[i][...].astype(BF16) for i in range(PAGE_GROUP)],
           [vpages[i][...].astype(BF16) for i in range(PAGE_GROUP)])

    @pl.when(g == pl.num_programs(1) - 1)
    def _():
        row = lax.broadcasted_iota(I32, thr.shape, 0)
        col = lax.broadcasted_iota(I32, thr.shape, 1)
        causal = jnp.where((col <= row) & (col < n_valid), 1, 0)
        update([seln_ref[...]], causal, [kn_ref[...]], [vn_ref[...]])
        out = acc_scr[...] / l_scr[...]
        lane = lax.broadcasted_iota(I32, (r, D_ATT), 1) // HEAD_DIM
        tot = jnp.where(lane == 0, out[:r], 0.0)
        for h in range(1, N_ATT_HEADS):
            tot = tot + jnp.where(lane == h, out[h * r:(h + 1) * r], 0.0)
        o_ref[...] = tot


def dsa_sample_attend(page_table, cache_kt, cache_vt, layer, q_bd, sel, thr, k_new_t, v_new_t, n_valid):
    b, n_pages = page_table.shape
    r = SAMPLE_ROWS
    page_spec = lambda i: pl.BlockSpec((None, None, D_ATT, PAGE_SIZE),
                                       lambda bi, g, pt: (layer, pt[bi, g * PAGE_GROUP + i], 0, 0))
    per_b = lambda shape: pl.BlockSpec((None,) + shape, lambda bi, g, pt: (bi,) + (0,) * len(shape))
    rows = N_ATT_HEADS * r
    return pl.pallas_call(
        functools.partial(_dsa_sample_attend_kernel, n_valid=n_valid),
        grid_spec=pltpu.PrefetchScalarGridSpec(
            num_scalar_prefetch=1,
            grid=(b, n_pages // PAGE_GROUP),
            in_specs=[page_spec(i) for i in range(PAGE_GROUP)] + [page_spec(i) for i in range(PAGE_GROUP)]
            + [per_b((rows, D_ATT)),
               pl.BlockSpec((None, PAGE_GROUP, r, PAGE_SIZE), lambda bi, g, pt: (bi, g, 0, 0)),
               pl.BlockSpec((None, None, r, PAGE_SIZE), lambda bi, g, pt: (bi, n_pages, 0, 0)),
               per_b((r, PAGE_SIZE)), per_b((D_ATT, PAGE_SIZE)), per_b((D_ATT, PAGE_SIZE))],
            out_specs=per_b((r, D_ATT)),
            scratch_shapes=[pltpu.VMEM((rows, 1), F32), pltpu.VMEM((rows, 1), F32), pltpu.VMEM((rows, D_ATT), F32)],
        ),
        out_shape=jax.ShapeDtypeStruct((b, r, D_ATT), F32),
        compiler_params=_cparams(("arbitrary", "arbitrary")),
        name="dsa_sample_attend",
    )(page_table, *([cache_kt] * PAGE_GROUP), *([cache_vt] * PAGE_GROUP), q_bd, sel, sel, thr, k_new_t, v_new_t)


def pack_w_in(w_in):
    d = w_in.shape[0]
    pad = jnp.zeros((d, ZP_END - ZP_KIW - (OFF_RW - OFF_KI)), w_in.dtype)
    return jnp.concatenate([w_in[:, OFF_S5:OFF_KI], w_in[:, OFF_RW:], w_in[:, OFF_KI:OFF_RW], pad], axis=1).astype(BF16)


def rope_tables(pos):
    half = HEAD_DIM // 2
    inv = ROPE_THETA ** (-jnp.arange(half, dtype=F32) / half)
    ang = pos.astype(F32)[:, None] * inv[None, :]
    cos, sin = jnp.cos(ang), jnp.sin(ang)
    return jnp.tile(jnp.concatenate([cos, cos], 1), (1, 2)), jnp.tile(jnp.concatenate([-sin, sin], 1), (1, 2))


def block_diag_queries(q):
    b, r, d = q.shape
    head_of_lane = jnp.arange(d) // HEAD_DIM
    keep = head_of_lane[None, :] == jnp.arange(N_ATT_HEADS)[:, None]
    return jnp.where(keep[None, :, None, :], q[:, None], jnp.zeros((), q.dtype)).reshape(b, N_ATT_HEADS * r, d)


def _pad_rows(a, rows):
    return jnp.pad(a, ((0, 0), (0, rows - a.shape[1])) + ((0, 0),) * (a.ndim - 2))


Q_SCALE = HEAD_DIM ** -0.5
Q_SCALE_EXP2 = Q_SCALE * math.log2(math.e)
PROMPT_TILE = 256
RW_PRE_TILE = 512
RW_STEP_ROWS = 256


def kernel(x_prompt, x_sample, c_prompt, c_sample, cache_k, cache_v, cache_kidx, state_s5_re, state_s5_im, state_rwkv, state_rwkv_shift, page_table, ada_w, ada_b, norm_pre, norm_post, ffn_wi, ffn_wo, w_in, w_out, s5_a_re, s5_a_im, s5_log_dt, s5_b_re, s5_b_im, s5_c_re, s5_c_im, s5_d, s5_glu_w, s5_glu_b, rw_mu, rw_w0, rw_w2, rw_a0, rw_a2, rw_g2, rw_k_k, rw_k_a, rw_r_k, rw_ln_w, rw_ln_b):
    bp, sp, d = x_prompt.shape
    bs, ss, _ = x_sample.shape
    depth = ada_w.shape[0]
    past = page_table.shape[1] * PAGE_SIZE
    np_tok, ns_tok = bp * sp, bs * ss
    tpb = sp // PROMPT_TILE
    row1 = lambda a: a.reshape(1, -1)

    c_all = _pad_rows(jnp.concatenate([c_prompt, c_sample], axis=0)[None], -(-(bp + bs) // 8) * 8)[0]
    mod = ada_mod(c_all, ada_w, ada_b).reshape(depth, c_all.shape[0], N_SUB, 3, d)

    cos_p, sin_p = rope_tables(jnp.arange(sp))
    cos_s, sin_s = (jnp.tile(t, (bs, 1)) for t in rope_tables(past + jnp.arange(ss)))
    n_pool = cache_k.shape[1]
    cache_kt = jnp.transpose(cache_k, (0, 1, 3, 4, 2)).reshape(depth, n_pool, D_ATT, PAGE_SIZE)
    cache_vt = jnp.transpose(cache_v, (0, 1, 3, 4, 2)).reshape(depth, n_pool, D_ATT, PAGE_SIZE)
    cache_kidx_t = jnp.swapaxes(cache_kidx, 2, 3)

    xp = x_prompt.reshape(np_tok, d)
    xs = x_sample.reshape(ns_tok, d)
    outs_p, outs_s = [], []
    for l in range(depth):
        mod_p = mod[l, :bp]
        mod_s = jnp.repeat(mod[l, bp:bp + bs], ss, axis=0)
        pm = lambda i, j: mod_p[:, i, j][:, None, :]
        sm = lambda i, j: mod_s[:, i, j][None]
        npre = lambda i: row1(norm_pre[l, i])
        npost = lambda i: row1(norm_post[l, i])
        wi0, wo0 = ffn_wi[l, 0].astype(BF16), ffn_wo[l, 0].astype(BF16)
        wi1, wo1 = ffn_wi[l, 1].astype(BF16), ffn_wo[l, 1].astype(BF16)
        w_in_p = pack_w_in(w_in[l])
        w_out_b = w_out[l].astype(BF16)
        powre, powim, bb_re, bb_im = s5_discretise(s5_a_re[l], s5_a_im[l], s5_log_dt[l], s5_b_re[l], s5_b_im[l],
                                                   rows=S5_CHUNK)
        wb, wc = s5_matrices(bb_re, bb_im, s5_c_re[l], s5_c_im[l])
        s5_tail = (powre, powim, wb, wc, row1(s5_d[l]), s5_glu_w[l].astype(BF16), row1(s5_glu_b[l]))
        wlr = rwkv_lowrank_matrix(rw_w2[l], rw_a2[l], rw_g2[l])
        rw_pre = (row1(rw_mu[l]), row1(rw_w0[l]), row1(rw_a0[l]), wlr, row1(rw_k_k[l]), row1(rw_k_a[l]),
                  row1(rw_r_k[l]))
        ln = (row1(rw_ln_w[l]), row1(rw_ln_b[l]))

        xp = ffn_block(xp, pm(0, 0), pm(0, 1), pm(0, 2), npre(0), npost(0), wi0, wo0, PROMPT_TILE, tpb)
        (u, q, _, kb, _, _, qi3, _, ki3, rw, vt, kt_f, vt_f, kiw_t) = inproj_block(
            xp, pm(1, 0), pm(1, 1), npre(1), w_in_p, cos_p, sin_p, PROMPT_TILE, tpb, tpb, Q_SCALE_EXP2)
        seq = lambda a: a.reshape(bp, sp, a.shape[-1])
        y_s5, h_s5 = s5_prompt(seq(u), jnp.zeros((bp, 1, 2 * S5_W), F32), *s5_tail)
        y_att = dsa_prompt(seq(q), seq(kb), vt, seq(qi3), seq(ki3), kiw_t)
        cols = seq(rw)
        packed = rwkv_prepare(cols, jnp.zeros((bp, 1, N_RW_COLS), F32), *rw_pre, RW_PRE_TILE, RW_CHUNK, RW_CHUNK)
        y_rw, st_rw = rwkv_chunked(packed, jnp.zeros((bp, N_RW_HEADS, HEAD_DIM, HEAD_DIM), F32), *ln,
                                   RW_STEP_ROWS, RW_CHUNK)
        flat = lambda a: a.reshape(np_tok, a.shape[-1])
        xp = outproj_block(xp, flat(y_s5), flat(y_att), flat(y_rw), pm(1, 2), npost(1), w_out_b, PROMPT_TILE, tpb)
        xp = ffn_block(xp, pm(2, 0), pm(2, 1), pm(2, 2), npre(2), npost(2), wi1, wo1, PROMPT_TILE, tpb)
        heads_last = lambda a: jnp.transpose(a.reshape(bp, N_ATT_HEADS, HEAD_DIM, sp), (0, 3, 1, 2))
        outs_p.append((heads_last(kt_f), heads_last(vt_f), jnp.swapaxes(kiw_t[:, :IDX_DIM, :], 1, 2),
                       h_s5[:, 0, :S5_W].reshape(bp, S5_GROUPS, S5_STATE),
                       h_s5[:, 0, S5_W:].reshape(bp, S5_GROUPS, S5_STATE), st_rw, cols[:, sp - 1]))

        xs = ffn_block(xs, sm(0, 0), sm(0, 1), sm(0, 2), npre(0), npost(0), wi0, wo0, ns_tok, 1)
        u, q, k, kb, v, vb, qi3, kiw, ki3, rw = inproj_block(xs, sm(1, 0), sm(1, 1), npre(1), w_in_p, cos_s, sin_s,
                                                             ns_tok, 1, 1, Q_SCALE)[:10]
        seq = lambda a: a.reshape(bs, ss, a.shape[-1])
        h0 = jnp.concatenate([state_s5_re[l].reshape(bs, S5_W), state_s5_im[l].reshape(bs, S5_W)], axis=1)
        y_s5, h_s5 = s5_sample(jnp.swapaxes(seq(u), 0, 1), h0, *s5_tail)
        y_s5 = jnp.swapaxes(y_s5, 0, 1)
        tok_t = lambda a: jnp.swapaxes(_pad_rows(a, PAGE_SIZE), 1, 2)
        sel, thr = dsa_sample_index(page_table, cache_kidx_t, l, _pad_rows(seq(qi3), SAMPLE_ROWS),
                                    _pad_rows(seq(kiw), SAMPLE_ROWS), tok_t(seq(kiw)[:, :, :IDX_DIM]), ss)
        q_bd = block_diag_queries(_pad_rows(seq(q), SAMPLE_ROWS))
        y_att = dsa_sample_attend(page_table, cache_kt, cache_vt, l, q_bd, sel, thr, tok_t(seq(kb)), tok_t(seq(vb)),
                                  ss)[:, :ss]
        cols = seq(rw)
        packed = rwkv_prepare(_pad_rows(cols, RW_CHUNK), state_rwkv_shift[l].reshape(bs, 1, N_RW_COLS), *rw_pre,
                              RW_CHUNK, RW_CHUNK, ss)
        y_rw, st_rw = rwkv_chunked(packed, state_rwkv[l], *ln, RW_CHUNK, RW_CHUNK)
        flat = lambda a: a.reshape(ns_tok, a.shape[-1])
        xs = outproj_block(xs, flat(y_s5), flat(y_att), flat(y_rw[:, :ss]), sm(1, 2), npost(1), w_out_b, ns_tok, 1)
        xs = ffn_block(xs, sm(2, 0), sm(2, 1), sm(2, 2), npre(2), npost(2), wi1, wo1, ns_tok, 1)
        outs_s.append((k.reshape(bs, ss, N_ATT_HEADS, HEAD_DIM), v.reshape(bs, ss, N_ATT_HEADS, HEAD_DIM),
                       seq(kiw)[:, :, :IDX_DIM],
                       h_s5[:, :S5_W].reshape(bs, S5_GROUPS, S5_STATE),
                       h_s5[:, S5_W:].reshape(bs, S5_GROUPS, S5_STATE), st_rw, cols[:, ss - 1]))

    stack = lambda outs, i: jnp.stack([o[i] for o in outs])
    return ((xp.reshape(bp, sp, d), xs.reshape(bs, ss, d))
            + tuple(stack(outs_p, i) for i in range(7)) + tuple(stack(outs_s, i) for i in range(7)))
```

```python
import functools
import math

import jax
import jax.numpy as jnp
from jax import lax
from jax.experimental import pallas as pl
from jax.experimental.pallas import tpu as pltpu

F32 = jnp.float32
BF16 = jnp.bfloat16
I32 = jnp.int32

D_MODEL = 1024
PAGE_SIZE = 128
D_S5 = 256
D_ATT = 512
D_RW = 256
S5_GROUP = 16
S5_GROUPS = D_S5 // S5_GROUP
S5_STATE = 64
S5_W = S5_GROUPS * S5_STATE
HEAD_DIM = 64
N_ATT_HEADS = D_ATT // HEAD_DIM
N_IDX_HEADS = 4
IDX_DIM = 64
TOPK_MAX = 256
Q_BLOCK = 128
ROPE_THETA = 10000.0
N_RW_HEADS = D_RW // HEAD_DIM
RW_W_RANK = 32
RW_A_RANK = 32
RW_G_RANK = 64
RW_GN_EPS = 64e-5
N_RW_COLS = 3 * D_RW + RW_W_RANK + RW_A_RANK + RW_G_RANK
D_FF = 2816
HALF_STEP = 0.5
RMS_EPS = 1e-6
N_SUB = 3
NEG_INF = -1e30
OFF_S5 = 0
OFF_Q = OFF_S5 + D_S5
OFF_K = OFF_Q + D_ATT
OFF_V = OFF_K + D_ATT
OFF_QI = OFF_V + D_ATT
OFF_KI = OFF_QI + N_IDX_HEADS * IDX_DIM
OFF_WI = OFF_KI + IDX_DIM
OFF_RW = OFF_WI + N_IDX_HEADS
N_IN = OFF_RW + N_RW_COLS

LANES = 128
VMEM_LIMIT = 56 * 1024 * 1024
INT_MIN = -(2 ** 31)

ZP_U, ZP_Q, ZP_K, ZP_V, ZP_QI, ZP_RW, ZP_KIW, ZP_END = 0, 256, 768, 1280, 1792, 2048, 2944, 3072

RW_CHUNK = 64
S5_CHUNK = 256
KEY_CHUNK = 512
ATT_CHUNK = 512


def _cparams(sem):
    return pltpu.CompilerParams(dimension_semantics=sem, vmem_limit_bytes=VMEM_LIMIT)


def _dot(a, b):
    return jnp.dot(a, b, preferred_element_type=F32)


def _dot_nt(a, b):
    return lax.dot_general(a, b, (((1,), (1,)), ((), ())), preferred_element_type=F32)


def _split(x):
    hi = x.astype(BF16)
    lo = (x - hi.astype(F32)).astype(BF16)
    return hi, lo


def _dot3(a, b):
    ah, al = _split(a)
    bh, bl = _split(b)
    return _dot(ah, bh) + (_dot(ah, bl) + _dot(al, bh))


def _dot3_nt(a, b):
    ah, al = _split(a)
    bh, bl = _split(b)
    return _dot_nt(ah, bh) + (_dot_nt(ah, bl) + _dot_nt(al, bh))


def _mm3(a, b):
    return _dot(a[0], b[0]) + (_dot(a[0], b[1]) + _dot(a[1], b[0]))


def _mm3_nt(a, b):
    return _dot_nt(a[0], b[0]) + (_dot_nt(a[0], b[1]) + _dot_nt(a[1], b[0]))


def _rms(x, g):
    return x * lax.rsqrt(jnp.mean(x * x, axis=-1, keepdims=True) + RMS_EPS) * g


def _sigmoid(x):
    return 1.0 / (1.0 + jnp.exp(-x))


def _ada_kernel(c_ref, w_ref, b_ref, o_ref):
    c = c_ref[...]
    h = (c * _sigmoid(c)).astype(BF16)
    o_ref[...] = _dot(h, w_ref[...].astype(BF16)) + b_ref[...]


def ada_mod(c_all, ada_w, ada_b, tn=1152):
    depth, d, n = ada_w.shape
    rows = c_all.shape[0]
    return pl.pallas_call(
        _ada_kernel,
        grid=(depth, n // tn),
        in_specs=[pl.BlockSpec((rows, d), lambda l, j: (0, 0)),
                  pl.BlockSpec((None, d, tn), lambda l, j: (l, 0, j)),
                  pl.BlockSpec((None, 1, tn), lambda l, j: (l, 0, j))],
        out_specs=pl.BlockSpec((None, rows, tn), lambda l, j: (l, 0, j)),
        out_shape=jax.ShapeDtypeStruct((depth, rows, n), F32),
        compiler_params=_cparams(("arbitrary", "arbitrary")),
        name="ada_mod",
    )(c_all, ada_w, ada_b.reshape(depth, 1, n))


def _mod_spec(mod, tiles_per_group):
    r = mod.shape[1]
    return pl.BlockSpec((None, r, mod.shape[2]), lambda i: (i // tiles_per_group, 0, 0))


def _ffn_kernel(x_ref, shift_ref, scale_ref, gate_ref, gpre_ref, gpost_ref, wi_ref, wo_ref, o_ref, *, res_w):
    x = x_ref[...]
    h = (_rms(x, gpre_ref[...]) * (1.0 + scale_ref[...]) + shift_ref[...]).astype(BF16)
    g = _dot(h, wi_ref[:, :D_FF])
    u = _dot(h, wi_ref[:, D_FF:])
    a = (g * _sigmoid(g) * u).astype(BF16)
    o = _dot(a, wo_ref[...])
    o_ref[...] = x + (res_w * gate_ref[...]) * _rms(o, gpost_ref[...])


def ffn_block(x, shift, scale, gate, g_pre, g_post, wi, wo, tm, tiles_per_group):
    n, d = x.shape
    row = pl.BlockSpec((tm, d), lambda i: (i, 0))
    const = lambda a: pl.BlockSpec(a.shape, lambda i: (0,) * a.ndim)
    return pl.pallas_call(
        functools.partial(_ffn_kernel, res_w=HALF_STEP),
        grid=(n // tm,),
        in_specs=[row, _mod_spec(shift, tiles_per_group), _mod_spec(scale, tiles_per_group),
                  _mod_spec(gate, tiles_per_group), const(g_pre), const(g_post), const(wi), const(wo)],
        out_specs=row,
        out_shape=jax.ShapeDtypeStruct((n, d), F32),
        compiler_params=_cparams(("arbitrary",)),
        name="ffn_block",
    )(x, shift, scale, gate, g_pre, g_post, wi, wo)


def _rot_block(blk, cos, sin, lane):
    partner = jnp.where((lane & 32) == 0, pltpu.roll(blk, LANES - 32, 1), pltpu.roll(blk, 32, 1))
    return blk * cos + partner * sin


def _inproj_kernel(x_ref, shift_ref, scale_ref, gpre_ref, w_ref, cos_ref, sin_ref,
                   u_ref, q_ref, k_ref, kb_ref, v_ref, vb_ref, qi3_ref, kiw_ref, ki3_ref, rw_ref,
                   vt_ref, ktf_ref, vtf_ref, kiwt_ref, *, q_scale):
    x = x_ref[...]
    h = (_rms(x, gpre_ref[...]) * (1.0 + scale_ref[...]) + shift_ref[...]).astype(BF16)
    z = _dot(h, w_ref[...])
    cos = cos_ref[...]
    sin = sin_ref[...]
    lane = lax.broadcasted_iota(I32, cos.shape, 1)
    lo_half = lane < 64
    rot = lambda off: _rot_block(z[:, off:off + LANES], cos, sin, lane)

    u_ref[...] = z[:, ZP_U:ZP_Q]
    for j in range(D_ATT // LANES):
        q_ref[:, j * LANES:(j + 1) * LANES] = (rot(ZP_Q + j * LANES) * q_scale).astype(BF16)
        kr = rot(ZP_K + j * LANES)
        k_ref[:, j * LANES:(j + 1) * LANES] = kr
        kb_ref[:, j * LANES:(j + 1) * LANES] = kr.astype(BF16)
        ktf_ref[j * LANES:(j + 1) * LANES, :] = kr.T
    v = z[:, ZP_V:ZP_QI]
    v_ref[...] = v
    vb_ref[...] = v.astype(BF16)
    vt = v.T
    vtf_ref[...] = vt
    vt_ref[...] = vt.astype(BF16)
    for j in range(N_IDX_HEADS * IDX_DIM // LANES):
        qr = rot(ZP_QI + j * LANES)
        hi = qr.astype(BF16).astype(F32)
        lo = qr - hi
        hi_sw = pltpu.roll(hi, 64, 1)
        lo_sw = pltpu.roll(lo, 64, 1)
        zero = jnp.zeros_like(hi)
        for half in range(2):
            base = (2 * j + half) * 2 * LANES
            a, b = (hi, lo) if half == 0 else (hi_sw, lo_sw)
            a_sw = hi_sw if half == 0 else hi
            qi3_ref[:, base:base + LANES] = jnp.where(lo_half, a, a_sw).astype(BF16)
            qi3_ref[:, base + LANES:base + 2 * LANES] = jnp.where(lo_half, b, zero).astype(BF16)
    raw = z[:, ZP_KIW:ZP_END]
    kiw = jnp.where(lo_half, _rot_block(raw, cos, sin, lane), raw)
    kiw_ref[...] = kiw
    kiwt_ref[...] = kiw.T
    hi = kiw.astype(BF16).astype(F32)
    lo = kiw - hi
    ki3_ref[:, :LANES] = jnp.where(lo_half, hi, pltpu.roll(lo, 64, 1)).astype(BF16)
    ki3_ref[:, LANES:] = jnp.where(lo_half, hi, jnp.zeros_like(hi)).astype(BF16)
    rw_ref[...] = z[:, ZP_RW:ZP_KIW]


def inproj_block(x, shift, scale, g_pre, w, cos_t, sin_t, tm, tiles_per_group, pos_tiles, q_scale):
    n, d = x.shape
    row = lambda width: pl.BlockSpec((tm, width), lambda i: (i, 0))
    const = lambda a: pl.BlockSpec(a.shape, lambda i: (0,) * a.ndim)
    tab = pl.BlockSpec((tm, LANES), lambda i: (i % pos_tiles, 0))
    widths = [(D_S5, F32), (D_ATT, BF16), (D_ATT, F32), (D_ATT, BF16), (D_ATT, F32), (D_ATT, BF16),
              (N_IDX_HEADS * 2 * LANES, BF16), (LANES, F32), (2 * LANES, BF16), (N_RW_COLS, F32)]
    t_widths = [(D_ATT, BF16), (D_ATT, F32), (D_ATT, F32), (LANES, F32)]
    return pl.pallas_call(
        functools.partial(_inproj_kernel, q_scale=q_scale),
        grid=(n // tm,),
        in_specs=[row(d), _mod_spec(shift, tiles_per_group), _mod_spec(scale, tiles_per_group),
                  const(g_pre), const(w), tab, tab],
        out_specs=[row(wd) for wd, _ in widths]
        + [pl.BlockSpec((None, wd, tm), lambda i: (i // pos_tiles, 0, i % pos_tiles)) for wd, _ in t_widths],
        out_shape=[jax.ShapeDtypeStruct((n, wd), dt) for wd, dt in widths]
        + [jax.ShapeDtypeStruct((n // (pos_tiles * tm), wd, pos_tiles * tm), dt) for wd, dt in t_widths],
        compiler_params=_cparams(("arbitrary",)),
        name="inproj_block",
    )(x, shift, scale, g_pre, w, cos_t, sin_t)


def _outproj_kernel(x_ref, ys5_ref, yatt_ref, yrw_ref, gate_ref, gpost_ref, w_ref, o_ref):
    o = (_dot(ys5_ref[...].astype(BF16), w_ref[:D_S5, :])
         + _dot(yatt_ref[...].astype(BF16), w_ref[D_S5:D_S5 + D_ATT, :])
         + _dot(yrw_ref[...].astype(BF16), w_ref[D_S5 + D_ATT:, :]))
    o_ref[...] = x_ref[...] + gate_ref[...] * _rms(o, gpost_ref[...])


def _outffn_kernel(x_ref, ys5_ref, yatt_ref, yrw_ref, gate1_ref, gpost1_ref, wout_ref,
                   shift_ref, scale_ref, gate_ref, gpre_ref, gpost_ref, wi_ref, wo_ref, o_ref, *, res_w):
    o = (_dot(ys5_ref[...].astype(BF16), wout_ref[:D_S5, :])
         + _dot(yatt_ref[...].astype(BF16), wout_ref[D_S5:D_S5 + D_ATT, :])
         + _dot(yrw_ref[...].astype(BF16), wout_ref[D_S5 + D_ATT:, :]))
    x = x_ref[...] + gate1_ref[...] * _rms(o, gpost1_ref[...])
    h = (_rms(x, gpre_ref[...]) * (1.0 + scale_ref[...]) + shift_ref[...]).astype(BF16)
    g = _dot(h, wi_ref[:, :D_FF])
    u = _dot(h, wi_ref[:, D_FF:])
    a = (g * _sigmoid(g) * u).astype(BF16)
    f = _dot(a, wo_ref[...])
    o_ref[...] = x + (res_w * gate_ref[...]) * _rms(f, gpost_ref[...])


def outffn_block(x, y_s5, y_att, y_rw, gate1, g_post1, w_out, shift, scale, gate, g_pre, g_post, wi, wo,
                 tm, tiles_per_group):
    n, d = x.shape
    row = lambda width: pl.BlockSpec((tm, width), lambda i: (i, 0))
    const = lambda a: pl.BlockSpec(a.shape, lambda i: (0,) * a.ndim)
    mod = lambda m: _mod_spec(m, tiles_per_group)
    return pl.pallas_call(
        functools.partial(_outffn_kernel, res_w=HALF_STEP),
        grid=(n // tm,),
        in_specs=[row(d), row(D_S5), row(D_ATT), row(D_RW), mod(gate1), const(g_post1), const(w_out),
                  mod(shift), mod(scale), mod(gate), const(g_pre), const(g_post), const(wi), const(wo)],
        out_specs=row(d),
        out_shape=jax.ShapeDtypeStruct((n, d), F32),
        compiler_params=_cparams(("arbitrary",)),
        name="outffn_block",
    )(x, y_s5, y_att, y_rw, gate1, g_post1, w_out, shift, scale, gate, g_pre, g_post, wi, wo)


def outproj_block(x, y_s5, y_att, y_rw, gate, g_post, w, tm, tiles_per_group):
    n, d = x.shape
    row = lambda width: pl.BlockSpec((tm, width), lambda i: (i, 0))
    const = lambda a: pl.BlockSpec(a.shape, lambda i: (0,) * a.ndim)
    return pl.pallas_call(
        _outproj_kernel,
        grid=(n // tm,),
        in_specs=[row(d), row(D_S5), row(D_ATT), row(D_RW), _mod_spec(gate, tiles_per_group),
                  const(g_post), const(w)],
        out_specs=row(d),
        out_shape=jax.ShapeDtypeStruct((n, d), F32),
        compiler_params=_cparams(("arbitrary",)),
        name="outproj_block",
    )(x, y_s5, y_att, y_rw, gate, g_post, w)


def _gelu_tanh(x):
    return 0.5 * x * (1.0 + jnp.tanh(math.sqrt(2.0 / math.pi) * (x + 0.044715 * (x * x * x))))


def _s5_disc_kernel(are_ref, aim_ref, ldt_ref, arec_ref, aimc_ref, ldtc_ref, bre_ref, bim_ref,
                    powre_ref, powim_ref, bbre_ref, bbim_ref, *, rows):
    def zoh(ar, ai, ldt):
        dt = jnp.exp(ldt)
        mag = jnp.exp(ar * dt)
        abr, abi = mag * jnp.cos(ai * dt), mag * jnp.sin(ai * dt)
        den = ar * ar + ai * ai
        nr, ni = abr - 1.0, abi
        return abr, abi, (nr * ar + ni * ai) / den, (ni * ar - nr * ai) / den

    abr, abi, _, _ = zoh(are_ref[...], aim_ref[...], ldt_ref[...])
    pr = jnp.broadcast_to(abr, (rows, abr.shape[1]))
    pi = jnp.broadcast_to(abi, (rows, abr.shape[1]))
    row = lax.broadcasted_iota(I32, pr.shape, 0)
    d = 1
    while d < rows:
        sr = pltpu.roll(pr, d, 0)
        si = pltpu.roll(pi, d, 0)
        m = row >= d
        pr, pi = jnp.where(m, pr * sr - pi * si, pr), jnp.where(m, pr * si + pi * sr, pi)
        d *= 2
    powre_ref[...] = pr
    powim_ref[...] = pi
    _, _, cr, ci = zoh(arec_ref[...], aimc_ref[...], ldtc_ref[...])
    br, bi = bre_ref[...], bim_ref[...]
    bbre_ref[...] = cr * br - ci * bi
    bbim_ref[...] = cr * bi + ci * br


def s5_discretise(a_re, a_im, log_dt, b_re, b_im, rows):
    g, p = a_re.shape
    w = g * p
    ldt = jnp.broadcast_to(log_dt[:, None], (g, p))
    args = (a_re.reshape(1, w), a_im.reshape(1, w), ldt.reshape(1, w),
            a_re.reshape(w, 1), a_im.reshape(w, 1), ldt.reshape(w, 1),
            b_re.reshape(w, S5_GROUP), b_im.reshape(w, S5_GROUP))
    return pl.pallas_call(
        functools.partial(_s5_disc_kernel, rows=rows),
        out_shape=[jax.ShapeDtypeStruct((rows, w), F32), jax.ShapeDtypeStruct((rows, w), F32),
                   jax.ShapeDtypeStruct((w, S5_GROUP), F32), jax.ShapeDtypeStruct((w, S5_GROUP), F32)],
        name="s5_discretise",
    )(*args)


def _s5_head(y, u, d_ref, gluw_ref, glub_ref):
    y = _gelu_tanh(y + d_ref[...] * u)
    return y * _sigmoid(_dot(y.astype(BF16), gluw_ref[...]) + glub_ref[...])


def _s5_kernel(u_ref, h0_ref, powre_ref, powim_ref, wb_ref, wc_ref, d_ref, gluw_ref, glub_ref,
               y_ref, ht_ref, cre, cim):
    t = pl.program_id(1)
    rows = u_ref.shape[0]

    @pl.when(t == 0)
    def _():
        cre[...] = h0_ref[:, :S5_W]
        cim[...] = h0_ref[:, S5_W:]

    u = u_ref[...]
    bu = _dot(u.astype(BF16), wb_ref[...])
    hr, hi = bu[:, :S5_W], bu[:, S5_W:]
    row = lax.broadcasted_iota(I32, hr.shape, 0)
    d = 1
    while d < rows:
        ar, ai = powre_ref[d - 1:d, :], powim_ref[d - 1:d, :]
        sr, si = pltpu.roll(hr, d, 0), pltpu.roll(hi, d, 0)
        m = row >= d
        hr, hi = (hr + jnp.where(m, ar * sr - ai * si, 0.0), hi + jnp.where(m, ar * si + ai * sr, 0.0))
        d *= 2
    pr, pi = powre_ref[...], powim_ref[...]
    c_r, c_i = cre[...], cim[...]
    hr, hi = hr + (pr * c_r - pi * c_i), hi + (pr * c_i + pi * c_r)
    cre[...] = hr[rows - 1:rows, :]
    cim[...] = hi[rows - 1:rows, :]
    ht_ref[:, :S5_W] = hr[rows - 1:rows, :]
    ht_ref[:, S5_W:] = hi[rows - 1:rows, :]
    y = _dot(hr.astype(BF16), wc_ref[:S5_W, :]) + _dot(hi.astype(BF16), wc_ref[S5_W:, :])
    y_ref[...] = _s5_head(y, u, d_ref, gluw_ref, glub_ref)


def s5_prompt(u, h0, powre, powim, wb, wc, d, gluw, glub):
    b, s, _ = u.shape
    rows = powre.shape[0]
    const = lambda a: pl.BlockSpec(a.shape, lambda i, j: (0,) * a.ndim)
    return pl.pallas_call(
        _s5_kernel,
        grid=(b, s // rows),
        in_specs=[pl.BlockSpec((None, rows, D_S5), lambda i, j: (i, j, 0)),
                  pl.BlockSpec((None, 1, 2 * S5_W), lambda i, j: (i, 0, 0)),
                  const(powre), const(powim), const(wb), const(wc), const(d), const(gluw), const(glub)],
        out_specs=[pl.BlockSpec((None, rows, D_S5), lambda i, j: (i, j, 0)),
                   pl.BlockSpec((None, 1, 2 * S5_W), lambda i, j: (i, 0, 0))],
        out_shape=[jax.ShapeDtypeStruct((b, s, D_S5), F32), jax.ShapeDtypeStruct((b, 1, 2 * S5_W), F32)],
        scratch_shapes=[pltpu.VMEM((1, S5_W), F32), pltpu.VMEM((1, S5_W), F32)],
        compiler_params=_cparams(("arbitrary", "arbitrary")),
        name="s5_prompt",
    )(u, h0, powre, powim, wb, wc, d, gluw, glub)


def _s5_step_kernel(u_ref, h0_ref, powre_ref, powim_ref, wb_ref, wc_ref, d_ref, gluw_ref, glub_ref,
                    y_ref, ht_ref):
    ar, ai = powre_ref[0:1, :], powim_ref[0:1, :]
    hr, hi = h0_ref[:, :S5_W], h0_ref[:, S5_W:]
    for t in range(u_ref.shape[0]):
        u = u_ref[t]
        bu = _dot(u.astype(BF16), wb_ref[...])
        hr, hi = ar * hr - ai * hi + bu[:, :S5_W], ar * hi + ai * hr + bu[:, S5_W:]
        y = _dot(hr.astype(BF16), wc_ref[:S5_W, :]) + _dot(hi.astype(BF16), wc_ref[S5_W:, :])
        y_ref[t] = _s5_head(y, u, d_ref, gluw_ref, glub_ref)
    ht_ref[:, :S5_W] = hr
    ht_ref[:, S5_W:] = hi


def s5_sample(u_tm, h0, powre, powim, wb, wc, d, gluw, glub):
    s, b, _ = u_tm.shape
    return pl.pallas_call(
        _s5_step_kernel,
        out_shape=[jax.ShapeDtypeStruct((s, b, D_S5), F32), jax.ShapeDtypeStruct((b, 2 * S5_W), F32)],
        compiler_params=pltpu.CompilerParams(vmem_limit_bytes=VMEM_LIMIT),
        name="s5_sample",
    )(u_tm, h0, powre, powim, wb, wc, d, gluw, glub)


def s5_matrices(bb_re, bb_im, c_re, c_im):
    g, p, h = S5_GROUPS, S5_STATE, S5_GROUP
    eye = jnp.eye(g, dtype=F32)
    bd_in = lambda bb: jnp.einsum('gph,gk->ghkp', bb.reshape(g, p, h), eye).reshape(g * h, g * p)
    bd_out = lambda c: jnp.einsum('ghp,gk->gpkh', c, eye).reshape(g * p, g * h)
    wb = jnp.concatenate([bd_in(bb_re), bd_in(bb_im)], axis=1).astype(BF16)
    wc = jnp.concatenate([bd_out(c_re), -bd_out(c_im)], axis=0).astype(BF16)
    return wb, wc


RW_PACK = 8 * D_RW


def _head_ones(n):
    r = lax.broadcasted_iota(I32, (n, n), 0) // HEAD_DIM
    c = lax.broadcasted_iota(I32, (n, n), 1) // HEAD_DIM
    return jnp.where(r == c, 1.0, 0.0).astype(BF16)


def _seg_sum(x, ones_bd):
    hi, lo = _split(x)
    return _dot(hi, ones_bd) + _dot(lo, ones_bd)


def _softplus(x):
    return jnp.maximum(x, 0.0) + jnp.log(1.0 + jnp.exp(-jnp.abs(x)))


def _rw_pre_kernel(cols_ref, shift0_ref, mu_ref, w0_ref, a0_ref, wlr_ref, kk_ref, ka_ref, rk_ref,
                   o_ref, carry, *, chunk, valid):
    @pl.when(pl.program_id(1) == 0)
    def _():
        carry[...] = shift0_ref[...]

    cf = cols_ref[...]
    tm = cf.shape[0]
    row = lax.broadcasted_iota(I32, (tm, 1), 0)
    prev = jnp.where(row == 0, carry[...], pltpu.roll(cf, 1, 0))
    carry[...] = cf[tm - 1:tm, :]
    xs = cf + (prev - cf) * mu_ref[...]
    r, k, v = xs[:, :D_RW], xs[:, D_RW:2 * D_RW], xs[:, 2 * D_RW:3 * D_RW]
    lr = xs[:, 3 * D_RW:]
    lane = lax.broadcasted_iota(I32, lr.shape, 1)
    t = jnp.where(lane < RW_W_RANK, jnp.tanh(lr), jnp.where(lane < RW_W_RANK + RW_A_RANK, lr, _sigmoid(lr)))
    proj = _dot3(t, wlr_ref[...])
    w = -_softplus(-(w0_ref[...] + proj[:, :D_RW])) - 0.5
    logw = -jnp.exp(w)
    a = _sigmoid(a0_ref[...] + proj[:, D_RW:2 * D_RW])
    g = proj[:, 2 * D_RW:]
    ones_bd = _head_ones(D_RW)
    kk = k * kk_ref[...]
    kk = kk / jnp.maximum(jnp.sqrt(_seg_sum(kk * kk, ones_bd)), 1e-12)
    km = k * (1.0 + (a - 1.0) * ka_ref[...])
    bonus = _seg_sum(r * km * rk_ref[...], ones_bd) * v
    pos = row % chunk
    if valid < chunk:
        live = pos < valid
        zero = jnp.zeros_like(r)
        logw, kk, km, v, r = (jnp.where(live, logw, zero), jnp.where(live, kk, zero), jnp.where(live, km, zero),
                              jnp.where(live, v, zero), jnp.where(live, r, zero))
    gc = logw
    d = 1
    while d < chunk:
        gc = gc + jnp.where(pos >= d, pltpu.roll(gc, d, 0), 0.0)
        d *= 2
    eg = jnp.exp(gc)
    eng = jnp.exp(-gc)
    o_ref[:, 0 * D_RW:1 * D_RW] = r * eg
    o_ref[:, 1 * D_RW:2 * D_RW] = kk * jnp.exp(gc - logw)
    o_ref[:, 2 * D_RW:3 * D_RW] = kk * a * eng
    o_ref[:, 3 * D_RW:4 * D_RW] = km * eng
    o_ref[:, 4 * D_RW:5 * D_RW] = v
    o_ref[:, 5 * D_RW:6 * D_RW] = eg
    o_ref[:, 6 * D_RW:7 * D_RW] = bonus
    o_ref[:, 7 * D_RW:8 * D_RW] = g


def rwkv_prepare(cols, shift0, mu, w0, a0, wlr, k_k, k_a, r_k, tm, chunk, valid):
    b, s, c = cols.shape
    const = lambda a: pl.BlockSpec(a.shape, lambda i, j: (0,) * a.ndim)
    return pl.pallas_call(
        functools.partial(_rw_pre_kernel, chunk=chunk, valid=valid),
        grid=(b, s // tm),
        in_specs=[pl.BlockSpec((None, tm, c), lambda i, j: (i, j, 0)),
                  pl.BlockSpec((None, 1, c), lambda i, j: (i, 0, 0)),
                  const(mu), const(w0), const(a0), const(wlr), const(k_k), const(k_a), const(r_k)],
        out_specs=pl.BlockSpec((None, tm, RW_PACK), lambda i, j: (i, j, 0)),
        out_shape=jax.ShapeDtypeStruct((b, s, RW_PACK), F32),
        scratch_shapes=[pltpu.VMEM((1, c), F32)],
        compiler_params=_cparams(("arbitrary", "arbitrary")),
        name="rwkv_prepare",
    )(cols, shift0, mu, w0, a0, wlr, k_k, k_a, r_k)


def _rw_chunk_kernel(x_ref, s0_ref, lnw_ref, lnb_ref, y_ref, st_ref, s_scr, y_scr, *, chunk):
    @pl.when(pl.program_id(1) == 0)
    def _():
        s_scr[...] = s0_ref[...]

    rows = x_ref.shape[0]
    hd = HEAD_DIM
    ri = lax.broadcasted_iota(I32, (chunk, chunk), 0)
    ci = lax.broadcasted_iota(I32, (chunk, chunk), 1)
    strict = ri > ci
    incl = ri >= ci
    eye = jnp.where(ri == ci, 1.0, 0.0)
    ek = lax.broadcasted_iota(I32, (hd, hd), 0) == lax.broadcasted_iota(I32, (hd, hd), 1)
    eye_k = jnp.where(ek, 1.0, 0.0)
    n_sq = chunk.bit_length() - 2
    n_chunks = rows // chunk
    probs = [(c, h) for c in range(n_chunks) for h in range(N_RW_HEADS)]
    each = lambda f: [f(i) for i in range(len(probs))]

    def col(i, j):
        c, h = probs[i]
        return x_ref[c * chunk:(c + 1) * chunk, j * D_RW + h * hd:j * D_RW + (h + 1) * hd]

    rt, kt, bt, km, v = (each(lambda i: col(i, j)) for j in range(5))
    lhs = each(lambda i: _split(jnp.concatenate([kt[i], rt[i]], axis=0)))
    bts, kms, vs = each(lambda i: _split(bt[i])), each(lambda i: _split(km[i])), each(lambda i: _split(v[i]))
    gb = each(lambda i: _mm3_nt(lhs[i], bts[i]))
    gk = each(lambda i: _mm3_nt(lhs[i], kms[i]))
    a_bb = each(lambda i: jnp.where(strict, gb[i][:chunk], 0.0))
    a_rb = each(lambda i: _split(jnp.where(incl, gb[i][chunk:], 0.0)))
    a_kr = each(lambda i: _split(jnp.concatenate([jnp.where(strict, gk[i][:chunk], 0.0),
                                                  jnp.where(incl, gk[i][chunk:], 0.0)], axis=0)))
    av = each(lambda i: _mm3(a_kr[i], vs[i]))
    minv = each(lambda i: eye - a_bb[i])
    p = a_bb
    for _ in range(n_sq):
        ps = each(lambda i: _split(p[i]))
        p = each(lambda i: _mm3(ps[i], ps[i]))
        minv = each(lambda i: minv[i] + _mm3(_split(minv[i]), _split(p[i])))
    minvs = each(lambda i: _split(minv[i]))
    khat = each(lambda i: _mm3(minvs[i], _split(kt[i])))
    p1 = each(lambda i: _mm3(minvs[i], _split(av[i][:chunk])))
    rhat = each(lambda i: rt[i] - _mm3(a_rb[i], _split(khat[i])))
    y1 = each(lambda i: av[i][chunk:] - _mm3(a_rb[i], _split(p1[i])))
    tb = each(lambda i: _mm3(_split(jnp.concatenate([khat[i].T, p1[i].T], axis=0)), bts[i]))
    vk = each(lambda i: _mm3(_split(v[i].T), kms[i]))

    def eg_last(i):
        c, h = probs[i]
        return x_ref[(c + 1) * chunk - 1:(c + 1) * chunk, 5 * D_RW + h * hd:5 * D_RW + (h + 1) * hd]

    gmat = each(lambda i: (eye_k - tb[i][:hd]) * eg_last(i))
    umat = each(lambda i: (vk[i] - tb[i][hd:]) * eg_last(i))
    state = [s_scr[h] for h in range(N_RW_HEADS)]
    for i, (c, h) in enumerate(probs):
        y_scr[c * chunk:(c + 1) * chunk, h * hd:(h + 1) * hd] = _dot3_nt(rhat[i], state[h]) + y1[i]
        state[h] = _dot3(state[h], gmat[i]) + umat[i]
    for h in range(N_RW_HEADS):
        s_scr[h] = state[h]
    st_ref[...] = s_scr[...]
    y = y_scr[...]
    ones_bd = _head_ones(D_RW)
    mean = _seg_sum(y, ones_bd) * (1.0 / hd)
    yc = y - mean
    var = _seg_sum(yc * yc, ones_bd) * (1.0 / hd)
    yn = yc * lax.rsqrt(var + RW_GN_EPS) * lnw_ref[...] + lnb_ref[...]
    y_ref[...] = (yn + x_ref[:, 6 * D_RW:7 * D_RW]) * x_ref[:, 7 * D_RW:8 * D_RW]


def rwkv_chunked(packed, state0, ln_w, ln_b, rows, chunk):
    b, s, _ = packed.shape
    const = lambda a: pl.BlockSpec(a.shape, lambda i, j: (0,) * a.ndim)
    st_spec = pl.BlockSpec((None, N_RW_HEADS, HEAD_DIM, HEAD_DIM), lambda i, j: (i, 0, 0, 0))
    return pl.pallas_call(
        functools.partial(_rw_chunk_kernel, chunk=chunk),
        grid=(b, s // rows),
        in_specs=[pl.BlockSpec((None, rows, RW_PACK), lambda i, j: (i, j, 0)), st_spec, const(ln_w), const(ln_b)],
        out_specs=[pl.BlockSpec((None, rows, D_RW), lambda i, j: (i, j, 0)), st_spec],
        out_shape=[jax.ShapeDtypeStruct((b, s, D_RW), F32),
                   jax.ShapeDtypeStruct((b, N_RW_HEADS, HEAD_DIM, HEAD_DIM), F32)],
        scratch_shapes=[pltpu.VMEM((N_RW_HEADS, HEAD_DIM, HEAD_DIM), F32), pltpu.VMEM((rows, D_RW), F32)],
        compiler_params=_cparams(("arbitrary", "arbitrary")),
        name="rwkv_chunked",
    )(packed, state0, ln_w, ln_b)


def rwkv_lowrank_matrix(w2, a2, g2):
    z = lambda r: jnp.zeros((r, D_RW), F32)
    return jnp.concatenate([
        jnp.concatenate([w2, z(RW_W_RANK), z(RW_W_RANK)], axis=1),
        jnp.concatenate([z(RW_A_RANK), a2, z(RW_A_RANK)], axis=1),
        jnp.concatenate([z(RW_G_RANK), z(RW_G_RANK), g2], axis=1)], axis=0)


def _float_key(x):
    b = lax.bitcast_convert_type(x, I32)
    return jnp.where(b < 0, jnp.int32(INT_MIN) - b, b)


def _py_key(v):
    import numpy as np
    b = int(np.float32(v).view(np.int32))
    return -(b & 0x7FFFFFFF) if b < 0 else b


KEY_NEG_INF = _py_key(NEG_INF)


def _count_ge(sc_ref, nc, cand):
    parts = [jnp.where(sc_ref[c] >= cand, 1.0, 0.0) for c in range(nc)]
    while len(parts) > 1:
        parts = [parts[i] + parts[i + 1] for i in range(0, len(parts) - 1, 2)] + (parts[-1:] if len(parts) % 2 else [])
    return jnp.sum(parts[0], axis=1, keepdims=True)


def _select_threshold(sc_ref, nc, k):
    _, r, wc = sc_ref.shape

    def bit_body(i, carry):
        t, cge = carry
        cand = t + jnp.left_shift(jnp.int32(1), 31 - i)
        cnt = _count_ge(sc_ref, nc, cand)
        ok = cnt >= k
        return jnp.where(ok, cand, t), jnp.where(ok, cnt, cge)

    t0 = jnp.full((r, 1), INT_MIN, I32)
    cge0 = jnp.zeros((r, 1), F32) + jnp.asarray(nc * wc, F32)
    return lax.fori_loop(0, 32, bit_body, (t0, cge0))


def _resolve_ties(sc_ref, nc, t, need):
    _, r, wc = sc_ref.shape
    upper = jnp.where(lax.broadcasted_iota(I32, (wc, wc), 0) < lax.broadcasted_iota(I32, (wc, wc), 1), 1.0, 0.0)

    def body(c, offs):
        x = sc_ref[c]
        e = x == t
        ef = jnp.where(e, 1.0, 0.0)
        rank = _dot(ef, upper) + offs
        sc_ref[c] = jnp.where(e, jnp.where(rank >= need, jnp.int32(INT_MIN), x), x)
        return offs + jnp.sum(ef, axis=1, keepdims=True)

    lax.fori_loop(0, nc, body, jnp.zeros((r, 1), F32))


def _topk_mask_prepare(sc_ref, nc, k, valid_rows):
    t, cge = _select_threshold(sc_ref, nc, k)
    r = t.shape[0]
    live = lax.broadcasted_iota(I32, (r, 1), 0) < valid_rows
    tied = live & (cge > k) & (t != KEY_NEG_INF)

    @pl.when(jnp.max(jnp.where(tied, 1.0, 0.0)) > 0.0)
    def _():
        cgt = _count_ge(sc_ref, nc, t + 1)
        _resolve_ties(sc_ref, nc, t, k - cgt)

    return t


def _indexer_total(s4, wcols, rows):
    tot = jnp.maximum(s4[:rows], 0.0) * wcols[0]
    for h in range(1, N_IDX_HEADS):
        tot = tot + jnp.maximum(s4[h * rows:(h + 1) * rows], 0.0) * wcols[h]
    return tot


SUBLANES = 8


def _col_tree(x, op, rows=SUBLANES):
    parts = [x[i:i + rows] for i in range(0, x.shape[0], rows)]
    while len(parts) > 1:
        parts = [op(parts[i], parts[i + 1]) for i in range(0, len(parts) - 1, 2)] + (parts[-1:] if len(parts) % 2 else [])
    return parts[0]


def _col_sum(x):
    return _col_tree(x, jnp.add)


def _col_max(x):
    return _col_tree(x, jnp.maximum)


def _count_ge_t(sc_ref, nc, cand):
    _, wc, nq = sc_ref.shape
    hits = lambda c: _col_sum(jnp.where(sc_ref[c] >= cand, 1.0, 0.0))

    def pair_body(i, acc):
        return acc + (hits(2 * i) + hits(2 * i + 1))

    acc = lax.fori_loop(0, nc // 2, pair_body, jnp.zeros((SUBLANES, nq), F32))
    acc = lax.cond(nc % 2 == 1, lambda a: a + hits(nc - 1), lambda a: a, acc)
    return jnp.sum(acc, axis=0, keepdims=True)


def _topk_mask_prepare_t(sc_ref, nc, k):
    _, wc, nq = sc_ref.shape

    def bit_body(i, carry):
        t, cge = carry
        cand = t + jnp.left_shift(jnp.int32(1), 31 - i)
        cnt = _count_ge_t(sc_ref, nc, cand)
        ok = cnt >= k
        return jnp.where(ok, cand, t), jnp.where(ok, cnt, cge)

    t0 = jnp.full((1, nq), INT_MIN, I32)
    cge0 = jnp.zeros((1, nq), F32) + jnp.asarray(nc * wc, F32)
    t, cge = lax.fori_loop(0, 32, bit_body, (t0, cge0))
    tied = (cge > k) & (t != KEY_NEG_INF)

    @pl.when(jnp.max(jnp.where(tied, 1.0, 0.0)) > 0.0)
    def _():
        need = k - _count_ge_t(sc_ref, nc, t + 1)
        ri = lax.broadcasted_iota(I32, (wc, wc), 0)
        ci = lax.broadcasted_iota(I32, (wc, wc), 1)
        lower = jnp.where(ci < ri, 1.0, 0.0)

        def body(c, offs):
            x = sc_ref[c]
            e = x == t
            ef = jnp.where(e, 1.0, 0.0)
            rank = _dot(lower, ef) + offs
            sc_ref[c] = jnp.where(e, jnp.where(rank >= need, jnp.int32(INT_MIN), x), x)
            return offs + jnp.sum(ef, axis=0, keepdims=True)

        lax.fori_loop(0, nc, body, jnp.zeros((1, nq), F32))

    return t


def _dsa_prompt_kernel(q_ref, kb_ref, vt_ref, qi3_ref, ki3_ref, kiw_ref, o_ref, sc, s_even, s_odd, *, k_sel):
    j = pl.program_id(1)
    qb, wc = Q_BLOCK, KEY_CHUNK
    nck = (j * qb + qb + wc - 1) // wc
    qpos = j * qb + lax.broadcasted_iota(I32, (1, qb), 1)
    keybase = lax.broadcasted_iota(I32, (wc, 1), 0)

    lhs = jnp.concatenate([qi3_ref[:, 2 * LANES * h:2 * LANES * (h + 1)] for h in range(N_IDX_HEADS)], axis=0)
    wrows = [kiw_ref[IDX_DIM + h:IDX_DIM + h + 1, :] for h in range(N_IDX_HEADS)]

    def head_scores(c):
        kc = ki3_ref[pl.ds(pl.multiple_of(c * wc, wc), wc), :]
        return _dot_nt(kc, lhs)

    def store_keys(c, s4, diagonal):
        tot = jnp.maximum(s4[:, :qb], 0.0) * wrows[0]
        for h in range(1, N_IDX_HEADS):
            tot = tot + jnp.maximum(s4[:, h * qb:(h + 1) * qb], 0.0) * wrows[h]
        if diagonal:
            tot = jnp.where(c * wc + keybase <= qpos, tot, NEG_INF)
        sc[c] = _float_key(tot)

    def score_pair(i, carry):
        s_a, s_b = head_scores(2 * i), head_scores(2 * i + 1)
        store_keys(2 * i, s_a, False)
        store_keys(2 * i + 1, s_b, False)
        return carry

    lax.fori_loop(0, (nck - 1) // 2, score_pair, 0)

    @pl.when((nck - 1) % 2 == 1)
    def _():
        store_keys(nck - 2, head_scores(nck - 2), False)

    store_keys(nck - 1, head_scores(nck - 1), True)
    t = _topk_mask_prepare_t(sc, nck, k_sel)
    t_sel = jnp.maximum(t, KEY_NEG_INF + 1)

    lane = lax.broadcasted_iota(I32, (qb, LANES), 1)
    n_pairs = N_ATT_HEADS // 2
    qpair = []
    for pr in range(n_pairs):
        blk = q_ref[:, pr * LANES:(pr + 1) * LANES]
        zero = jnp.zeros_like(blk)
        qpair.append(jnp.concatenate([jnp.where(lane < HEAD_DIM, blk, zero), jnp.where(lane < HEAD_DIM, zero, blk)],
                                     axis=0))

    ac = ATT_CHUNK
    per = wc // ac
    n_att = (j * qb + qb + ac - 1) // ac

    def scores_to(c, buf):
        off = pl.multiple_of(c * ac, ac)
        keys = sc[c // per, pl.ds(pl.multiple_of((c % per) * ac, ac), ac), :]
        bias = jnp.where(keys >= t_sel, 0.0, NEG_INF)
        bias2 = jnp.concatenate([bias, bias], axis=1)
        for pr in range(n_pairs):
            buf[pr] = _dot_nt(kb_ref[pl.ds(off, ac), pr * LANES:(pr + 1) * LANES], qpair[pr]) + bias2

    def consume(c, buf, carry):
        ms, ls, accs = carry
        off = pl.multiple_of(c * ac, ac)
        ss = [buf[pr] for pr in range(n_pairs)]
        new_m = [jnp.maximum(ms[pr], jnp.max(_col_max(ss[pr]), axis=0, keepdims=True)) for pr in range(n_pairs)]
        alphas = [jnp.exp2(ms[pr] - new_m[pr]) for pr in range(n_pairs)]
        ps = [jnp.exp2(ss[pr] - new_m[pr]) for pr in range(n_pairs)]
        new_l = [alphas[pr] * ls[pr] + jnp.sum(_col_sum(ps[pr]), axis=0, keepdims=True) for pr in range(n_pairs)]
        new_acc = [alphas[pr] * accs[pr]
                   + _dot(vt_ref[pr * LANES:(pr + 1) * LANES, pl.ds(off, ac)], ps[pr].astype(BF16))
                   for pr in range(n_pairs)]
        return tuple(new_m), tuple(new_l), tuple(new_acc)

    def att_body(i, carry):
        scores_to(2 * i + 1, s_odd)
        carry = consume(2 * i, s_even, carry)
        scores_to(jnp.minimum(2 * i + 2, n_att - 1), s_even)
        return consume(2 * i + 1, s_odd, carry)

    init = (tuple(jnp.full((1, 2 * qb), NEG_INF, F32) for _ in range(n_pairs)),
            tuple(jnp.zeros((1, 2 * qb), F32) for _ in range(n_pairs)),
            tuple(jnp.zeros((LANES, 2 * qb), F32) for _ in range(n_pairs)))
    scores_to(0, s_even)
    carry = lax.fori_loop(0, n_att // 2, att_body, init)
    _, ls, accs = lax.cond(n_att % 2 == 1, lambda cr: consume(n_att - 1, s_even, cr), lambda cr: cr, carry)
    sub = lax.broadcasted_iota(I32, (LANES, qb), 0)
    for pr in range(n_pairs):
        o2 = accs[pr] / ls[pr]
        o_ref[:, pr * LANES:(pr + 1) * LANES] = jnp.where(sub < HEAD_DIM, o2[:, :qb], o2[:, qb:]).T


def dsa_prompt(q, kb, vt, qi3, ki3, kiw_t):
    b, s, _ = q.shape
    k_sel = min(TOPK_MAX, s // 4)
    blk = lambda w: pl.BlockSpec((None, Q_BLOCK, w), lambda i, j: (i, j, 0))
    full = lambda w: pl.BlockSpec((None, s, w), lambda i, j: (i, 0, 0))
    return pl.pallas_call(
        functools.partial(_dsa_prompt_kernel, k_sel=k_sel),
        grid=(b, s // Q_BLOCK),
        in_specs=[blk(D_ATT), full(D_ATT), pl.BlockSpec((None, D_ATT, s), lambda i, j: (i, 0, 0)),
                  blk(N_IDX_HEADS * 2 * LANES), full(2 * LANES),
                  pl.BlockSpec((None, LANES, Q_BLOCK), lambda i, j: (i, 0, j))],
        out_specs=blk(D_ATT),
        out_shape=jax.ShapeDtypeStruct((b, s, D_ATT), F32),
        scratch_shapes=[pltpu.VMEM((s // KEY_CHUNK, KEY_CHUNK, Q_BLOCK), I32),
                        pltpu.VMEM((N_ATT_HEADS // 2, ATT_CHUNK, 2 * Q_BLOCK), F32),
                        pltpu.VMEM((N_ATT_HEADS // 2, ATT_CHUNK, 2 * Q_BLOCK), F32)],
        compiler_params=_cparams(("arbitrary", "arbitrary")),
        name="dsa_prompt",
    )(q, kb, vt, qi3, ki3, kiw_t)


SAMPLE_ROWS = 8
PAGE_GROUP = 16
IDX_PAGE_GROUP = 32


def _dsa_sample_index_kernel(pt_ref, *refs, n_valid, k_sel):
    pages = refs[:IDX_PAGE_GROUP]
    qi3_ref, kiw_ref, ki3n_ref, sc_ref, thr_ref = refs[IDX_PAGE_GROUP:]
    g = pl.program_id(1)
    r = SAMPLE_ROWS
    n_chunks = sc_ref.shape[0]
    q3 = qi3_ref[...].astype(F32)
    qhi = jnp.concatenate([q3[:, 2 * LANES * h:2 * LANES * h + IDX_DIM] for h in range(N_IDX_HEADS)],
                          axis=0).astype(BF16)
    qlo = jnp.concatenate([q3[:, 2 * LANES * h + LANES:2 * LANES * h + LANES + IDX_DIM]
                           for h in range(N_IDX_HEADS)], axis=0).astype(BF16)
    wcols = [kiw_ref[:, IDX_DIM + h:IDX_DIM + h + 1] for h in range(N_IDX_HEADS)]

    def scores(kt):
        khi, klo = _split(kt)
        return _dot(qhi, khi) + (_dot(qhi, klo) + _dot(qlo, khi))

    for i in range(IDX_PAGE_GROUP):
        sc_ref[g * IDX_PAGE_GROUP + i] = _float_key(_indexer_total(scores(pages[i][...]), wcols, r))

    @pl.when(g == pl.num_programs(1) - 1)
    def _():
        tot = _indexer_total(scores(ki3n_ref[...]), wcols, r)
        row = lax.broadcasted_iota(I32, tot.shape, 0)
        col = lax.broadcasted_iota(I32, tot.shape, 1)
        tot = jnp.where((col <= row) & (col < n_valid), tot, NEG_INF)
        sc_ref[n_chunks - 1] = _float_key(tot)
        t = _topk_mask_prepare(sc_ref, n_chunks, k_sel, n_valid)
        thr_ref[...] = jnp.broadcast_to(t, thr_ref.shape)


def dsa_sample_index(page_table, cache_kidx_t, layer, qi3, kiw, ki_new_t, n_valid):
    b, n_pages = page_table.shape
    n_chunks = n_pages + 1
    k_sel = min(TOPK_MAX, (n_pages * PAGE_SIZE + n_valid) // 4)
    page_spec = lambda i: pl.BlockSpec((None, None, IDX_DIM, PAGE_SIZE),
                                       lambda bi, g, pt: (layer, pt[bi, g * IDX_PAGE_GROUP + i], 0, 0))
    per_b = lambda shape: pl.BlockSpec((None,) + shape, lambda bi, g, pt: (bi,) + (0,) * len(shape))
    r = SAMPLE_ROWS
    return pl.pallas_call(
        functools.partial(_dsa_sample_index_kernel, n_valid=n_valid, k_sel=k_sel),
        grid_spec=pltpu.PrefetchScalarGridSpec(
            num_scalar_prefetch=1,
            grid=(b, n_pages // IDX_PAGE_GROUP),
            in_specs=[page_spec(i) for i in range(IDX_PAGE_GROUP)]
            + [per_b((r, N_IDX_HEADS * 2 * LANES)), per_b((r, LANES)), per_b((IDX_DIM, PAGE_SIZE))],
            out_specs=[per_b((n_chunks, r, PAGE_SIZE)), per_b((r, PAGE_SIZE))],
        ),
        out_shape=[jax.ShapeDtypeStruct((b, n_chunks, r, PAGE_SIZE), I32),
                   jax.ShapeDtypeStruct((b, r, PAGE_SIZE), I32)],
        compiler_params=_cparams(("arbitrary", "arbitrary")),
        name="dsa_sample_index",
    )(page_table, *([cache_kidx_t] * IDX_PAGE_GROUP), qi3, kiw, ki_new_t)


def _dsa_sample_attend_kernel(pt_ref, *refs, n_valid):
    kpages = refs[:PAGE_GROUP]
    vpages = refs[PAGE_GROUP:2 * PAGE_GROUP]
    qbd_ref, sel_ref, seln_ref, thr_ref, kn_ref, vn_ref, o_ref, m_scr, l_scr, acc_scr = refs[2 * PAGE_GROUP:]
    g = pl.program_id(1)
    r = SAMPLE_ROWS

    @pl.when(g == 0)
    def _():
        m_scr[...] = jnp.full(m_scr.shape, NEG_INF, F32)
        l_scr[...] = jnp.zeros(l_scr.shape, F32)
        acc_scr[...] = jnp.zeros(acc_scr.shape, F32)

    qbd = qbd_ref[...]
    thr = thr_ref[...]
    per_head = lambda x: jnp.concatenate([x] * N_ATT_HEADS, axis=0)
    thr_all = per_head(thr)

    def update(sels, extra, kts, vts):
        n = len(kts)
        scores = jnp.concatenate([_dot(qbd, kt) for kt in kts], axis=1)
        mask = jnp.concatenate([per_head(sel) for sel in sels], axis=1) >= jnp.concatenate([thr_all] * n, axis=1)
        if extra is not None:
            mask = mask & (per_head(extra) > 0)
        s = jnp.where(mask, scores, NEG_INF)
        m_prev = m_scr[...]
        m_new = jnp.maximum(m_prev, jnp.max(s, axis=1, keepdims=True))
        alpha = jnp.exp(m_prev - m_new)
        p = jnp.exp(s - m_new)
        l_scr[...] = alpha * l_scr[...] + jnp.sum(p, axis=1, keepdims=True)
        p = p.astype(BF16)
        pv = _dot_nt(p[:, :PAGE_SIZE], vts[0])
        for i in range(1, n):
            pv = pv + _dot_nt(p[:, i * PAGE_SIZE:(i + 1) * PAGE_SIZE], vts[i])
        acc_scr[...] = alpha * acc_scr[...] + pv
        m_scr[...] = m_new

    update([sel_ref[i] for i in range(PAGE_GROUP)], None,
           [kpages[i][...].astype(BF16) for i in range(PAGE_GROUP)],
           [vpages[i][...].astype(BF16) for i in range(PAGE_GROUP)])

    @pl.when(g == pl.num_programs(1) - 1)
    def _():
        row = lax.broadcasted_iota(I32, thr.shape, 0)
        col = lax.broadcasted_iota(I32, thr.shape, 1)
        causal = jnp.where((col <= row) & (col < n_valid), 1, 0)
        update([seln_ref[...]], causal, [kn_ref[...]], [vn_ref[...]])
        out = acc_scr[...] / l_scr[...]
        lane = lax.broadcasted_iota(I32, (r, D_ATT), 1) // HEAD_DIM
        tot = jnp.where(lane == 0, out[:r], 0.0)
        for h in range(1, N_ATT_HEADS):
            tot = tot + jnp.where(lane == h, out[h * r:(h + 1) * r], 0.0)
        o_ref[...] = tot


def dsa_sample_attend(page_table, cache_kt, cache_vt, layer, q_bd, sel, thr, k_new_t, v_new_t, n_valid):
    b, n_pages = page_table.shape
    r = SAMPLE_ROWS
    page_spec = lambda i: pl.BlockSpec((None, None, D_ATT, PAGE_SIZE),
                                       lambda bi, g, pt: (layer, pt[bi, g * PAGE_GROUP + i], 0, 0))
    per_b = lambda shape: pl.BlockSpec((None,) + shape, lambda bi, g, pt: (bi,) + (0,) * len(shape))
    rows = N_ATT_HEADS * r
    return pl.pallas_call(
        functools.partial(_dsa_sample_attend_kernel, n_valid=n_valid),
        grid_spec=pltpu.PrefetchScalarGridSpec(
            num_scalar_prefetch=1,
            grid=(b, n_pages // PAGE_GROUP),
            in_specs=[page_spec(i) for i in range(PAGE_GROUP)] + [page_spec(i) for i in range(PAGE_GROUP)]
            + [per_b((rows, D_ATT)),
               pl.BlockSpec((None, PAGE_GROUP, r, PAGE_SIZE), lambda bi, g, pt: (bi, g, 0, 0)),
               pl.BlockSpec((None, None, r, PAGE_SIZE), lambda bi, g, pt: (bi, n_pages, 0, 0)),
               per_b((r, PAGE_SIZE)), per_b((D_ATT, PAGE_SIZE)), per_b((D_ATT, PAGE_SIZE))],
            out_specs=per_b((r, D_ATT)),
            scratch_shapes=[pltpu.VMEM((rows, 1), F32), pltpu.VMEM((rows, 1), F32), pltpu.VMEM((rows, D_ATT), F32)],
        ),
        out_shape=jax.ShapeDtypeStruct((b, r, D_ATT), F32),
        compiler_params=_cparams(("arbitrary", "arbitrary")),
        name="dsa_sample_attend",
    )(page_table, *([cache_kt] * PAGE_GROUP), *([cache_vt] * PAGE_GROUP), q_bd, sel, sel, thr, k_new_t, v_new_t)


def pack_w_in(w_in):
    d = w_in.shape[0]
    pad = jnp.zeros((d, ZP_END - ZP_KIW - (OFF_RW - OFF_KI)), w_in.dtype)
    return jnp.concatenate([w_in[:, OFF_S5:OFF_KI], w_in[:, OFF_RW:], w_in[:, OFF_KI:OFF_RW], pad], axis=1).astype(BF16)


def rope_tables(pos):
    half = HEAD_DIM // 2
    inv = ROPE_THETA ** (-jnp.arange(half, dtype=F32) / half)
    ang = pos.astype(F32)[:, None] * inv[None, :]
    cos, sin = jnp.cos(ang), jnp.sin(ang)
    return jnp.tile(jnp.concatenate([cos, cos], 1), (1, 2)), jnp.tile(jnp.concatenate([-sin, sin], 1), (1, 2))


def block_diag_queries(q):
    b, r, d = q.shape
    head_of_lane = jnp.arange(d) // HEAD_DIM
    keep = head_of_lane[None, :] == jnp.arange(N_ATT_HEADS)[:, None]
    return jnp.where(keep[None, :, None, :], q[:, None], jnp.zeros((), q.dtype)).reshape(b, N_ATT_HEADS * r, d)


def _pad_rows(a, rows):
    return jnp.pad(a, ((0, 0), (0, rows - a.shape[1])) + ((0, 0),) * (a.ndim - 2))


Q_SCALE = HEAD_DIM ** -0.5
Q_SCALE_EXP2 = Q_SCALE * math.log2(math.e)
PROMPT_TILE = 256
RW_PRE_TILE = 512
RW_STEP_ROWS = 256


def kernel(x_prompt, x_sample, c_prompt, c_sample, cache_k, cache_v, cache_kidx, state_s5_re, state_s5_im, state_rwkv, state_rwkv_shift, page_table, ada_w, ada_b, norm_pre, norm_post, ffn_wi, ffn_wo, w_in, w_out, s5_a_re, s5_a_im, s5_log_dt, s5_b_re, s5_b_im, s5_c_re, s5_c_im, s5_d, s5_glu_w, s5_glu_b, rw_mu, rw_w0, rw_w2, rw_a0, rw_a2, rw_g2, rw_k_k, rw_k_a, rw_r_k, rw_ln_w, rw_ln_b):
    bp, sp, d = x_prompt.shape
    bs, ss, _ = x_sample.shape
    depth = ada_w.shape[0]
    past = page_table.shape[1] * PAGE_SIZE
    np_tok, ns_tok = bp * sp, bs * ss
    tpb = sp // PROMPT_TILE
    row1 = lambda a: a.reshape(1, -1)

    c_all = _pad_rows(jnp.concatenate([c_prompt, c_sample], axis=0)[None], -(-(bp + bs) // 8) * 8)[0]
    mod = ada_mod(c_all, ada_w, ada_b).reshape(depth, c_all.shape[0], N_SUB, 3, d)

    cos_p, sin_p = rope_tables(jnp.arange(sp))
    cos_s, sin_s = (jnp.tile(t, (bs, 1)) for t in rope_tables(past + jnp.arange(ss)))
    n_pool = cache_k.shape[1]
    cache_kt = jnp.transpose(cache_k, (0, 1, 3, 4, 2)).reshape(depth, n_pool, D_ATT, PAGE_SIZE)
    cache_vt = jnp.transpose(cache_v, (0, 1, 3, 4, 2)).reshape(depth, n_pool, D_ATT, PAGE_SIZE)
    cache_kidx_t = jnp.swapaxes(cache_kidx, 2, 3)

    xp = x_prompt.reshape(np_tok, d)
    xs = x_sample.reshape(ns_tok, d)
    outs_p, outs_s = [], []
    for l in range(depth):
        mod_p = mod[l, :bp]
        mod_s = jnp.repeat(mod[l, bp:bp + bs], ss, axis=0)
        pm = lambda i, j: mod_p[:, i, j][:, None, :]
        sm = lambda i, j: mod_s[:, i, j][None]
        npre = lambda i: row1(norm_pre[l, i])
        npost = lambda i: row1(norm_post[l, i])
        wi0, wo0 = ffn_wi[l, 0].astype(BF16), ffn_wo[l, 0].astype(BF16)
        wi1, wo1 = ffn_wi[l, 1].astype(BF16), ffn_wo[l, 1].astype(BF16)
        w_in_p = pack_w_in(w_in[l])
        w_out_b = w_out[l].astype(BF16)
        powre, powim, bb_re, bb_im = s5_discretise(s5_a_re[l], s5_a_im[l], s5_log_dt[l], s5_b_re[l], s5_b_im[l],
                                                   rows=S5_CHUNK)
        wb, wc = s5_matrices(bb_re, bb_im, s5_c_re[l], s5_c_im[l])
        s5_tail = (powre, powim, wb, wc, row1(s5_d[l]), s5_glu_w[l].astype(BF16), row1(s5_glu_b[l]))
        wlr = rwkv_lowrank_matrix(rw_w2[l], rw_a2[l], rw_g2[l])
        rw_pre = (row1(rw_mu[l]), row1(rw_w0[l]), row1(rw_a0[l]), wlr, row1(rw_k_k[l]), row1(rw_k_a[l]),
                  row1(rw_r_k[l]))
        ln = (row1(rw_ln_w[l]), row1(rw_ln_b[l]))

        xp = ffn_block(xp, pm(0, 0), pm(0, 1), pm(0, 2), npre(0), npost(0), wi0, wo0, PROMPT_TILE, tpb)
        (u, q, _, kb, _, _, qi3, _, ki3, rw, vt, kt_f, vt_f, kiw_t) = inproj_block(
            xp, pm(1, 0), pm(1, 1), npre(1), w_in_p, cos_p, sin_p, PROMPT_TILE, tpb, tpb, Q_SCALE_EXP2)
        seq = lambda a: a.reshape(bp, sp, a.shape[-1])
        y_s5, h_s5 = s5_prompt(seq(u), jnp.zeros((bp, 1, 2 * S5_W), F32), *s5_tail)
        y_att = dsa_prompt(seq(q), seq(kb), vt, seq(qi3), seq(ki3), kiw_t)
        cols = seq(rw)
        packed = rwkv_prepare(cols, jnp.zeros((bp, 1, N_RW_COLS), F32), *rw_pre, RW_PRE_TILE, RW_CHUNK, RW_CHUNK)
        y_rw, st_rw = rwkv_chunked(packed, jnp.zeros((bp, N_RW_HEADS, HEAD_DIM, HEAD_DIM), F32), *ln,
                                   RW_STEP_ROWS, RW_CHUNK)
        flat = lambda a: a.reshape(np_tok, a.shape[-1])
        xp = outffn_block(xp, flat(y_s5), flat(y_att), flat(y_rw), pm(1, 2), npost(1), w_out_b,
                          pm(2, 0), pm(2, 1), pm(2, 2), npre(2), npost(2), wi1, wo1, PROMPT_TILE, tpb)
        heads_last = lambda a: jnp.transpose(a.reshape(bp, N_ATT_HEADS, HEAD_DIM, sp), (0, 3, 1, 2))
        outs_p.append((heads_last(kt_f), heads_last(vt_f), jnp.swapaxes(kiw_t[:, :IDX_DIM, :], 1, 2),
                       h_s5[:, 0, :S5_W].reshape(bp, S5_GROUPS, S5_STATE),
                       h_s5[:, 0, S5_W:].reshape(bp, S5_GROUPS, S5_STATE), st_rw, cols[:, sp - 1]))

        xs = ffn_block(xs, sm(0, 0), sm(0, 1), sm(0, 2), npre(0), npost(0), wi0, wo0, ns_tok, 1)
        u, q, k, kb, v, vb, qi3, kiw, ki3, rw = inproj_block(xs, sm(1, 0), sm(1, 1), npre(1), w_in_p, cos_s, sin_s,
                                                             ns_tok, 1, 1, Q_SCALE)[:10]
        seq = lambda a: a.reshape(bs, ss, a.shape[-1])
        h0 = jnp.concatenate([state_s5_re[l].reshape(bs, S5_W), state_s5_im[l].reshape(bs, S5_W)], axis=1)
        y_s5, h_s5 = s5_sample(jnp.swapaxes(seq(u), 0, 1), h0, *s5_tail)
        y_s5 = jnp.swapaxes(y_s5, 0, 1)
        tok_t = lambda a: jnp.swapaxes(_pad_rows(a, PAGE_SIZE), 1, 2)
        sel, thr = dsa_sample_index(page_table, cache_kidx_t, l, _pad_rows(seq(qi3), SAMPLE_ROWS),
                                    _pad_rows(seq(kiw), SAMPLE_ROWS), tok_t(seq(kiw)[:, :, :IDX_DIM]), ss)
        q_bd = block_diag_queries(_pad_rows(seq(q), SAMPLE_ROWS))
        y_att = dsa_sample_attend(page_table, cache_kt, cache_vt, l, q_bd, sel, thr, tok_t(seq(kb)), tok_t(seq(vb)),
                                  ss)[:, :ss]
        cols = seq(rw)
        packed = rwkv_prepare(_pad_rows(cols, RW_CHUNK), state_rwkv_shift[l].reshape(bs, 1, N_RW_COLS), *rw_pre,
                              RW_CHUNK, RW_CHUNK, ss)
        y_rw, st_rw = rwkv_chunked(packed, state_rwkv[l], *ln, RW_CHUNK, RW_CHUNK)
        flat = lambda a: a.reshape(ns_tok, a.shape[-1])
        xs = outproj_block(xs, flat(y_s5), flat(y_att), flat(y_rw[:, :ss]), sm(1, 2), npost(1), w_out_b, ns_tok, 1)
        xs = ffn_block(xs, sm(2, 0), sm(2, 1), sm(2, 2), npre(2), npost(2), wi1, wo1, ns_tok, 1)
        outs_s.append((k.reshape(bs, ss, N_ATT_HEADS, HEAD_DIM), v.reshape(bs, ss, N_ATT_HEADS, HEAD_DIM),
                       seq(kiw)[:, :, :IDX_DIM],
                       h_s5[:, :S5_W].reshape(bs, S5_GROUPS, S5_STATE),
                       h_s5[:, S5_W:].reshape(bs, S5_GROUPS, S5_STATE), st_rw, cols[:, ss - 1]))

    stack = lambda outs, i: jnp.stack([o[i] for o in outs])
    return ((xp.reshape(bp, sp, d), xs.reshape(bs, ss, d))
            + tuple(stack(outs_p, i) for i in range(7)) + tuple(stack(outs_s, i) for i in range(7)))
```
